```python
import math
import jax, jax.numpy as jnp
from jax import lax
import numpy as np

D_MODEL = 2048
BATCH = 4
SEQ = 2048
DEPTH = 4
DEC_BATCH = 128
DEC_SEQ = 8
PAST_LEN = 16384
PAGE_SIZE = 128

N_META = 16
N_EVEN = (DEPTH + 1) // 2
N_ODD = DEPTH // 2
D_A = D_MODEL // 2
H_A = 8
BS_A = D_A // H_A
CONV_A = 4
C_RG = 8.0
D_B = D_MODEL // 2
H_B = 8
DK_B = D_B // H_B
DV_B = D_B // H_B
CHUNK_B = 64
LB_FLOOR = 1e-30
D_C = D_MODEL // 2
GROUP_C = 16
G_C = D_C // GROUP_C
P_C = 64
D_FF = ((8 * D_MODEL // 3 + 127) // 128) * 128
CONV_F = 3
EPS = 1e-6
EVEN_SPLITS = (D_A, D_A, D_B, D_B, D_B, D_B)
EVEN_SPLIT_IDX = tuple(int(v) for v in np.cumsum(EVEN_SPLITS)[:-1])

kernel_name = "hybrid_rglru_hgrn2_s5_convffn_step"

F32 = jnp.float32


def rmsnorm(x, g):
    xf = x.astype(F32)
    y = xf * lax.rsqrt(jnp.mean(xf * xf, axis=-1, keepdims=True) + EPS)
    return (y * g.astype(F32)).astype(x.dtype)


def causal_dwconv(x, buf, w, b):
    width = w.shape[0]
    T = x.shape[1]
    xp = jnp.concatenate([buf.astype(x.dtype), x], axis=1)
    y = jnp.zeros_like(x) + b.astype(x.dtype)
    for k in range(width):
        y = y + w[k].astype(x.dtype) * xp[:, k:k + T]
    return y, xp[:, T:]


def linear_scan(a, b):
    def comb(l, r):
        al, bl = l
        ar, br = r
        return al * ar, ar * bl + br
    _, h = lax.associative_scan(comb, (a, b), axis=1)
    return h


def complex_scan(ar, ai, br, bi):
    def comb(l, r):
        a1r, a1i, b1r, b1i = l
        a2r, a2i, b2r, b2i = r
        return (a2r * a1r - a2i * a1i, a2r * a1i + a2i * a1r,
                a2r * b1r - a2i * b1i + b2r, a2r * b1i + a2i * b1r + b2i)
    _, _, sr, si = lax.associative_scan(comb, (ar, ai, br, bi), axis=1)
    return sr, si


def rg_lru(xc, h0, w_r, b_r, w_i, b_i, lam, reset_first):
    B_, T, _ = xc.shape
    xf = xc.astype(F32)
    xh = xf.reshape(B_, T, H_A, BS_A)
    r = jax.nn.sigmoid(jnp.einsum('bthi,hij->bthj', xh, w_r.astype(F32)) + b_r.astype(F32)).reshape(B_, T, D_A)
    ig = jax.nn.sigmoid(jnp.einsum('bthi,hij->bthj', xh, w_i.astype(F32)) + b_i.astype(F32)).reshape(B_, T, D_A)
    log_a = -C_RG * r * jax.nn.softplus(-lam.astype(F32))
    a = jnp.exp(log_a)
    mult = jnp.sqrt(-jnp.expm1(2.0 * log_a))
    if reset_first:
        mult = mult.at[:, 0].set(1.0)
    b = mult * ig * xf
    b = b.at[:, 0].add(a[:, 0] * h0.astype(F32))
    h = linear_scan(a, b)
    return h, h[:, -1]


def gla_chunk(S0, q, logf, k, v):
    C = q.shape[1]
    bcum = jnp.cumsum(logf, axis=1)
    causal = jnp.tril(jnp.ones((C, C), dtype=bool))[None, :, :, None, None]
    diff = bcum[:, :, None] - bcum[:, None, :]
    decay = jnp.where(causal, jnp.exp(jnp.where(causal, diff, 0.0)), 0.0)
    attn = jnp.einsum('bthk,btshk,bshk->bhts', q, decay, k)
    o = jnp.einsum('bhts,bshv->bthv', attn, v) + jnp.einsum('bthk,bhkv->bthv', q * jnp.exp(bcum), S0)
    b_end = bcum[:, -1]
    kd = k * jnp.exp(b_end[:, None] - bcum)
    S = jnp.exp(b_end)[..., None] * S0 + jnp.einsum('bshk,bshv->bhkv', kd, v)
    return S, o


def hgrn2(q, fz, v, g, S0, lb, gn, is_prompt):
    B_, T, _ = q.shape
    qf = q.astype(F32).reshape(B_, T, H_B, DK_B)
    fzf = fz.astype(F32).reshape(B_, T, H_B, DK_B)
    vf = v.astype(F32).reshape(B_, T, H_B, DV_B)
    lbh = jnp.clip(lb.astype(F32), 0.0, 1.0).reshape(H_B, DK_B)
    logf = jnp.logaddexp(jnp.log(jnp.maximum(lbh, LB_FLOOR)), jnp.log1p(-lbh) + jax.nn.log_sigmoid(fzf))
    k = (1.0 - lbh) * jax.nn.sigmoid(-fzf)
    if is_prompt:
        S, o_meta = gla_chunk(S0, qf[:, :N_META], logf[:, :N_META], k[:, :N_META], vf[:, :N_META])
        n = (T - N_META) // CHUNK_B

        def to_chunks(t):
            return t[:, N_META:].reshape(B_, n, CHUNK_B, H_B, t.shape[-1]).swapaxes(0, 1)

        def step(S_c, inp):
            return gla_chunk(S_c, *inp)

        S, o_c = lax.scan(step, S, (to_chunks(qf), to_chunks(logf), to_chunks(k), to_chunks(vf)))
        o_real = o_c.swapaxes(0, 1).reshape(B_, T - N_META, H_B, DV_B)
        o = jnp.concatenate([o_meta, o_real], axis=1)
    else:
        S, o = gla_chunk(S0, qf, logf, k, vf)
    o = o * lax.rsqrt(jnp.mean(o * o, axis=-1, keepdims=True) + EPS)
    o = o.reshape(B_, T, D_B) * gn.astype(F32) * jax.nn.silu(g.astype(F32))
    return o, S


def s5(u, s0_re, s0_im, lam_re, lam_im, log_dt, bmat_re, bmat_im, cmat_re, cmat_im, d_skip):
    B_, T, _ = u.shape
    uf = u.astype(F32).reshape(B_, T, G_C, GROUP_C)
    lr = lam_re.astype(F32)
    li = lam_im.astype(F32)
    dt = jnp.exp(log_dt.astype(F32))[:, None]
    mag = jnp.exp(lr * dt)
    ar = mag * jnp.cos(li * dt)
    ai = mag * jnp.sin(li * dt)
    den = lr * lr + li * li
    zr = ((ar - 1.0) * lr + ai * li) / den
    zi = (ai * lr - (ar - 1.0) * li) / den
    br_ = bmat_re.astype(F32)
    bi_ = bmat_im.astype(F32)
    bb_re = zr[..., None] * br_ - zi[..., None] * bi_
    bb_im = zr[..., None] * bi_ + zi[..., None] * br_
    xr = jnp.einsum('btgc,gpc->btgp', uf, bb_re)
    xi = jnp.einsum('btgc,gpc->btgp', uf, bb_im)
    s0r = s0_re.astype(F32)
    s0i = s0_im.astype(F32)
    xr = xr.at[:, 0].add(ar * s0r - ai * s0i)
    xi = xi.at[:, 0].add(ar * s0i + ai * s0r)
    sr, si = complex_scan(jnp.broadcast_to(ar, xr.shape), jnp.broadcast_to(ai, xr.shape), xr, xi)
    y = (jnp.einsum('btgp,gcp->btgc', sr, cmat_re.astype(F32))
         - jnp.einsum('btgp,gcp->btgc', si, cmat_im.astype(F32))
         + d_skip.astype(F32).reshape(G_C, GROUP_C) * uf)
    return y.reshape(B_, T, D_C), sr[:, -1], si[:, -1]


def conv_ffn(x, buf, g, w_up, w_gate, cw, cb, w_down):
    h = rmsnorm(x, g)
    up = h @ w_up.astype(x.dtype)
    upc, new_buf = causal_dwconv(up, buf, cw, cb)
    act = jax.nn.gelu(upc) * (h @ w_gate.astype(x.dtype))
    return x + act @ w_down.astype(x.dtype), new_buf


def trunk(x, a_conv, a_h, b_S, c_re, c_im, f_conv, is_prompt,
          g_mix, w_in_e, conv_a_w, conv_a_b, w_ra, b_ra, w_ia, b_ia, lam_a, lb_all, gn_b, w_out_e,
          w_in_c, lam_re, lam_im, log_dt, bmat_re, bmat_im, cmat_re, cmat_im, d_skip, w_glu_v, w_glu_g,
          g_ffn, w_up, w_gate, conv_f_w, conv_f_b, w_down, g_final):
    n_a_conv, n_a_h, n_b_S, n_c_re, n_c_im, n_f = [], [], [], [], [], []
    dt_ = x.dtype
    for l in range(DEPTH):
        h = rmsnorm(x, g_mix[l])
        if l % 2 == 0:
            i = l // 2
            z = h @ w_in_e[i].astype(dt_)
            xa, ga, q, fz, v, gb = jnp.split(z, EVEN_SPLIT_IDX, axis=-1)
            xc, nb = causal_dwconv(xa, a_conv[i], conv_a_w[i], conv_a_b[i])
            ha, hl = rg_lru(xc, a_h[i], w_ra[i], b_ra[i], w_ia[i], b_ia[i], lam_a[i], is_prompt)
            out_a = ha * jax.nn.gelu(ga.astype(F32))
            out_b, S = hgrn2(q, fz, v, gb, b_S[i].astype(F32), lb_all[i], gn_b[i], is_prompt)
            mix = jnp.concatenate([out_a, out_b], axis=-1).astype(dt_) @ w_out_e[i].astype(dt_)
            n_a_conv.append(nb)
            n_a_h.append(hl)
            n_b_S.append(S)
        else:
            j = l // 2
            u = h @ w_in_c[j].astype(dt_)
            y, sr, si = s5(u, c_re[j], c_im[j], lam_re[j], lam_im[j], log_dt[j],
                           bmat_re[j], bmat_im[j], cmat_re[j], cmat_im[j], d_skip[j])
            y = jax.nn.gelu(y).astype(dt_)
            mix = (y @ w_glu_v[j].astype(dt_)) * jax.nn.sigmoid(y @ w_glu_g[j].astype(dt_))
            n_c_re.append(sr)
            n_c_im.append(si)
        x = x + mix
        x, fb = conv_ffn(x, f_conv[l], g_ffn[l], w_up[l], w_gate[l], conv_f_w[l], conv_f_b[l], w_down[l])
        n_f.append(fb)
    y = rmsnorm(x, g_final)
    return (y, jnp.stack(n_a_conv), jnp.stack(n_a_h), jnp.stack(n_b_S),
            jnp.stack(n_c_re), jnp.stack(n_c_im), jnp.stack(n_f))


def setup_inputs(seed: int = 0) -> dict:
    key = jax.random.key(seed)
    ks = iter(jax.random.split(key, 64))

    def nrm(shape, scale):
        return scale * jax.random.normal(next(ks), shape, F32)

    u_a = jax.random.uniform(next(ks), (N_EVEN, D_A), F32, 0.9, 0.999)
    a_base = u_a ** (1.0 / C_RG)
    lam_a = jnp.log(a_base) - jnp.log1p(-a_base)
    lam_im = math.pi * jnp.arange(P_C, dtype=F32)[None, None, :] + nrm((N_ODD, G_C, P_C), 0.01)
    log_dt = jax.random.uniform(next(ks), (N_ODD, G_C), F32, math.log(0.001), math.log(0.1))
    return {
        "x_prompt": nrm((BATCH, SEQ, D_MODEL), 1.0),
        "x_sample": nrm((DEC_BATCH, DEC_SEQ, D_MODEL), 1.0),
        "state_a_conv": nrm((N_EVEN, DEC_BATCH, CONV_A - 1, D_A), 1.0),
        "state_a_h": nrm((N_EVEN, DEC_BATCH, D_A), 0.5),
        "state_b_S": nrm((N_EVEN, DEC_BATCH, H_B, DK_B, DV_B), 0.5),
        "state_c_re": nrm((N_ODD, DEC_BATCH, G_C, P_C), 0.3),
        "state_c_im": nrm((N_ODD, DEC_BATCH, G_C, P_C), 0.3),
        "state_ffn_conv": nrm((DEPTH, DEC_BATCH, CONV_F - 1, D_FF), 1.0),
        "meta_tokens": nrm((N_META, D_MODEL), 1.0),
        "g_mix": 1.0 + nrm((DEPTH, D_MODEL), 0.05),
        "w_in_e": nrm((N_EVEN, D_MODEL, sum(EVEN_SPLITS)), D_MODEL ** -0.5),
        "conv_a_w": nrm((N_EVEN, CONV_A, D_A), CONV_A ** -0.5),
        "conv_a_b": nrm((N_EVEN, D_A), 0.01),
        "w_ra": nrm((N_EVEN, H_A, BS_A, BS_A), BS_A ** -0.5),
        "b_ra": nrm((N_EVEN, H_A, BS_A), 0.01),
        "w_ia": nrm((N_EVEN, H_A, BS_A, BS_A), BS_A ** -0.5),
        "b_ia": nrm((N_EVEN, H_A, BS_A), 0.01),
        "lam_a": lam_a,
        "lb_logits": nrm((N_EVEN, D_B), 1.0),
        "gn_b": 1.0 + nrm((N_EVEN, D_B), 0.05),
        "w_out_e": nrm((N_EVEN, D_A + D_B, D_MODEL), (D_A + D_B) ** -0.5),
        "w_in_c": nrm((N_ODD, D_MODEL, D_C), D_MODEL ** -0.5),
        "lam_re": -0.5 + nrm((N_ODD, G_C, P_C), 0.01),
        "lam_im": lam_im,
        "log_dt": log_dt,
        "bmat_re": nrm((N_ODD, G_C, P_C, GROUP_C), (2 * GROUP_C) ** -0.5),
        "bmat_im": nrm((N_ODD, G_C, P_C, GROUP_C), (2 * GROUP_C) ** -0.5),
        "cmat_re": nrm((N_ODD, G_C, GROUP_C, P_C), P_C ** -0.5),
        "cmat_im": nrm((N_ODD, G_C, GROUP_C, P_C), P_C ** -0.5),
        "d_skip": nrm((N_ODD, D_C), 0.5),
        "w_glu_v": nrm((N_ODD, D_C, D_MODEL), D_C ** -0.5),
        "w_glu_g": nrm((N_ODD, D_C, D_MODEL), D_C ** -0.5),
        "g_ffn": 1.0 + nrm((DEPTH, D_MODEL), 0.05),
        "w_up": nrm((DEPTH, D_MODEL, D_FF), D_MODEL ** -0.5),
        "w_gate": nrm((DEPTH, D_MODEL, D_FF), D_MODEL ** -0.5),
        "conv_f_w": nrm((DEPTH, CONV_F, D_FF), CONV_F ** -0.5),
        "conv_f_b": nrm((DEPTH, D_FF), 0.01),
        "w_down": nrm((DEPTH, D_FF, D_MODEL), D_FF ** -0.5),
        "g_final": 1.0 + nrm((D_MODEL,), 0.05),
    }


def reference(x_prompt, x_sample, state_a_conv, state_a_h, state_b_S, state_c_re, state_c_im, state_ffn_conv,
              meta_tokens, g_mix, w_in_e, conv_a_w, conv_a_b, w_ra, b_ra, w_ia, b_ia, lam_a, lb_logits, gn_b,
              w_out_e, w_in_c, lam_re, lam_im, log_dt, bmat_re, bmat_im, cmat_re, cmat_im, d_skip,
              w_glu_v, w_glu_g, g_ffn, w_up, w_gate, conv_f_w, conv_f_b, w_down, g_final):
    sm = jax.nn.softmax(lb_logits.astype(F32), axis=0)
    lb_all = jnp.clip(jnp.cumsum(sm, axis=0) - sm[0:1], 0.0, 1.0)
    B_ = x_prompt.shape[0]
    dtp = x_prompt.dtype
    meta = jnp.broadcast_to(meta_tokens.astype(dtp)[None], (B_, N_META, D_MODEL))
    xp = jnp.concatenate([meta, x_prompt], axis=1)
    z_a_conv = jnp.zeros((N_EVEN, B_, CONV_A - 1, D_A), dtp)
    z_a_h = jnp.zeros((N_EVEN, B_, D_A), F32)
    z_b_S = jnp.zeros((N_EVEN, B_, H_B, DK_B, DV_B), F32)
    z_c = jnp.zeros((N_ODD, B_, G_C, P_C), F32)
    z_f = jnp.zeros((DEPTH, B_, CONV_F - 1, D_FF), dtp)
    yp, p_a_conv, p_a_h, p_b_S, p_c_re, p_c_im, p_f_conv = trunk(
        xp, z_a_conv, z_a_h, z_b_S, z_c, z_c, z_f, True,
        g_mix, w_in_e, conv_a_w, conv_a_b, w_ra, b_ra, w_ia, b_ia, lam_a, lb_all, gn_b, w_out_e,
        w_in_c, lam_re, lam_im, log_dt, bmat_re, bmat_im, cmat_re, cmat_im, d_skip, w_glu_v, w_glu_g,
        g_ffn, w_up, w_gate, conv_f_w, conv_f_b, w_down, g_final)
    y_prompt = yp[:, N_META:]
    y_sample, s_a_conv, s_a_h, s_b_S, s_c_re, s_c_im, s_f_conv = trunk(
        x_sample, state_a_conv, state_a_h, state_b_S, state_c_re, state_c_im, state_ffn_conv, False,
        g_mix, w_in_e, conv_a_w, conv_a_b, w_ra, b_ra, w_ia, b_ia, lam_a, lb_all, gn_b, w_out_e,
        w_in_c, lam_re, lam_im, log_dt, bmat_re, bmat_im, cmat_re, cmat_im, d_skip, w_glu_v, w_glu_g,
        g_ffn, w_up, w_gate, conv_f_w, conv_f_b, w_down, g_final)
    return (y_prompt, y_sample,
            p_a_conv, p_a_h, p_b_S, p_c_re, p_c_im, p_f_conv,
            s_a_conv, s_a_h, s_b_S, s_c_re, s_c_im, s_f_conv)
```

```python
import functools
import math
from typing import NamedTuple

import jax
import jax.numpy as jnp
from jax import lax
from jax.experimental import pallas as pl
from jax.experimental.pallas import tpu as pltpu

F32 = jnp.float32
BF16 = jnp.bfloat16

D_MODEL = 2048
DEPTH = 4
N_META = 16
D_A = 1024
H_A = 8
CONV_A = 4
C_RG = 8.0
D_B = 1024
H_B = 8
HEAD = 128
LB_FLOOR = 1e-30
D_C = 1024
GROUP_C = 16
G_C = 64
P_C = 64
STATE_C = G_C * P_C
D_FF = 5504
D_FF_PAD = 5632
CONV_F = 3
EPS = 1e-6

LANE = 128
SUBLANE = 8
S5_LANES = LANE * P_C // GROUP_C
VMEM_LIMIT = 48 * 1024 * 1024


class Seg(NamedTuple):
    G: int
    sh: int
    T: int
    reset_first: bool
    share_state: bool


def _params(sem):
    return pltpu.CompilerParams(dimension_semantics=sem, vmem_limit_bytes=VMEM_LIMIT)


def _dot(a, b):
    return jnp.dot(a, b, preferred_element_type=F32)


def _dot_nt(a, b):
    return lax.dot_general(a, b, (((1,), (1,)), ((), ())), preferred_element_type=F32)


def _dot_tn(a, b):
    return lax.dot_general(a, b, (((0,), (0,)), ((), ())), preferred_element_type=F32)


def _softplus(x):
    return jnp.maximum(x, 0.0) + jnp.log1p(jnp.exp(-jnp.abs(x)))


def _delay(x, prev, k, sh):
    rows = x.shape[0]
    n = k * sh
    p = prev.shape[0]
    if sh % SUBLANE == 0:
        return jnp.concatenate([prev[p - n:], x[:rows - n]], axis=0)
    assert sh == 1
    y = pltpu.roll(x, n, 0)
    row = lax.broadcasted_iota(jnp.int32, x.shape, 0)
    for i in range(n):
        y = jnp.where(row == i, prev[p - n + i:p - n + i + 1], y)
    return y


def _norm_kernel(x_ref, g_ref, o_ref):
    x = x_ref[...]
    ms = jnp.mean(x * x, axis=-1, keepdims=True)
    o_ref[...] = (x * lax.rsqrt(ms + EPS) * g_ref[...]).astype(o_ref.dtype)


def _rmsnorm(x, g, out_dtype):
    rows = x.shape[0]
    tm = min(rows, 512)
    return pl.pallas_call(
        _norm_kernel,
        grid=(rows // tm,),
        in_specs=[pl.BlockSpec((tm, D_MODEL), lambda i: (i, 0)),
                  pl.BlockSpec((1, D_MODEL), lambda i: (0, 0))],
        out_specs=pl.BlockSpec((tm, D_MODEL), lambda i: (i, 0)),
        out_shape=jax.ShapeDtypeStruct((rows, D_MODEL), out_dtype),
        compiler_params=_params(("parallel",)),
        name="rmsnorm",
    )(x, g.reshape(1, D_MODEL))


def _mm_kernel(*refs, n_in, has_res, glu):
    xs, ws = refs[:n_in], refs[n_in:2 * n_in]
    o_ref = refs[-1]
    if glu:
        acc = _dot(xs[0][...], ws[0][...]) * jax.nn.sigmoid(_dot(xs[1][...], ws[1][...]))
    else:
        acc = _dot(xs[0][...], ws[0][...])
        for x_ref, w_ref in zip(xs[1:], ws[1:]):
            acc = acc + _dot(x_ref[...], w_ref[...])
    if has_res:
        acc = refs[2 * n_in][...] + acc
    o_ref[...] = acc.astype(o_ref.dtype)


def _matmul(xs, ws, kblocks, n_out, *, res=None, glu=False, out_dtype=F32, name="matmul"):
    rows = xs[0].shape[0]
    kmax = max(x.shape[1] for x in xs)
    tm = min(rows, 1024 if kmax <= D_MODEL else 512)
    tn = 512
    in_specs = [pl.BlockSpec((tm, x.shape[1]), lambda i, j: (i, 0)) for x in xs]
    in_specs += [pl.BlockSpec((x.shape[1], tn), functools.partial(lambda i, j, kb: (kb, j), kb=kb))
                 for x, kb in zip(xs, kblocks)]
    args = list(xs) + list(ws)
    if res is not None:
        in_specs.append(pl.BlockSpec((tm, tn), lambda i, j: (i, j)))
        args.append(res)
    return pl.pallas_call(
        functools.partial(_mm_kernel, n_in=len(xs), has_res=res is not None, glu=glu),
        grid=(rows // tm, n_out // tn),
        in_specs=in_specs,
        out_specs=pl.BlockSpec((tm, tn), lambda i, j: (i, j)),
        out_shape=jax.ShapeDtypeStruct((rows, n_out), out_dtype),
        compiler_params=_params(("parallel", "parallel")),
        name=name,
    )(*args)


def _ffn_up_kernel(*refs, sh, rows, nblk):
    if nblk > 1:
        h_ref, hp_ref, fb_ref, wu_ref, wg_ref, cw_ref, cb_ref, act_ref, tail_ref = refs
    else:
        h_ref, fb_ref, wu_ref, wg_ref, cw_ref, cb_ref, act_ref, tail_ref = refs
    h = h_ref[...]
    up = _dot(h, wu_ref[...])
    gate = _dot(h, wg_ref[...])
    prev = fb_ref[0]
    if nblk > 1:
        up_prev = _dot(hp_ref[...], wu_ref[...])[SUBLANE - 2:]
        prev = jnp.where(pl.program_id(0) % nblk == 0, prev, up_prev)
    upc = (cb_ref[...] + cw_ref[0:1, :] * _delay(up, prev, 2, sh)
           + cw_ref[1:2, :] * _delay(up, prev, 1, sh) + cw_ref[2:3, :] * up)
    act_ref[...] = (jax.nn.gelu(upc) * gate).astype(act_ref.dtype)
    tail_ref[0] = up[rows - 2 * sh:]


def _ffn_up(h, fbuf, w_up, w_gate, cw, cb, seg):
    G, sh, T = seg.G, seg.sh, seg.T
    tt = min(T, 1024 // sh)
    rows, nblk = tt * sh, T // tt
    tf = 512
    gi = (lambda i: 0) if seg.share_state else (lambda i: i // nblk)
    in_specs = [pl.BlockSpec((rows, D_MODEL), lambda i, j: (i, 0))]
    args = [h]
    if nblk > 1:
        per = rows // SUBLANE
        in_specs.append(pl.BlockSpec((SUBLANE, D_MODEL), lambda i, j: (jnp.maximum(i * per - 1, 0), 0)))
        args.append(h)
    in_specs += [pl.BlockSpec((1, 2 * sh, tf), lambda i, j: (gi(i), 0, j)),
                 pl.BlockSpec((D_MODEL, tf), lambda i, j: (0, j)),
                 pl.BlockSpec((D_MODEL, tf), lambda i, j: (0, j)),
                 pl.BlockSpec((CONV_F, tf), lambda i, j: (0, j)),
                 pl.BlockSpec((1, tf), lambda i, j: (0, j))]
    args += [fbuf, w_up, w_gate, cw, cb]
    act, tails = pl.pallas_call(
        functools.partial(_ffn_up_kernel, sh=sh, rows=rows, nblk=nblk),
        grid=(G * nblk, D_FF_PAD // tf),
        in_specs=in_specs,
        out_specs=[pl.BlockSpec((rows, tf), lambda i, j: (i, j)),
                   pl.BlockSpec((1, 2 * sh, tf), lambda i, j: (i, 0, j))],
        out_shape=[jax.ShapeDtypeStruct((G * T * sh, D_FF_PAD), BF16),
                   jax.ShapeDtypeStruct((G * nblk, 2 * sh, D_FF_PAD), F32)],
        compiler_params=_params(("arbitrary", "arbitrary")),
        name="ffn_up",
    )(*args)
    return act, tails[nblk - 1::nblk]


def _rglru_kernel(xa_ref, ga_ref, cst_ref, h0_ref, cw_ref, cb_ref, wr_ref, br_ref, wi_ref, bi_ref, lam_ref,
                  oa_ref, cso_ref, ho_ref, prev_sc, h_sc, a_sc, b_sc, *, sh, tt, heads, reset_first):
    tb = pl.program_id(2)
    rows = tt * sh

    @pl.when(tb == 0)
    def _():
        prev_sc[...] = cst_ref[0]
        h_sc[...] = h0_ref[0]

    xa = xa_ref[...]
    prev = prev_sc[...]
    xc = cb_ref[...] + cw_ref[CONV_A - 1:CONV_A, :] * xa
    for k in range(CONV_A - 1):
        xc = xc + cw_ref[k:k + 1, :] * _delay(xa, prev, CONV_A - 1 - k, sh)
    new_prev = xa[rows - (CONV_A - 1) * sh:]
    prev_sc[...] = new_prev
    cso_ref[0] = new_prev

    row = lax.broadcasted_iota(jnp.int32, (rows, HEAD), 0)
    for hd in range(heads):
        cs = slice(hd * HEAD, (hd + 1) * HEAD)
        xh = xc[:, cs]
        xb = xh.astype(BF16)
        r = jax.nn.sigmoid(_dot(xb, wr_ref[hd]) + br_ref[:, cs])
        ig = jax.nn.sigmoid(_dot(xb, wi_ref[hd]) + bi_ref[:, cs])
        log_a = -C_RG * r * _softplus(-lam_ref[:, cs])
        mult = jnp.sqrt(1.0 - jnp.exp(2.0 * log_a))
        if reset_first:
            mult = jnp.where((row < sh) & (tb == 0), 1.0, mult)
        a_sc[:, cs] = jnp.exp(log_a)
        b_sc[:, cs] = mult * ig * xh

    def step(t, h):
        sl = pl.ds(pl.multiple_of(t * sh, sh), sh)
        h = a_sc[sl, :] * h + b_sc[sl, :]
        b_sc[sl, :] = h
        return h

    h = lax.fori_loop(0, tt, step, h_sc[...], unroll=8)
    h_sc[...] = h
    ho_ref[0] = h
    oa_ref[...] = (b_sc[...] * jax.nn.gelu(ga_ref[...])).astype(oa_ref.dtype)


def _rglru(z, a_conv, a_h, cw, cb, w_r, b_r, w_i, b_i, lam, seg):
    G, sh, T = seg.G, seg.sh, seg.T
    tt = min(T, 512 // sh if sh == 1 else T)
    rows, nblk = tt * sh, T // tt
    heads = 2
    cwid = heads * HEAD
    ncb = D_A // cwid
    gi = (lambda g: 0) if seg.share_state else (lambda g: g)
    row_blk = lambda c, g, tb: g * nblk + tb
    vec = lambda n: pl.BlockSpec((n, cwid), lambda c, g, tb: (0, c))
    return pl.pallas_call(
        functools.partial(_rglru_kernel, sh=sh, tt=tt, heads=heads, reset_first=seg.reset_first),
        grid=(ncb, G, nblk),
        in_specs=[pl.BlockSpec((rows, cwid), lambda c, g, tb: (row_blk(c, g, tb), c)),
                  pl.BlockSpec((rows, cwid), lambda c, g, tb: (row_blk(c, g, tb), ncb + c)),
                  pl.BlockSpec((1, (CONV_A - 1) * sh, cwid), lambda c, g, tb: (gi(g), 0, c)),
                  pl.BlockSpec((1, sh, cwid), lambda c, g, tb: (gi(g), 0, c)),
                  vec(CONV_A), vec(1),
                  pl.BlockSpec((heads, HEAD, HEAD), lambda c, g, tb: (c, 0, 0)), vec(1),
                  pl.BlockSpec((heads, HEAD, HEAD), lambda c, g, tb: (c, 0, 0)), vec(1),
                  vec(1)],
        out_specs=[pl.BlockSpec((rows, cwid), lambda c, g, tb: (row_blk(c, g, tb), c)),
                   pl.BlockSpec((1, (CONV_A - 1) * sh, cwid), lambda c, g, tb: (g, 0, c)),
                   pl.BlockSpec((1, sh, cwid), lambda c, g, tb: (g, 0, c))],
        out_shape=[jax.ShapeDtypeStruct((G * T * sh, D_A), BF16),
                   jax.ShapeDtypeStruct((G, (CONV_A - 1) * sh, D_A), F32),
                   jax.ShapeDtypeStruct((G, sh, D_A), F32)],
        scratch_shapes=[pltpu.VMEM(((CONV_A - 1) * sh, cwid), F32),
                        pltpu.VMEM((sh, cwid), F32),
                        pltpu.VMEM((rows, cwid), F32),
                        pltpu.VMEM((rows, cwid), F32)],
        compiler_params=_params(("arbitrary", "arbitrary", "arbitrary")),
        name="rglru",
    )(z, z, a_conv, a_h, cw, cb.reshape(1, D_A), w_r, b_r.reshape(1, D_A), w_i, b_i.reshape(1, D_A),
      lam.reshape(1, D_A))


def _cumsum_rows(x):
    n = x.shape[0]
    row = lax.broadcasted_iota(jnp.int32, x.shape, 0)
    d = 1
    while d < n:
        x = x + jnp.where(row >= d, pltpu.roll(x, d, 0), 0.0)
        d *= 2
    return x


def _gla_pairwise(q, bc, k, v):
    n = q.shape[0]
    row = lax.broadcasted_iota(jnp.int32, (n, 1), 0)
    o = jnp.zeros((n, HEAD), F32)
    for s in range(n):
        diff = jnp.where(row >= s, bc - bc[s:s + 1], 0.0)
        w = q * k[s:s + 1] * jnp.exp(diff)
        a = jnp.where(row >= s, jnp.sum(w, axis=-1, keepdims=True), 0.0)
        o = o + a * v[s:s + 1]
    return o


def _gla_chunk(q, logf, k, v, st, base):
    c = q.shape[0]
    bc = _cumsum_rows(logf)
    o_state = _dot_nt((q * jnp.exp(bc)).astype(BF16), st.astype(BF16))
    parts = [_gla_pairwise(q[i:i + base], bc[i:i + base], k[i:i + base], v[i:i + base])
             for i in range(0, c, base)]
    s = base
    while s < c:
        for p in range(0, c, 2 * s):
            bm = bc[p + s - 1:p + s]
            qh = q[p + s:p + 2 * s] * jnp.exp(bc[p + s:p + 2 * s] - bm)
            kh = k[p:p + s] * jnp.exp(bm - bc[p:p + s])
            att = _dot_nt(qh.astype(BF16), kh.astype(BF16))
            upd = _dot(att.astype(BF16), v[p:p + s].astype(BF16))
            for i in range(s // base):
                idx = (p + s) // base + i
                parts[idx] = parts[idx] + upd[i * base:(i + 1) * base]
        s *= 2
    o = o_state + (parts[0] if len(parts) == 1 else jnp.concatenate(parts, axis=0))
    b_end = bc[c - 1:c]
    kd = k * jnp.exp(b_end - bc)
    st_new = st * jnp.exp(b_end) + _dot_tn(v.astype(BF16), kd.astype(BF16))
    return o, st_new


def _hgrn2_gates(fz, lb):
    la = jnp.log(jnp.maximum(lb, LB_FLOOR))
    lbv = jnp.log1p(-lb) + jnp.minimum(fz, 0.0) - jnp.log1p(jnp.exp(-jnp.abs(fz)))
    logf = jnp.maximum(la, lbv) + jnp.log1p(jnp.exp(-jnp.abs(la - lbv)))
    k = (1.0 - lb) * jax.nn.sigmoid(-fz)
    return logf, k


def _hgrn2_out(o, gn, gb):
    o = o * lax.rsqrt(jnp.mean(o * o, axis=-1, keepdims=True) + EPS)
    return o * gn * jax.nn.silu(gb)


def _hgrn2_long_kernel(q_ref, fz_ref, v_ref, gb_ref, s0_ref, lb_ref, gn_ref, ob_ref, so_ref, st_sc,
                       *, tt, chunk, base, nblk):
    tb = pl.program_id(2)

    @pl.when(tb == 0)
    def _():
        st_sc[...] = s0_ref[0, 0].T

    lb = lb_ref[...]

    def body(ci, carry):
        sl = pl.ds(pl.multiple_of(ci * chunk, chunk), chunk)
        fz = fz_ref[sl, :]
        logf, k = _hgrn2_gates(fz, lb)
        o, st_new = _gla_chunk(q_ref[sl, :], logf, k, v_ref[sl, :], st_sc[...], base)
        st_sc[...] = st_new
        ob_ref[sl, :] = _hgrn2_out(o, gn_ref[...], gb_ref[sl, :]).astype(ob_ref.dtype)
        return carry

    lax.fori_loop(0, tt // chunk, body, 0)

    @pl.when(tb == nblk - 1)
    def _():
        so_ref[0, 0] = st_sc[...].T


def _hgrn2_wide_kernel(q_ref, fz_ref, v_ref, gb_ref, s0_ref, lb_ref, gn_ref, ob_ref, so_ref, o_sc,
                       *, sh, T, nsb):
    sb = pl.program_id(1)
    lb = lb_ref[...]

    def body(jj, carry):
        sl = pl.ds(sb * nsb + jj, T, stride=sh)
        fz = fz_ref[sl, :]
        logf, k = _hgrn2_gates(fz, lb)
        o, st_new = _gla_chunk(q_ref[sl, :], logf, k, v_ref[sl, :], s0_ref[jj, 0].T, T)
        so_ref[jj, 0] = st_new.T
        o_sc[sl, :] = o
        return carry

    lax.fori_loop(0, nsb, body, 0)

    @pl.when(sb == sh // nsb - 1)
    def _():
        ob_ref[...] = _hgrn2_out(o_sc[...], gn_ref[...], gb_ref[...]).astype(ob_ref.dtype)


def _hgrn2(z, s0, lb, gn, seg):
    G, sh, T = seg.G, seg.sh, seg.T
    q0, f0, v0, g0 = 2 * D_A // HEAD, (2 * D_A + D_B) // HEAD, (2 * D_A + 2 * D_B) // HEAD, (2 * D_A + 3 * D_B) // HEAD
    nseq = G * sh
    out_shape = [jax.ShapeDtypeStruct((G * T * sh, D_B), BF16),
                 jax.ShapeDtypeStruct((nseq, H_B, HEAD, HEAD), F32)]
    lbr, gnr = lb.reshape(1, D_B), gn.reshape(1, D_B)
    if sh == 1:
        tt = min(T, 256)
        chunk = min(tt, 64)
        nblk = T // tt
        gi = (lambda g: 0) if seg.share_state else (lambda g: g)
        col = lambda c0: pl.BlockSpec((tt, HEAD), lambda h, g, tb: (g * nblk + tb, c0 + h))
        return pl.pallas_call(
            functools.partial(_hgrn2_long_kernel, tt=tt, chunk=chunk, base=min(chunk, 16), nblk=nblk),
            grid=(H_B, G, nblk),
            in_specs=[col(q0), col(f0), col(v0), col(g0),
                      pl.BlockSpec((1, 1, HEAD, HEAD), lambda h, g, tb: (gi(g), h, 0, 0)),
                      pl.BlockSpec((1, HEAD), lambda h, g, tb: (0, h)),
                      pl.BlockSpec((1, HEAD), lambda h, g, tb: (0, h))],
            out_specs=[pl.BlockSpec((tt, HEAD), lambda h, g, tb: (g * nblk + tb, h)),
                       pl.BlockSpec((1, 1, HEAD, HEAD), lambda h, g, tb: (g, h, 0, 0))],
            out_shape=out_shape,
            scratch_shapes=[pltpu.VMEM((HEAD, HEAD), F32)],
            compiler_params=_params(("arbitrary", "arbitrary", "arbitrary")),
            name="hgrn2_long",
        )(z, z, z, z, s0, lbr, gnr)
    assert G == 1 and T == SUBLANE
    rows = T * sh
    nsb = min(sh, 32)
    col = lambda c0: pl.BlockSpec((rows, HEAD), lambda h, sb: (0, c0 + h))
    return pl.pallas_call(
        functools.partial(_hgrn2_wide_kernel, sh=sh, T=T, nsb=nsb),
        grid=(H_B, sh // nsb),
        in_specs=[col(q0), col(f0), col(v0), col(g0),
                  pl.BlockSpec((nsb, 1, HEAD, HEAD), lambda h, sb: (sb, h, 0, 0)),
                  pl.BlockSpec((1, HEAD), lambda h, sb: (0, h)),
                  pl.BlockSpec((1, HEAD), lambda h, sb: (0, h))],
        out_specs=[pl.BlockSpec((rows, HEAD), lambda h, sb: (0, h)),
                   pl.BlockSpec((nsb, 1, HEAD, HEAD), lambda h, sb: (sb, h, 0, 0))],
        out_shape=out_shape,
        scratch_shapes=[pltpu.VMEM((rows, HEAD), F32)],
        compiler_params=_params(("arbitrary", "arbitrary")),
        name="hgrn2_wide",
    )(z, z, z, z, s0, lbr, gnr)


def _s5_kernel(u_ref, sr0_ref, si0_ref, ar_ref, ai_ref, bre_ref, bim_ref, cre_ref, cim_ref, d_ref,
               y_ref, sro_ref, sio_ref, sr_sc, si_sc, xr_sc, xi_sc, *, sh, tt):
    tb = pl.program_id(2)

    @pl.when(tb == 0)
    def _():
        sr_sc[...] = sr0_ref[0]
        si_sc[...] = si0_ref[0]

    u = u_ref[...]
    ub = u.astype(BF16)
    xr_sc[...] = _dot(ub, bre_ref[0])
    xi_sc[...] = _dot(ub, bim_ref[0])
    ar = jnp.broadcast_to(ar_ref[...], (sh, S5_LANES))
    ai = jnp.broadcast_to(ai_ref[...], (sh, S5_LANES))

    def step(t, carry):
        sr, si = carry
        sl = pl.ds(pl.multiple_of(t * sh, sh), sh)
        nr = ar * sr - ai * si + xr_sc[sl, :]
        ni = ar * si + ai * sr + xi_sc[sl, :]
        xr_sc[sl, :] = nr
        xi_sc[sl, :] = ni
        return nr, ni

    sr, si = lax.fori_loop(0, tt, step, (sr_sc[...], si_sc[...]), unroll=8)
    sr_sc[...] = sr
    si_sc[...] = si
    sro_ref[0] = sr
    sio_ref[0] = si
    y = (_dot(xr_sc[...].astype(BF16), cre_ref[0]) - _dot(xi_sc[...].astype(BF16), cim_ref[0])
         + d_ref[...] * u)
    y_ref[...] = jax.nn.gelu(y).astype(y_ref.dtype)


def _s5(u, s_re, s_im, prm, seg):
    G, sh, T = seg.G, seg.sh, seg.T
    tt = min(T, 256 // sh if sh == 1 else T)
    rows, nblk = tt * sh, T // tt
    ncb = D_C // LANE
    gi = (lambda g: 0) if seg.share_state else (lambda g: g)
    st_in = pl.BlockSpec((1, sh, S5_LANES), lambda c, g, tb: (gi(g), 0, c))
    st_out = pl.BlockSpec((1, sh, S5_LANES), lambda c, g, tb: (g, 0, c))
    vec = pl.BlockSpec((1, S5_LANES), lambda c, g, tb: (0, c))
    return pl.pallas_call(
        functools.partial(_s5_kernel, sh=sh, tt=tt),
        grid=(ncb, G, nblk),
        in_specs=[pl.BlockSpec((rows, LANE), lambda c, g, tb: (g * nblk + tb, c)),
                  st_in, st_in, vec, vec,
                  pl.BlockSpec((1, LANE, S5_LANES), lambda c, g, tb: (c, 0, 0)),
                  pl.BlockSpec((1, LANE, S5_LANES), lambda c, g, tb: (c, 0, 0)),
                  pl.BlockSpec((1, S5_LANES, LANE), lambda c, g, tb: (c, 0, 0)),
                  pl.BlockSpec((1, S5_LANES, LANE), lambda c, g, tb: (c, 0, 0)),
                  pl.BlockSpec((1, LANE), lambda c, g, tb: (0, c))],
        out_specs=[pl.BlockSpec((rows, LANE), lambda c, g, tb: (g * nblk + tb, c)), st_out, st_out],
        out_shape=[jax.ShapeDtypeStruct((G * T * sh, D_C), BF16),
                   jax.ShapeDtypeStruct((G, sh, STATE_C), F32),
                   jax.ShapeDtypeStruct((G, sh, STATE_C), F32)],
        scratch_shapes=[pltpu.VMEM((sh, S5_LANES), F32), pltpu.VMEM((sh, S5_LANES), F32),
                        pltpu.VMEM((rows, S5_LANES), F32), pltpu.VMEM((rows, S5_LANES), F32)],
        compiler_params=_params(("arbitrary", "arbitrary", "arbitrary")),
        name="s5",
    )(u, s_re, s_im, prm["ar"], prm["ai"], prm["bre"], prm["bim"], prm["cre"], prm["cim"], prm["d"])


def _s5_params(lam_re, lam_im, log_dt, bmat_re, bmat_im, cmat_re, cmat_im, d_skip):
    lr, li = lam_re.astype(F32), lam_im.astype(F32)
    dt = jnp.exp(log_dt.astype(F32))[:, None]
    mag = jnp.exp(lr * dt)
    ar = mag * jnp.cos(li * dt)
    ai = mag * jnp.sin(li * dt)
    den = lr * lr + li * li
    zr = ((ar - 1.0) * lr + ai * li) / den
    zi = (ai * lr - (ar - 1.0) * li) / den
    br_, bi_ = bmat_re.astype(F32), bmat_im.astype(F32)
    bb_re = zr[..., None] * br_ - zi[..., None] * bi_
    bb_im = zr[..., None] * bi_ + zi[..., None] * br_
    gpb = LANE // GROUP_C
    eye = jnp.eye(gpb, dtype=F32)

    def pack_b(bb):
        bb = bb.reshape(G_C // gpb, gpb, P_C, GROUP_C)
        return jnp.einsum("ag,bapc->bacgp", eye, bb).reshape(G_C // gpb, LANE, S5_LANES).astype(BF16)

    def pack_c(cm):
        cm = cm.astype(F32).reshape(G_C // gpb, gpb, GROUP_C, P_C)
        return jnp.einsum("ag,bacp->bapgc", eye, cm).reshape(G_C // gpb, S5_LANES, LANE).astype(BF16)

    return {"ar": ar.reshape(1, STATE_C), "ai": ai.reshape(1, STATE_C),
            "bre": pack_b(bb_re), "bim": pack_b(bb_im), "cre": pack_c(cmat_re), "cim": pack_c(cmat_im),
            "d": d_skip.astype(F32).reshape(1, D_C)}


def _trunk(x, seg, st, w):
    new = {k: [] for k in ("a_conv", "a_h", "b_S", "c_re", "c_im", "f_conv")}
    for l in range(DEPTH):
        h = _rmsnorm(x, w["g_mix"][l], BF16)
        if l % 2 == 0:
            i = l // 2
            z = _matmul([h], [w["w_in_e"][i]], [0], 6 * D_A, name="in_proj_even")
            out_a, nb, hl = _rglru(z, st["a_conv"][i], st["a_h"][i], w["conv_a_w"][i], w["conv_a_b"][i],
                                   w["w_ra"][i], w["b_ra"][i], w["w_ia"][i], w["b_ia"][i], w["lam_a"][i], seg)
            out_b, s_new = _hgrn2(z, st["b_S"][i], w["lb_all"][i], w["gn_b"][i], seg)
            x = _matmul([out_a, out_b], [w["w_out_e"][i]] * 2, [0, 1], D_MODEL, res=x, name="out_proj_even")
            new["a_conv"].append(nb)
            new["a_h"].append(hl)
            new["b_S"].append(s_new)
        else:
            j = l // 2
            u = _matmul([h], [w["w_in_c"][j]], [0], D_C, name="in_proj_odd")
            y, sr, si = _s5(u, st["c_re"][j], st["c_im"][j], w["s5"][j], seg)
            x = _matmul([y, y], [w["w_glu_v"][j], w["w_glu_g"][j]], [0, 0], D_MODEL, res=x, glu=True,
                        name="glu_odd")
            new["c_re"].append(sr)
            new["c_im"].append(si)
        h = _rmsnorm(x, w["g_ffn"][l], BF16)
        act, tail = _ffn_up(h, st["f_conv"][l], w["w_up"][l], w["w_gate"][l], w["conv_f_w"][l],
                            w["conv_f_b"][l], seg)
        x = _matmul([act], [w["w_down"][l]], [0], D_MODEL, res=x, name="ffn_down")
        new["f_conv"].append(tail)
    return _rmsnorm(x, w["g_final"], F32), new


def kernel(x_prompt, x_sample, state_a_conv, state_a_h, state_b_S, state_c_re, state_c_im, state_ffn_conv,
           meta_tokens, g_mix, w_in_e, conv_a_w, conv_a_b, w_ra, b_ra, w_ia, b_ia, lam_a, lb_logits, gn_b,
           w_out_e, w_in_c, lam_re, lam_im, log_dt, bmat_re, bmat_im, cmat_re, cmat_im, d_skip,
           w_glu_v, w_glu_g, g_ffn, w_up, w_gate, conv_f_w, conv_f_b, w_down, g_final):
    n_even, n_odd = w_in_e.shape[0], w_in_c.shape[0]
    batch, seq = x_prompt.shape[0], x_prompt.shape[1]
    dec_batch, dec_seq = x_sample.shape[0], x_sample.shape[1]
    fpad = D_FF_PAD - D_FF

    sm = jax.nn.softmax(lb_logits.astype(F32), axis=0)
    lb_all = jnp.clip(jnp.clip(jnp.cumsum(sm, axis=0) - sm[0:1], 0.0, 1.0), 0.0, 1.0)
    w = {
        "g_mix": g_mix, "g_ffn": g_ffn, "g_final": g_final,
        "w_in_e": w_in_e.astype(BF16), "w_out_e": w_out_e.astype(BF16),
        "conv_a_w": conv_a_w, "conv_a_b": conv_a_b,
        "w_ra": w_ra.astype(BF16), "b_ra": b_ra, "w_ia": w_ia.astype(BF16), "b_ia": b_ia, "lam_a": lam_a,
        "lb_all": lb_all, "gn_b": gn_b,
        "w_in_c": w_in_c.astype(BF16), "w_glu_v": w_glu_v.astype(BF16), "w_glu_g": w_glu_g.astype(BF16),
        "s5": [_s5_params(lam_re[j], lam_im[j], log_dt[j], bmat_re[j], bmat_im[j], cmat_re[j], cmat_im[j],
                          d_skip[j]) for j in range(n_odd)],
        "w_up": jnp.pad(w_up.astype(BF16), ((0, 0), (0, 0), (0, fpad))),
        "w_gate": jnp.pad(w_gate.astype(BF16), ((0, 0), (0, 0), (0, fpad))),
        "conv_f_w": jnp.pad(conv_f_w.astype(F32), ((0, 0), (0, 0), (0, fpad))),
        "conv_f_b": jnp.pad(conv_f_b.astype(F32), ((0, 0), (0, fpad))).reshape(DEPTH, 1, D_FF_PAD),
        "w_down": jnp.pad(w_down.astype(BF16), ((0, 0), (0, fpad), (0, 0))),
    }

    meta_seg = Seg(G=1, sh=1, T=N_META, reset_first=True, share_state=False)
    zero = {
        "a_conv": [jnp.zeros((1, CONV_A - 1, D_A), F32)] * n_even,
        "a_h": [jnp.zeros((1, 1, D_A), F32)] * n_even,
        "b_S": [jnp.zeros((1, H_B, HEAD, HEAD), F32)] * n_even,
        "c_re": [jnp.zeros((1, 1, STATE_C), F32)] * n_odd,
        "c_im": [jnp.zeros((1, 1, STATE_C), F32)] * n_odd,
        "f_conv": [jnp.zeros((1, CONV_F - 1, D_FF_PAD), F32)] * DEPTH,
    }
    _, meta_st = _trunk(meta_tokens.astype(F32), meta_seg, zero, w)

    p_seg = Seg(G=batch, sh=1, T=seq, reset_first=False, share_state=True)
    yp, p_st = _trunk(x_prompt.reshape(batch * seq, D_MODEL), p_seg, meta_st, w)

    s_seg = Seg(G=1, sh=dec_batch, T=dec_seq, reset_first=False, share_state=False)
    tmaj = lambda a: jnp.swapaxes(a, 0, 1)
    s_init = {
        "a_conv": [tmaj(state_a_conv[i]).reshape(1, (CONV_A - 1) * dec_batch, D_A) for i in range(n_even)],
        "a_h": [state_a_h[i].reshape(1, dec_batch, D_A) for i in range(n_even)],
        "b_S": [state_b_S[i] for i in range(n_even)],
        "c_re": [state_c_re[j].reshape(1, dec_batch, STATE_C) for j in range(n_odd)],
        "c_im": [state_c_im[j].reshape(1, dec_batch, STATE_C) for j in range(n_odd)],
        "f_conv": [jnp.pad(tmaj(state_ffn_conv[l]), ((0, 0), (0, 0), (0, fpad)))
                   .reshape(1, (CONV_F - 1) * dec_batch, D_FF_PAD) for l in range(DEPTH)],
    }
    ys, s_st = _trunk(tmaj(x_sample).reshape(dec_seq * dec_batch, D_MODEL), s_seg, s_init, w)

    y_prompt = yp.reshape(batch, seq, D_MODEL)
    y_sample = tmaj(ys.reshape(dec_seq, dec_batch, D_MODEL))
    p_out = (jnp.stack(p_st["a_conv"]),
             jnp.stack(p_st["a_h"]).reshape(n_even, batch, D_A),
             jnp.stack(p_st["b_S"]),
             jnp.stack(p_st["c_re"]).reshape(n_odd, batch, G_C, P_C),
             jnp.stack(p_st["c_im"]).reshape(n_odd, batch, G_C, P_C),
             jnp.stack(p_st["f_conv"])[..., :D_FF])
    s_out = (jnp.stack([tmaj(a.reshape(CONV_A - 1, dec_batch, D_A)) for a in s_st["a_conv"]]),
             jnp.stack(s_st["a_h"]).reshape(n_even, dec_batch, D_A),
             jnp.stack(s_st["b_S"]),
             jnp.stack(s_st["c_re"]).reshape(n_odd, dec_batch, G_C, P_C),
             jnp.stack(s_st["c_im"]).reshape(n_odd, dec_batch, G_C, P_C),
             jnp.stack([tmaj(a.reshape(CONV_F - 1, dec_batch, D_FF_PAD)) for a in s_st["f_conv"]])[..., :D_FF])
    return (y_prompt, y_sample) + p_out + s_out
```

```python
import functools
from typing import NamedTuple

import jax
import jax.numpy as jnp
from jax import lax
from jax.experimental import pallas as pl
from jax.experimental.pallas import tpu as pltpu

F32 = jnp.float32
BF16 = jnp.bfloat16

D_MODEL = 2048
DEPTH = 4
N_META = 16
D_A = 1024
CONV_A = 4
C_RG = 8.0
D_B = 1024
H_B = 8
HEAD = 128
LB_FLOOR = 1e-30
D_C = 1024
GROUP_C = 16
G_C = 64
P_C = 64
STATE_C = G_C * P_C
D_FF = 5504
D_FF_PAD = 5632
CONV_F = 3
EPS = 1e-6

LANE = 128
SUBLANE = 8
S5_LANES = LANE * P_C // GROUP_C
VMEM_LIMIT = 52 * 1024 * 1024


class Seg(NamedTuple):
    G: int
    sh: int
    T: int
    reset_first: bool
    share_state: bool


def _params(n_axes):
    return pltpu.CompilerParams(dimension_semantics=("arbitrary",) * n_axes, vmem_limit_bytes=VMEM_LIMIT)


def _dot(a, b):
    return jnp.dot(a, b, preferred_element_type=F32)


def _dot_nt(a, b):
    return lax.dot_general(a, b, (((1,), (1,)), ((), ())), preferred_element_type=F32)


def _dot_tn(a, b):
    return lax.dot_general(a, b, (((0,), (0,)), ((), ())), preferred_element_type=F32)


def _softplus(x):
    return jnp.maximum(x, 0.0) + jnp.log1p(jnp.exp(-jnp.abs(x)))


def _rms(x, g):
    ms = jnp.mean(x * x, axis=-1, keepdims=True)
    return x * lax.rsqrt(ms + EPS) * g


def _delay(x, prev, k, sh):
    rows = x.shape[0]
    n = k * sh
    p = prev.shape[0]
    if sh % SUBLANE == 0:
        return jnp.concatenate([prev[p - n:], x[:rows - n]], axis=0)
    assert sh == 1
    y = pltpu.roll(x, n, 0)
    row = lax.broadcasted_iota(jnp.int32, x.shape, 0)
    for i in range(n):
        y = jnp.where(row == i, prev[p - n + i:p - n + i + 1], y)
    return y


def _row_tile(x, i):
    return jnp.broadcast_to(x[i:i + 1], (SUBLANE, x.shape[1]))


def _norm_kernel(x_ref, g_ref, o_ref):
    o_ref[...] = _rms(x_ref[...], g_ref[...]).astype(o_ref.dtype)


def _rmsnorm(x, g):
    rows = x.shape[0]
    tm = min(rows, 512)
    return pl.pallas_call(
        _norm_kernel,
        grid=(rows // tm,),
        in_specs=[pl.BlockSpec((tm, D_MODEL), lambda i: (i, 0)),
                  pl.BlockSpec((1, D_MODEL), lambda i: (0, 0))],
        out_specs=pl.BlockSpec((tm, D_MODEL), lambda i: (i, 0)),
        out_shape=jax.ShapeDtypeStruct((rows, D_MODEL), F32),
        compiler_params=_params(1),
        name="rmsnorm",
    )(x, g.reshape(1, D_MODEL))


def _mm_kernel(*refs, n_in, has_res, glu, norm):
    xs, ws = list(refs[:n_in]), refs[n_in:2 * n_in]
    pos = 2 * n_in
    if norm:
        g_ref, h_sc = refs[pos], refs[-1]
        pos += 1

        @pl.when(pl.program_id(1) == 0)
        def _():
            h_sc[...] = _rms(xs[0][...], g_ref[...]).astype(BF16)

        xs[0] = h_sc
    o_ref = refs[pos + (1 if has_res else 0)]
    if glu:
        acc = _dot(xs[0][...], ws[0][...]) * jax.nn.sigmoid(_dot(xs[1][...], ws[1][...]))
    else:
        acc = _dot(xs[0][...], ws[0][...])
        for x_ref, w_ref in zip(xs[1:], ws[1:]):
            acc = acc + _dot(x_ref[...], w_ref[...])
    if has_res:
        acc = refs[pos][...] + acc
    o_ref[...] = acc.astype(o_ref.dtype)


def _matmul(xs, ws, n_out, *, res=None, glu=False, norm_g=None, name="matmul"):
    rows = xs[0].shape[0]
    kmax = max(x.shape[1] for x in xs)
    tm = min(rows, 1024 if kmax <= D_MODEL else 512)
    tn = 512
    in_specs = [pl.BlockSpec((tm, x.shape[1]), lambda i, j: (i, 0)) for x in xs]
    in_specs += [pl.BlockSpec((None, x.shape[1], tn), functools.partial(lambda i, j, l, kb: (l, kb, j), l=l, kb=kb))
                 for x, (_, l, kb) in zip(xs, ws)]
    args = list(xs) + [w for w, _, _ in ws]
    scratch = []
    if norm_g is not None:
        g, gl = norm_g
        in_specs.append(pl.BlockSpec((None, 1, D_MODEL), lambda i, j: (gl, 0, 0)))
        args.append(g)
        scratch.append(pltpu.VMEM((tm, D_MODEL), BF16))
    if res is not None:
        in_specs.append(pl.BlockSpec((tm, tn), lambda i, j: (i, j)))
        args.append(res)
    return pl.pallas_call(
        functools.partial(_mm_kernel, n_in=len(xs), has_res=res is not None, glu=glu, norm=norm_g is not None),
        grid=(rows // tm, n_out // tn),
        in_specs=in_specs,
        out_specs=pl.BlockSpec((tm, tn), lambda i, j: (i, j)),
        out_shape=jax.ShapeDtypeStruct((rows, n_out), F32),
        scratch_shapes=scratch,
        compiler_params=_params(2),
        name=name,
    )(*args)


def _ffn_up_kernel(*refs, sh, rows, nblk):
    if nblk > 1:
        x_ref, xp_ref, g_ref, fb_ref, wu_ref, wg_ref, cw_ref, cb_ref, act_ref, tail_ref, h_sc, hp_sc = refs
    else:
        x_ref, g_ref, fb_ref, wu_ref, wg_ref, cw_ref, cb_ref, act_ref, tail_ref, h_sc = refs

    @pl.when(pl.program_id(1) == 0)
    def _():
        h_sc[...] = _rms(x_ref[...], g_ref[...]).astype(BF16)
        if nblk > 1:
            hp_sc[...] = _rms(xp_ref[...], g_ref[...]).astype(BF16)

    h = h_sc[...]
    up = _dot(h, wu_ref[...])
    gate = _dot(h, wg_ref[...])
    prev = fb_ref[...]
    if nblk > 1:
        up_prev = _dot(hp_sc[...], wu_ref[...])[SUBLANE - 2:]
        prev = jnp.where(pl.program_id(0) % nblk == 0, prev, up_prev)
    upc = (cb_ref[...] + cw_ref[0:1, :] * _delay(up, prev, 2, sh)
           + cw_ref[1:2, :] * _delay(up, prev, 1, sh) + cw_ref[2:3, :] * up)
    act_ref[...] = (jax.nn.gelu(upc) * gate).astype(act_ref.dtype)
    tail_ref[0] = up[rows - 2 * sh:]


def _ffn_up(x, g, fbuf, w_up, w_gate, cw, cb, l, seg):
    G, sh, T = seg.G, seg.sh, seg.T
    tt = min(T, 1024 // sh)
    rows, nblk = tt * sh, T // tt
    tf = 512
    fb, fl = fbuf
    gi = (lambda i: 0) if seg.share_state else (lambda i: i // nblk)
    in_specs = [pl.BlockSpec((rows, D_MODEL), lambda i, j: (i, 0))]
    args = [x]
    scratch = [pltpu.VMEM((rows, D_MODEL), BF16)]
    if nblk > 1:
        per = rows // SUBLANE
        in_specs.append(pl.BlockSpec((SUBLANE, D_MODEL), lambda i, j: (jnp.maximum(i * per - 1, 0), 0)))
        args.append(x)
        scratch.append(pltpu.VMEM((SUBLANE, D_MODEL), BF16))
    in_specs += [pl.BlockSpec((None, 1, D_MODEL), lambda i, j: (l, 0, 0)),
                 pl.BlockSpec((None, None, 2 * sh, tf), lambda i, j: (fl, gi(i), 0, j)),
                 pl.BlockSpec((None, D_MODEL, tf), lambda i, j: (l, 0, j)),
                 pl.BlockSpec((None, D_MODEL, tf), lambda i, j: (l, 0, j)),
                 pl.BlockSpec((None, CONV_F, tf), lambda i, j: (l, 0, j)),
                 pl.BlockSpec((None, 1, tf), lambda i, j: (l, 0, j))]
    args += [g, fb, w_up, w_gate, cw, cb]
    act, tails = pl.pallas_call(
        functools.partial(_ffn_up_kernel, sh=sh, rows=rows, nblk=nblk),
        grid=(G * nblk, D_FF_PAD // tf),
        in_specs=in_specs,
        out_specs=[pl.BlockSpec((rows, tf), lambda i, j: (i, j)),
                   pl.BlockSpec((1, 2 * sh, tf), lambda i, j: (i, 0, j))],
        out_shape=[jax.ShapeDtypeStruct((G * T * sh, D_FF_PAD), BF16),
                   jax.ShapeDtypeStruct((G * nblk, 2 * sh, D_FF_PAD), F32)],
        scratch_shapes=scratch,
        compiler_params=_params(2),
        name="ffn_up",
    )(*args)
    return act, tails[nblk - 1::nblk]


def _rglru_kernel(xa_ref, ga_ref, cst_ref, h0_ref, cw_ref, cb_ref, wr_ref, br_ref, wi_ref, bi_ref, lam_ref,
                  oa_ref, cso_ref, ho_ref, prev_sc, h_sc, a_sc, b_sc, *, sh, tt, heads, reset_first):
    tb = pl.program_id(2)
    rows = tt * sh

    @pl.when(tb == 0)
    def _():
        prev_sc[...] = cst_ref[...]
        h_sc[...] = h0_ref[...]

    xa = xa_ref[...]
    prev = prev_sc[...]
    xc = cb_ref[...] + cw_ref[CONV_A - 1:CONV_A, :] * xa
    for k in range(CONV_A - 1):
        xc = xc + cw_ref[k:k + 1, :] * _delay(xa, prev, CONV_A - 1 - k, sh)
    new_prev = xa[rows - (CONV_A - 1) * sh:]
    prev_sc[...] = new_prev
    cso_ref[0] = new_prev

    row = lax.broadcasted_iota(jnp.int32, (rows, HEAD), 0)
    for hd in range(heads):
        cs = slice(hd * HEAD, (hd + 1) * HEAD)
        xh = xc[:, cs]
        xb = xh.astype(BF16)
        r = jax.nn.sigmoid(_dot(xb, wr_ref[hd]) + br_ref[:, cs])
        ig = jax.nn.sigmoid(_dot(xb, wi_ref[hd]) + bi_ref[:, cs])
        log_a = -C_RG * r * _softplus(-lam_ref[:, cs])
        mult = jnp.sqrt(1.0 - jnp.exp(2.0 * log_a))
        if reset_first:
            mult = jnp.where((row < sh) & (tb == 0), 1.0, mult)
        a_sc[:, cs] = jnp.exp(log_a)
        b_sc[:, cs] = mult * ig * xh

    if sh == 1:
        width = a_sc.shape[1]
        a3 = a_sc[...].reshape(rows // SUBLANE, SUBLANE, width)
        b3 = b_sc[...].reshape(rows // SUBLANE, SUBLANE, width)
        sub = lax.broadcasted_iota(jnp.int32, a3.shape, 1)
        d = 1
        while d < SUBLANE:
            keep = sub >= d
            b3 = b3 + a3 * jnp.where(keep, pltpu.roll(b3, d, 1), 0.0)
            a3 = a3 * jnp.where(keep, pltpu.roll(a3, d, 1), 1.0)
            d *= 2
        a_sc[...] = a3.reshape(rows, width)
        b_sc[...] = b3.reshape(rows, width)

        def tile_step(k, h):
            sl = pl.ds(pl.multiple_of(k * SUBLANE, SUBLANE), SUBLANE)
            ht = b_sc[sl, :] + a_sc[sl, :] * jnp.broadcast_to(h, (SUBLANE, width))
            b_sc[sl, :] = ht
            return ht[SUBLANE - 1:]

        h = lax.fori_loop(0, rows // SUBLANE, tile_step, h_sc[...], unroll=min(4, rows // SUBLANE))
    else:
        def step(t, h):
            sl = pl.ds(pl.multiple_of(t * sh, sh), sh)
            h = a_sc[sl, :] * h + b_sc[sl, :]
            b_sc[sl, :] = h
            return h

        h = lax.fori_loop(0, tt, step, h_sc[...], unroll=True)
    h_sc[...] = h
    ho_ref[0] = h
    oa_ref[...] = (b_sc[...] * jax.nn.gelu(ga_ref[...])).astype(oa_ref.dtype)


def _rglru(z, a_conv, a_h, prm, l, seg):
    G, sh, T = seg.G, seg.sh, seg.T
    tt = min(T, 512) if sh == 1 else T
    rows, nblk = tt * sh, T // tt
    heads = 8 if sh == 1 else 2
    cwid = heads * HEAD
    ncb = D_A // cwid
    (ac, acl), (ah, ahl) = a_conv, a_h
    gi = (lambda g: 0) if seg.share_state else (lambda g: g)
    row_blk = lambda c, g, tb: g * nblk + tb
    vec = lambda n: pl.BlockSpec((None, n, cwid), lambda c, g, tb: (l, 0, c))
    mat = pl.BlockSpec((None, heads, HEAD, HEAD), lambda c, g, tb: (l, c, 0, 0))
    return pl.pallas_call(
        functools.partial(_rglru_kernel, sh=sh, tt=tt, heads=heads, reset_first=seg.reset_first),
        grid=(ncb, G, nblk),
        in_specs=[pl.BlockSpec((rows, cwid), lambda c, g, tb: (row_blk(c, g, tb), c)),
                  pl.BlockSpec((rows, cwid), lambda c, g, tb: (row_blk(c, g, tb), ncb + c)),
                  pl.BlockSpec((None, None, (CONV_A - 1) * sh, cwid), lambda c, g, tb: (acl, gi(g), 0, c)),
                  pl.BlockSpec((None, None, sh, cwid), lambda c, g, tb: (ahl, gi(g), 0, c)),
                  vec(CONV_A), vec(1), mat, vec(1), mat, vec(1), vec(1)],
        out_specs=[pl.BlockSpec((rows, cwid), lambda c, g, tb: (row_blk(c, g, tb), c)),
                   pl.BlockSpec((1, (CONV_A - 1) * sh, cwid), lambda c, g, tb: (g, 0, c)),
                   pl.BlockSpec((1, sh, cwid), lambda c, g, tb: (g, 0, c))],
        out_shape=[jax.ShapeDtypeStruct((G * T * sh, D_A), BF16),
                   jax.ShapeDtypeStruct((G, (CONV_A - 1) * sh, D_A), F32),
                   jax.ShapeDtypeStruct((G, sh, D_A), F32)],
        scratch_shapes=[pltpu.VMEM(((CONV_A - 1) * sh, cwid), F32),
                        pltpu.VMEM((sh, cwid), F32),
                        pltpu.VMEM((rows, cwid), F32),
                        pltpu.VMEM((rows, cwid), F32)],
        compiler_params=_params(3),
        name="rglru",
    )(z, z, ac, ah, prm["conv_a_w"], prm["conv_a_b"], prm["w_ra"], prm["b_ra"], prm["w_ia"], prm["b_ia"],
      prm["lam_a"])


def _cumsum_rows(x):
    n = x.shape[0]
    row = lax.broadcasted_iota(jnp.int32, x.shape, 0)
    d = 1
    while d < n:
        x = x + jnp.where(row >= d, pltpu.roll(x, d, 0), 0.0)
        d *= 2
    return x


def _gla_pairwise(q, bc, k, v):
    n = q.shape[0]
    row = lax.broadcasted_iota(jnp.int32, (n, 1), 0)
    o = jnp.zeros((n, HEAD), F32)
    for s in range(n):
        diff = jnp.where(row >= s, bc - bc[s:s + 1], 0.0)
        w = q * k[s:s + 1] * jnp.exp(diff)
        a = jnp.where(row >= s, jnp.sum(w, axis=-1, keepdims=True), 0.0)
        o = o + a * v[s:s + 1]
    return o


def _gla_chunk(q, logf, k, v, st, base):
    c = q.shape[0]
    bc = _cumsum_rows(logf)
    o_state = _dot_nt((q * jnp.exp(bc)).astype(BF16), st.astype(BF16))
    parts = [_gla_pairwise(q[i:i + base], bc[i:i + base], k[i:i + base], v[i:i + base])
             for i in range(0, c, base)]
    s = base
    while s < c:
        for p in range(0, c, 2 * s):
            bm = bc[p + s - 1:p + s]
            qh = q[p + s:p + 2 * s] * jnp.exp(bc[p + s:p + 2 * s] - bm)
            kh = k[p:p + s] * jnp.exp(bm - bc[p:p + s])
            att = _dot_nt(qh.astype(BF16), kh.astype(BF16))
            upd = _dot(att.astype(BF16), v[p:p + s].astype(BF16))
            for i in range(s // base):
                idx = (p + s) // base + i
                parts[idx] = parts[idx] + upd[i * base:(i + 1) * base]
        s *= 2
    o = o_state + (parts[0] if len(parts) == 1 else jnp.concatenate(parts, axis=0))
    b_end = bc[c - 1:c]
    kd = k * jnp.exp(b_end - bc)
    st_new = st * jnp.exp(b_end) + _dot_tn(v.astype(BF16), kd.astype(BF16))
    return o, st_new


def _hgrn2_gates(fz, lb):
    la = jnp.log(jnp.maximum(lb, LB_FLOOR))
    lbv = jnp.log1p(-lb) + jnp.minimum(fz, 0.0) - jnp.log1p(jnp.exp(-jnp.abs(fz)))
    logf = jnp.maximum(la, lbv) + jnp.log1p(jnp.exp(-jnp.abs(la - lbv)))
    k = (1.0 - lb) * jax.nn.sigmoid(-fz)
    return logf, k


def _hgrn2_out(o, gn, gb):
    o = o * lax.rsqrt(jnp.mean(o * o, axis=-1, keepdims=True) + EPS)
    return o * gn * jax.nn.silu(gb)


def _hgrn2_long_kernel(q_ref, fz_ref, v_ref, gb_ref, s0_ref, lb_ref, gn_ref, ob_ref, so_ref, st_sc,
                       *, tt, chunk, base, nblk, hb):
    tb = pl.program_id(2)

    @pl.when(tb == 0)
    def _():
        for hd in range(hb):
            st_sc[hd] = s0_ref[hd].T

    def body(ci, carry):
        sl = pl.ds(pl.multiple_of(ci * chunk, chunk), chunk)
        for hd in range(hb):
            cs = slice(hd * HEAD, (hd + 1) * HEAD)
            logf, k = _hgrn2_gates(fz_ref[sl, cs], lb_ref[:, cs])
            o, st_new = _gla_chunk(q_ref[sl, cs], logf, k, v_ref[sl, cs], st_sc[hd], base)
            st_sc[hd] = st_new
            ob_ref[sl, cs] = _hgrn2_out(o, gn_ref[:, cs], gb_ref[sl, cs]).astype(ob_ref.dtype)
        return carry

    lax.fori_loop(0, tt // chunk, body, 0)

    @pl.when(tb == nblk - 1)
    def _():
        for hd in range(hb):
            so_ref[0, hd] = st_sc[hd].T


def _hgrn2_wide_kernel(*refs, sh, T, nsb, unroll, first, n_slabs):
    if first:
        q_ref, fz_ref, v_ref, gb_ref, s0_ref, lb_ref, gn_ref, ob_ref, so_ref, o_sc = refs
        if n_slabs > 1:
            so_ref[1:] = jnp.zeros((n_slabs - 1, nsb, 1, HEAD, HEAD), F32)
        so = so_ref.at[0]
    else:
        q_ref, fz_ref, v_ref, gb_ref, s0_ref, lb_ref, gn_ref, _, ob_ref, so, o_sc = refs
    sb = pl.program_id(1)
    lb = lb_ref[...]

    def body(it, carry):
        for u in range(unroll):
            jj = it * unroll + u
            sl = pl.ds(sb * nsb + jj, T, stride=sh)
            logf, k = _hgrn2_gates(fz_ref[sl, :], lb)
            o, st_new = _gla_chunk(q_ref[sl, :], logf, k, v_ref[sl, :], s0_ref[jj, 0].T, T)
            so[jj, 0] = st_new.T
            o_sc[sl, :] = o
        return carry

    lax.fori_loop(0, nsb // unroll, body, 0)

    @pl.when(sb == sh // nsb - 1)
    def _():
        ob_ref[...] = _hgrn2_out(o_sc[...], gn_ref[...], gb_ref[...]).astype(ob_ref.dtype)


def _hgrn2(z, s0, prm, l, seg, n_slabs=1, slab=0, s_buf=None):
    G, sh, T = seg.G, seg.sh, seg.T
    nseq = G * sh
    (sa, sl_) = s0
    if sh == 1:
        hb = 4
        wid = hb * HEAD
        q0, f0, v0, g0 = (2 * D_A // wid, (2 * D_A + D_B) // wid, (2 * D_A + 2 * D_B) // wid,
                          (2 * D_A + 3 * D_B) // wid)
        tt = min(T, 256)
        chunk = min(tt, 64)
        nblk = T // tt
        gi = (lambda g: 0) if seg.share_state else (lambda g: g)
        col = lambda c0: pl.BlockSpec((tt, wid), lambda h, g, tb: (g * nblk + tb, c0 + h))
        vec = pl.BlockSpec((None, 1, wid), lambda h, g, tb: (l, 0, h))
        return pl.pallas_call(
            functools.partial(_hgrn2_long_kernel, tt=tt, chunk=chunk, base=min(chunk, 16), nblk=nblk, hb=hb),
            grid=(H_B // hb, G, nblk),
            in_specs=[col(q0), col(f0), col(v0), col(g0),
                      pl.BlockSpec((None, None, hb, HEAD, HEAD), lambda h, g, tb: (sl_, gi(g), h, 0, 0)),
                      vec, vec],
            out_specs=[pl.BlockSpec((tt, wid), lambda h, g, tb: (g * nblk + tb, h)),
                       pl.BlockSpec((1, hb, HEAD, HEAD), lambda h, g, tb: (g, h, 0, 0))],
            out_shape=[jax.ShapeDtypeStruct((G * T * sh, D_B), BF16),
                       jax.ShapeDtypeStruct((nseq, H_B, HEAD, HEAD), F32)],
            scratch_shapes=[pltpu.VMEM((hb, HEAD, HEAD), F32)],
            compiler_params=_params(3),
            name="hgrn2_long",
        )(z, z, z, z, sa, prm["lb_all"], prm["gn_b"])
    assert G == 1 and T == SUBLANE
    q0, f0, v0, g0 = 2 * D_A // HEAD, (2 * D_A + D_B) // HEAD, (2 * D_A + 2 * D_B) // HEAD, (2 * D_A + 3 * D_B) // HEAD
    rows = T * sh
    nsb = min(sh, 32)
    unroll = 4 if nsb % 4 == 0 else 1
    col = lambda c0: pl.BlockSpec((rows, HEAD), lambda h, sb: (0, c0 + h))
    vec = pl.BlockSpec((None, 1, HEAD), lambda h, sb: (l, 0, h))
    in_specs = [col(q0), col(f0), col(v0), col(g0),
                pl.BlockSpec((None, nsb, 1, HEAD, HEAD), lambda h, sb: (sl_, sb, h, 0, 0)), vec, vec]
    args = [z, z, z, z, sa, prm["lb_all"], prm["gn_b"]]
    first = s_buf is None
    if first:
        so_spec = pl.BlockSpec((n_slabs, nsb, 1, HEAD, HEAD), lambda h, sb: (0, sb, h, 0, 0))
        aliases = {}
    else:
        in_specs.append(pl.BlockSpec(memory_space=pl.ANY))
        args.append(s_buf)
        so_spec = pl.BlockSpec((None, nsb, 1, HEAD, HEAD), lambda h, sb: (slab, sb, h, 0, 0))
        aliases = {len(args) - 1: 1}
    return pl.pallas_call(
        functools.partial(_hgrn2_wide_kernel, sh=sh, T=T, nsb=nsb, unroll=unroll, first=first, n_slabs=n_slabs),
        grid=(H_B, sh // nsb),
        in_specs=in_specs,
        out_specs=[pl.BlockSpec((rows, HEAD), lambda h, sb: (0, h)), so_spec],
        out_shape=[jax.ShapeDtypeStruct((G * T * sh, D_B), BF16),
                   jax.ShapeDtypeStruct((n_slabs, nseq, H_B, HEAD, HEAD), F32)],
        scratch_shapes=[pltpu.VMEM((rows, HEAD), F32)],
        input_output_aliases=aliases,
        compiler_params=_params(2),
        name="hgrn2_wide",
    )(*args)


def _cmul(ar, ai, br, bi):
    return ar * br - ai * bi, ar * bi + ai * br


def _s5_packed_scan(xr_sc, xi_sc, sr0, si0, ar, ai, seg_len):
    lanes = xr_sc.shape[1]
    row = lax.broadcasted_iota(jnp.int32, (SUBLANE, lanes), 0)
    art, ait = jnp.broadcast_to(ar, (SUBLANE, lanes)), jnp.broadcast_to(ai, (SUBLANE, lanes))

    def local(t, carry):
        sr, si = carry
        sl = pl.ds(pl.multiple_of(t * SUBLANE, SUBLANE), SUBLANE)
        pr, pi = _cmul(art, ait, sr, si)
        nr, ni = pr + xr_sc[sl, :], pi + xi_sc[sl, :]
        xr_sc[sl, :] = nr
        xi_sc[sl, :] = ni
        return nr, ni

    init = (jnp.where(row == 0, jnp.broadcast_to(sr0, (SUBLANE, lanes)), 0.0),
            jnp.where(row == 0, jnp.broadcast_to(si0, (SUBLANE, lanes)), 0.0))
    fr, fi = lax.fori_loop(0, seg_len, local, init, unroll=min(4, seg_len))

    alr, ali = ar, ai
    n = 1
    while n < seg_len:
        alr, ali = _cmul(alr, ali, alr, ali)
        n *= 2
    assert n == seg_len
    cr, ci = fr[0:1], fi[0:1]
    car_r, car_i = jnp.zeros((SUBLANE, lanes), F32), jnp.zeros((SUBLANE, lanes), F32)
    for i in range(1, SUBLANE):
        car_r = jnp.where(row == i, jnp.broadcast_to(cr, (SUBLANE, lanes)), car_r)
        car_i = jnp.where(row == i, jnp.broadcast_to(ci, (SUBLANE, lanes)), car_i)
        pr, pi = _cmul(alr, ali, cr, ci)
        cr, ci = fr[i:i + 1] + pr, fi[i:i + 1] + pi

    def fix(t, pw):
        pwr, pwi = pw
        sl = pl.ds(pl.multiple_of(t * SUBLANE, SUBLANE), SUBLANE)
        dr, di = _cmul(jnp.broadcast_to(pwr, (SUBLANE, lanes)), jnp.broadcast_to(pwi, (SUBLANE, lanes)),
                       car_r, car_i)
        xr_sc[sl, :] = xr_sc[sl, :] + dr
        xi_sc[sl, :] = xi_sc[sl, :] + di
        return _cmul(pwr, pwi, ar, ai)

    lax.fori_loop(0, seg_len, fix, (ar, ai), unroll=min(4, seg_len))
    return cr, ci


def _s5_kernel(*refs, sh, tt):
    if sh == 1:
        (u_ref, sr0_ref, si0_ref, ar_ref, ai_ref, bre_ref, bim_ref, cre_ref, cim_ref, d_ref,
         y_ref, sro_ref, sio_ref, sr_sc, si_sc, xr_sc, xi_sc, perm_sc) = refs
    else:
        (u_ref, sr0_ref, si0_ref, ar_ref, ai_ref, bre_ref, bim_ref, cre_ref, cim_ref, d_ref,
         y_ref, sro_ref, sio_ref, sr_sc, si_sc, xr_sc, xi_sc) = refs
    tb = pl.program_id(2)
    rows = tt * sh

    @pl.when(tb == 0)
    def _():
        sr_sc[...] = sr0_ref[...]
        si_sc[...] = si0_ref[...]

    u = u_ref[...]
    if sh == 1:
        seg_len = rows // SUBLANE

        @pl.when((pl.program_id(0) == 0) & (pl.program_id(1) == 0) & (tb == 0))
        def _():
            r = lax.broadcasted_iota(jnp.int32, (rows, rows), 0)
            c = lax.broadcasted_iota(jnp.int32, (rows, rows), 1)
            src = jnp.bitwise_and(r, SUBLANE - 1) * seg_len + jnp.right_shift(r, 3)
            perm_sc[...] = jnp.where(c == src, 1.0, 0.0).astype(BF16)

        perm = perm_sc[...]
        u_hi = u.astype(BF16)
        u_lo = (u - u_hi.astype(F32)).astype(BF16)
        up_hi = _dot(perm, u_hi)
        u_skip = up_hi + _dot(perm, u_lo)
        ub = up_hi.astype(BF16)
    else:
        u_skip = u
        ub = u.astype(BF16)
    xr_sc[...] = _dot(ub, bre_ref[...])
    xi_sc[...] = _dot(ub, bim_ref[...])

    if sh == 1:
        sr, si = _s5_packed_scan(xr_sc, xi_sc, sr_sc[...], si_sc[...], ar_ref[...], ai_ref[...], seg_len)
    else:
        ar = jnp.broadcast_to(ar_ref[...], (sh, S5_LANES))
        ai = jnp.broadcast_to(ai_ref[...], (sh, S5_LANES))

        def step(t, carry):
            sl = pl.ds(pl.multiple_of(t * sh, sh), sh)
            pr, pi = _cmul(ar, ai, carry[0], carry[1])
            nr, ni = pr + xr_sc[sl, :], pi + xi_sc[sl, :]
            xr_sc[sl, :] = nr
            xi_sc[sl, :] = ni
            return nr, ni

        sr, si = lax.fori_loop(0, tt, step, (sr_sc[...], si_sc[...]), unroll=True)
    sr_sc[...] = sr
    si_sc[...] = si
    sro_ref[0] = sr
    sio_ref[0] = si
    y = (_dot(xr_sc[...].astype(BF16), cre_ref[...]) - _dot(xi_sc[...].astype(BF16), cim_ref[...])
         + d_ref[...] * u_skip)
    out = jax.nn.gelu(y).astype(BF16)
    if sh == 1:
        out = _dot_tn(perm, out).astype(BF16)
    y_ref[...] = out


def _s5(u, s_re, s_im, prm, j, seg):
    G, sh, T = seg.G, seg.sh, seg.T
    tt = min(T, 512) if sh == 1 else T
    rows, nblk = tt * sh, T // tt
    ncb = D_C // LANE
    (sr, srl), (si, sil) = s_re, s_im
    gi = (lambda g: 0) if seg.share_state else (lambda g: g)
    st_in = lambda lay: pl.BlockSpec((None, None, sh, S5_LANES), lambda c, g, tb: (lay, gi(g), 0, c))
    st_out = pl.BlockSpec((1, sh, S5_LANES), lambda c, g, tb: (g, 0, c))
    vec = pl.BlockSpec((None, 1, S5_LANES), lambda c, g, tb: (j, 0, c))
    bmat = pl.BlockSpec((None, None, LANE, S5_LANES), lambda c, g, tb: (j, c, 0, 0))
    cmat = pl.BlockSpec((None, None, S5_LANES, LANE), lambda c, g, tb: (j, c, 0, 0))
    scratch = [pltpu.VMEM((sh, S5_LANES), F32), pltpu.VMEM((sh, S5_LANES), F32),
               pltpu.VMEM((rows, S5_LANES), F32), pltpu.VMEM((rows, S5_LANES), F32)]
    if sh == 1:
        scratch.append(pltpu.VMEM((rows, rows), BF16))
    return pl.pallas_call(
        functools.partial(_s5_kernel, sh=sh, tt=tt),
        grid=(ncb, G, nblk),
        in_specs=[pl.BlockSpec((rows, LANE), lambda c, g, tb: (g * nblk + tb, c)),
                  st_in(srl), st_in(sil), vec, vec, bmat, bmat, cmat, cmat,
                  pl.BlockSpec((None, 1, LANE), lambda c, g, tb: (j, 0, c))],
        out_specs=[pl.BlockSpec((rows, LANE), lambda c, g, tb: (g * nblk + tb, c)), st_out, st_out],
        out_shape=[jax.ShapeDtypeStruct((G * T * sh, D_C), BF16),
                   jax.ShapeDtypeStruct((G, sh, STATE_C), F32),
                   jax.ShapeDtypeStruct((G, sh, STATE_C), F32)],
        scratch_shapes=scratch,
        compiler_params=_params(3),
        name="s5",
    )(u, sr, si, prm["ar"], prm["ai"], prm["bre"], prm["bim"], prm["cre"], prm["cim"], prm["d"])


def _s5_params(lam_re, lam_im, log_dt, bmat_re, bmat_im, cmat_re, cmat_im, d_skip):
    n = lam_re.shape[0]
    lr, li = lam_re.astype(F32), lam_im.astype(F32)
    dt = jnp.exp(log_dt.astype(F32))[..., None]
    mag = jnp.exp(lr * dt)
    ar = mag * jnp.cos(li * dt)
    ai = mag * jnp.sin(li * dt)
    den = lr * lr + li * li
    zr = ((ar - 1.0) * lr + ai * li) / den
    zi = (ai * lr - (ar - 1.0) * li) / den
    br_, bi_ = bmat_re.astype(F32), bmat_im.astype(F32)
    bb_re = zr[..., None] * br_ - zi[..., None] * bi_
    bb_im = zr[..., None] * bi_ + zi[..., None] * br_
    gpb = LANE // GROUP_C
    nb = G_C // gpb
    eye = jnp.eye(gpb, dtype=F32)

    def pack_b(bb):
        bb = bb.reshape(n, nb, gpb, P_C, GROUP_C)
        return jnp.einsum("ag,nbapc->nbacgp", eye, bb).reshape(n, nb, LANE, S5_LANES).astype(BF16)

    def pack_c(cm):
        cm = cm.astype(F32).reshape(n, nb, gpb, GROUP_C, P_C)
        return jnp.einsum("ag,nbacp->nbapgc", eye, cm).reshape(n, nb, S5_LANES, LANE).astype(BF16)

    return {"ar": ar.reshape(n, 1, STATE_C), "ai": ai.reshape(n, 1, STATE_C),
            "bre": pack_b(bb_re), "bim": pack_b(bb_im), "cre": pack_c(cmat_re), "cim": pack_c(cmat_im),
            "d": d_skip.astype(F32).reshape(n, 1, D_C)}


def _trunk(x, seg, st, w, n_even):
    new = {k: [] for k in ("a_conv", "a_h", "b_S", "c_re", "c_im", "f_conv")}
    s_buf = None
    lay = lambda name, idx: (st[name], min(idx, st[name].shape[0] - 1))
    for l in range(DEPTH):
        if l % 2 == 0:
            i = l // 2
            z = _matmul([x], [(w["w_in_e"], i, 0)], 6 * D_A, norm_g=(w["g_mix"], l), name="in_proj_even")
            out_a, nb, hl = _rglru(z, lay("a_conv", i), lay("a_h", i), w, i, seg)
            if seg.sh == 1:
                out_b, s_new = _hgrn2(z, lay("b_S", i), w, i, seg)
                new["b_S"].append(s_new)
            else:
                out_b, s_buf = _hgrn2(z, lay("b_S", i), w, i, seg, n_slabs=n_even, slab=i, s_buf=s_buf)
            x = _matmul([out_a, out_b], [(w["w_out_e"], i, 0), (w["w_out_e"], i, 1)], D_MODEL, res=x,
                        name="out_proj_even")
            new["a_conv"].append(nb)
            new["a_h"].append(hl)
        else:
            j = l // 2
            u = _matmul([x], [(w["w_in_c"], j, 0)], D_C, norm_g=(w["g_mix"], l), name="in_proj_odd")
            y, sr, si = _s5(u, lay("c_re", j), lay("c_im", j), w["s5"], j, seg)
            x = _matmul([y, y], [(w["w_glu_v"], j, 0), (w["w_glu_g"], j, 0)], D_MODEL, res=x, glu=True,
                        name="glu_odd")
            new["c_re"].append(sr)
            new["c_im"].append(si)
        act, tail = _ffn_up(x, w["g_ffn"], lay("f_conv", l), w["w_up"], w["w_gate"], w["conv_f_w"],
                            w["conv_f_b"], l, seg)
        x = _matmul([act], [(w["w_down"], l, 0)], D_MODEL, res=x, name="ffn_down")
        new["f_conv"].append(tail)
    if s_buf is not None:
        new["b_S"] = s_buf
    return _rmsnorm(x, w["g_final"]), new


def kernel(x_prompt, x_sample, state_a_conv, state_a_h, state_b_S, state_c_re, state_c_im, state_ffn_conv,
           meta_tokens, g_mix, w_in_e, conv_a_w, conv_a_b, w_ra, b_ra, w_ia, b_ia, lam_a, lb_logits, gn_b,
           w_out_e, w_in_c, lam_re, lam_im, log_dt, bmat_re, bmat_im, cmat_re, cmat_im, d_skip,
           w_glu_v, w_glu_g, g_ffn, w_up, w_gate, conv_f_w, conv_f_b, w_down, g_final):
    n_even, n_odd = w_in_e.shape[0], w_in_c.shape[0]
    batch, seq = x_prompt.shape[0], x_prompt.shape[1]
    dec_batch, dec_seq = x_sample.shape[0], x_sample.shape[1]
    fpad = D_FF_PAD - D_FF

    sm = jax.nn.softmax(lb_logits.astype(F32), axis=0)
    lb_all = jnp.clip(jnp.clip(jnp.cumsum(sm, axis=0) - sm[0:1], 0.0, 1.0), 0.0, 1.0)
    w = {
        "g_mix": g_mix.reshape(DEPTH, 1, D_MODEL), "g_ffn": g_ffn.reshape(DEPTH, 1, D_MODEL), "g_final": g_final,
        "w_in_e": w_in_e.astype(BF16), "w_out_e": w_out_e.astype(BF16),
        "conv_a_w": conv_a_w, "conv_a_b": conv_a_b.reshape(n_even, 1, D_A),
        "w_ra": w_ra.astype(BF16), "b_ra": b_ra.reshape(n_even, 1, D_A),
        "w_ia": w_ia.astype(BF16), "b_ia": b_ia.reshape(n_even, 1, D_A), "lam_a": lam_a.reshape(n_even, 1, D_A),
        "lb_all": lb_all.reshape(n_even, 1, D_B), "gn_b": gn_b.reshape(n_even, 1, D_B),
        "w_in_c": w_in_c.astype(BF16), "w_glu_v": w_glu_v.astype(BF16), "w_glu_g": w_glu_g.astype(BF16),
        "s5": _s5_params(lam_re, lam_im, log_dt, bmat_re, bmat_im, cmat_re, cmat_im, d_skip),
        "w_up": jnp.pad(w_up.astype(BF16), ((0, 0), (0, 0), (0, fpad))),
        "w_gate": jnp.pad(w_gate.astype(BF16), ((0, 0), (0, 0), (0, fpad))),
        "conv_f_w": jnp.pad(conv_f_w.astype(F32), ((0, 0), (0, 0), (0, fpad))),
        "conv_f_b": jnp.pad(conv_f_b.astype(F32), ((0, 0), (0, fpad))).reshape(DEPTH, 1, D_FF_PAD),
        "w_down": jnp.pad(w_down.astype(BF16), ((0, 0), (0, fpad), (0, 0))),
    }

    meta_seg = Seg(G=1, sh=1, T=N_META, reset_first=True, share_state=False)
    zero = {
        "a_conv": jnp.zeros((1, 1, CONV_A - 1, D_A), F32),
        "a_h": jnp.zeros((1, 1, 1, D_A), F32),
        "b_S": jnp.zeros((1, 1, H_B, HEAD, HEAD), F32),
        "c_re": jnp.zeros((1, 1, 1, STATE_C), F32),
        "c_im": jnp.zeros((1, 1, 1, STATE_C), F32),
        "f_conv": jnp.zeros((1, 1, CONV_F - 1, D_FF_PAD), F32),
    }
    _, meta_st = _trunk(meta_tokens.astype(F32), meta_seg, zero, w, n_even)

    p_seg = Seg(G=batch, sh=1, T=seq, reset_first=False, share_state=True)
    p_init = {k: jnp.stack(v) for k, v in meta_st.items()}
    yp, p_st = _trunk(x_prompt.reshape(batch * seq, D_MODEL), p_seg, p_init, w, n_even)

    s_seg = Seg(G=1, sh=dec_batch, T=dec_seq, reset_first=False, share_state=False)
    s_init = {
        "a_conv": jnp.swapaxes(state_a_conv, 1, 2).reshape(n_even, 1, (CONV_A - 1) * dec_batch, D_A),
        "a_h": state_a_h.reshape(n_even, 1, dec_batch, D_A),
        "b_S": state_b_S,
        "c_re": state_c_re.reshape(n_odd, 1, dec_batch, STATE_C),
        "c_im": state_c_im.reshape(n_odd, 1, dec_batch, STATE_C),
        "f_conv": jnp.pad(jnp.swapaxes(state_ffn_conv, 1, 2), ((0, 0), (0, 0), (0, 0), (0, fpad)))
        .reshape(DEPTH, 1, (CONV_F - 1) * dec_batch, D_FF_PAD),
    }
    ys, s_st = _trunk(jnp.swapaxes(x_sample, 0, 1).reshape(dec_seq * dec_batch, D_MODEL), s_seg, s_init, w, n_even)

    y_prompt = yp.reshape(batch, seq, D_MODEL)
    y_sample = jnp.swapaxes(ys.reshape(dec_seq, dec_batch, D_MODEL), 0, 1)
    p_out = (jnp.stack(p_st["a_conv"]),
             jnp.stack(p_st["a_h"]).reshape(n_even, batch, D_A),
             jnp.stack(p_st["b_S"]),
             jnp.stack(p_st["c_re"]).reshape(n_odd, batch, G_C, P_C),
             jnp.stack(p_st["c_im"]).reshape(n_odd, batch, G_C, P_C),
             jnp.stack(p_st["f_conv"])[..., :D_FF])
    s_out = (jnp.swapaxes(jnp.stack(s_st["a_conv"]).reshape(n_even, CONV_A - 1, dec_batch, D_A), 1, 2),
             jnp.stack(s_st["a_h"]).reshape(n_even, dec_batch, D_A),
             s_st["b_S"],
             jnp.stack(s_st["c_re"]).reshape(n_odd, dec_batch, G_C, P_C),
             jnp.stack(s_st["c_im"]).reshape(n_odd, dec_batch, G_C, P_C),
             jnp.swapaxes(jnp.stack(s_st["f_conv"]).reshape(DEPTH, CONV_F - 1, dec_batch, D_FF_PAD), 1, 2)[..., :D_FF])
    return (y_prompt, y_sample) + p_out + s_out
```

```python
import functools
from typing import NamedTuple

import jax
import jax.numpy as jnp
from jax import lax
from jax.experimental import pallas as pl
from jax.experimental.pallas import tpu as pltpu

F32 = jnp.float32
BF16 = jnp.bfloat16

D_MODEL = 2048
DEPTH = 4
N_META = 16
D_A = 1024
CONV_A = 4
C_RG = 8.0
D_B = 1024
H_B = 8
HEAD = 128
LB_FLOOR = 1e-30
D_C = 1024
GROUP_C = 16
G_C = 64
P_C = 64
STATE_C = G_C * P_C
D_FF = 5504
CONV_F = 3
EPS = 1e-6
GLA_SAFE_DECAY = 60.0
GLA_MIN_HALF = 32

LANE = 128
SUBLANE = 8
S5_LANES = LANE * P_C // GROUP_C
FFN_SUBTILE = 256
VMEM_LIMIT = 52 * 1024 * 1024


class Seg(NamedTuple):
    G: int
    sh: int
    T: int
    reset_first: bool
    share_state: bool


def _params(n_axes):
    return pltpu.CompilerParams(dimension_semantics=("arbitrary",) * n_axes, vmem_limit_bytes=VMEM_LIMIT)


def _dot(a, b):
    return jnp.dot(a, b, preferred_element_type=F32)


def _dot_nt(a, b):
    return lax.dot_general(a, b, (((1,), (1,)), ((), ())), preferred_element_type=F32)


def _dot_tn(a, b):
    return lax.dot_general(a, b, (((0,), (0,)), ((), ())), preferred_element_type=F32)


def _softplus(x):
    return jnp.maximum(x, 0.0) + jnp.log1p(jnp.exp(-jnp.abs(x)))


def _rms(x, g):
    ms = jnp.mean(x * x, axis=-1, keepdims=True)
    return x * lax.rsqrt(ms + EPS) * g


def _delay(x, prev, k, sh):
    rows = x.shape[0]
    n = k * sh
    p = prev.shape[0]
    if sh % SUBLANE == 0:
        return jnp.concatenate([prev[p - n:], x[:rows - n]], axis=0)
    assert sh == 1
    y = pltpu.roll(x, n, 0)
    row = lax.broadcasted_iota(jnp.int32, x.shape, 0)
    for i in range(n):
        y = jnp.where(row == i, prev[p - n + i:p - n + i + 1], y)
    return y


def _row_tile(x, i):
    return jnp.broadcast_to(x[i:i + 1], (SUBLANE, x.shape[1]))


def _norm_kernel(x_ref, g_ref, o_ref):
    o_ref[...] = _rms(x_ref[...], g_ref[...]).astype(o_ref.dtype)


def _rmsnorm(x, g):
    rows = x.shape[0]
    tm = min(rows, 512)
    return pl.pallas_call(
        _norm_kernel,
        grid=(rows // tm,),
        in_specs=[pl.BlockSpec((tm, D_MODEL), lambda i: (i, 0)),
                  pl.BlockSpec((1, D_MODEL), lambda i: (0, 0))],
        out_specs=pl.BlockSpec((tm, D_MODEL), lambda i: (i, 0)),
        out_shape=jax.ShapeDtypeStruct((rows, D_MODEL), F32),
        compiler_params=_params(1),
        name="rmsnorm",
    )(x, g.reshape(1, D_MODEL))


def _mm_kernel(*refs, n_in, has_res, glu, norm):
    xs, ws = list(refs[:n_in]), refs[n_in:2 * n_in]
    pos = 2 * n_in
    if norm:
        g_ref, h_sc = refs[pos], refs[-1]
        pos += 1

        @pl.when(pl.program_id(1) == 0)
        def _():
            h_sc[...] = _rms(xs[0][...], g_ref[...]).astype(BF16)

        xs[0] = h_sc
    o_ref = refs[pos + (1 if has_res else 0)]
    if glu:
        acc = _dot(xs[0][...], ws[0][...]) * jax.nn.sigmoid(_dot(xs[1][...], ws[1][...]))
    else:
        acc = _dot(xs[0][...], ws[0][...])
        for x_ref, w_ref in zip(xs[1:], ws[1:]):
            acc = acc + _dot(x_ref[...], w_ref[...])
    if has_res:
        acc = refs[pos][...] + acc
    o_ref[...] = acc.astype(o_ref.dtype)


def _matmul(xs, ws, n_out, *, tm, tn, res=None, glu=False, norm_g=None, name="matmul"):
    rows = xs[0].shape[0]
    tm = min(rows, tm)
    in_specs = [pl.BlockSpec((tm, x.shape[1]), lambda i, j: (i, 0)) for x in xs]
    in_specs += [pl.BlockSpec((None, x.shape[1], tn), functools.partial(lambda i, j, l, kb: (l, kb, j), l=l, kb=kb))
                 for x, (_, l, kb) in zip(xs, ws)]
    args = list(xs) + [w for w, _, _ in ws]
    scratch = []
    if norm_g is not None:
        g, gl = norm_g
        in_specs.append(pl.BlockSpec((None, 1, D_MODEL), lambda i, j: (gl, 0, 0)))
        args.append(g)
        scratch.append(pltpu.VMEM((tm, D_MODEL), BF16))
    if res is not None:
        in_specs.append(pl.BlockSpec((tm, tn), lambda i, j: (i, j)))
        args.append(res)
    return pl.pallas_call(
        functools.partial(_mm_kernel, n_in=len(xs), has_res=res is not None, glu=glu, norm=norm_g is not None),
        grid=(rows // tm, n_out // tn),
        in_specs=in_specs,
        out_specs=pl.BlockSpec((tm, tn), lambda i, j: (i, j)),
        out_shape=jax.ShapeDtypeStruct((rows, n_out), F32),
        scratch_shapes=scratch,
        compiler_params=_params(2),
        name=name,
    )(*args)


def _ffn_up_kernel(*refs, sh, rows, nblk):
    if nblk > 1:
        x_ref, xp_ref, g_ref, fb_ref, wu_ref, wg_ref, cw_ref, cb_ref, act_ref, tail_ref, h_sc, hp_sc = refs
    else:
        x_ref, g_ref, fb_ref, wu_ref, wg_ref, cw_ref, cb_ref, act_ref, tail_ref, h_sc = refs

    @pl.when(pl.program_id(1) == 0)
    def _():
        h_sc[...] = _rms(x_ref[...], g_ref[...]).astype(BF16)
        if nblk > 1:
            hp_sc[...] = _rms(xp_ref[...], g_ref[...]).astype(BF16)

    h = h_sc[...]
    tf = act_ref.shape[1]
    for c0 in range(0, tf, FFN_SUBTILE):
        cs = slice(c0, c0 + FFN_SUBTILE)
        up = _dot(h, wu_ref[:, cs])
        gate = _dot(h, wg_ref[:, cs])
        prev = fb_ref[:, cs]
        if nblk > 1:
            up_prev = _dot(hp_sc[...], wu_ref[:, cs])[SUBLANE - 2:]
            prev = jnp.where(pl.program_id(0) % nblk == 0, prev, up_prev)
        upc = (cb_ref[:, cs] + cw_ref[0:1, cs] * _delay(up, prev, 2, sh)
               + cw_ref[1:2, cs] * _delay(up, prev, 1, sh) + cw_ref[2:3, cs] * up)
        act_ref[:, cs] = (jax.nn.gelu(upc) * gate).astype(act_ref.dtype)
        tail_ref[0, :, cs] = up[rows - 2 * sh:]


def _ffn_up(x, g, fbuf, w_up, w_gate, cw, cb, l, seg):
    G, sh, T = seg.G, seg.sh, seg.T
    tt = min(T, 1024 // sh)
    rows, nblk = tt * sh, T // tt
    tf = 512
    fb, fl = fbuf
    gi = (lambda i: 0) if seg.share_state else (lambda i: i // nblk)
    in_specs = [pl.BlockSpec((rows, D_MODEL), lambda i, j: (i, 0))]
    args = [x]
    scratch = [pltpu.VMEM((rows, D_MODEL), BF16)]
    if nblk > 1:
        per = rows // SUBLANE
        in_specs.append(pl.BlockSpec((SUBLANE, D_MODEL), lambda i, j: (jnp.maximum(i * per - 1, 0), 0)))
        args.append(x)
        scratch.append(pltpu.VMEM((SUBLANE, D_MODEL), BF16))
    in_specs += [pl.BlockSpec((None, 1, D_MODEL), lambda i, j: (l, 0, 0)),
                 pl.BlockSpec((None, None, 2 * sh, tf), lambda i, j: (fl, gi(i), 0, j)),
                 pl.BlockSpec((None, D_MODEL, tf), lambda i, j: (l, 0, j)),
                 pl.BlockSpec((None, D_MODEL, tf), lambda i, j: (l, 0, j)),
                 pl.BlockSpec((None, CONV_F, tf), lambda i, j: (l, 0, j)),
                 pl.BlockSpec((None, 1, tf), lambda i, j: (l, 0, j))]
    args += [g, fb, w_up, w_gate, cw, cb]
    act, tails = pl.pallas_call(
        functools.partial(_ffn_up_kernel, sh=sh, rows=rows, nblk=nblk),
        grid=(G * nblk, pl.cdiv(D_FF, tf)),
        in_specs=in_specs,
        out_specs=[pl.BlockSpec((rows, tf), lambda i, j: (i, j)),
                   pl.BlockSpec((1, 2 * sh, tf), lambda i, j: (i, 0, j))],
        out_shape=[jax.ShapeDtypeStruct((G * T * sh, D_FF), BF16),
                   jax.ShapeDtypeStruct((G * nblk, 2 * sh, D_FF), F32)],
        scratch_shapes=scratch,
        compiler_params=_params(2),
        name="ffn_up",
    )(*args)
    return act, tails[nblk - 1::nblk]


def _rglru_kernel(xa_ref, ga_ref, cst_ref, h0_ref, cw_ref, cb_ref, wr_ref, br_ref, wi_ref, bi_ref, lam_ref,
                  oa_ref, cso_ref, ho_ref, prev_sc, h_sc, a_sc, b_sc, *, sh, tt, heads, reset_first):
    tb = pl.program_id(2)
    rows = tt * sh

    @pl.when(tb == 0)
    def _():
        prev_sc[...] = cst_ref[...]
        h_sc[...] = h0_ref[...]

    xa = xa_ref[...]
    prev = prev_sc[...]
    xc = cb_ref[...] + cw_ref[CONV_A - 1:CONV_A, :] * xa
    for k in range(CONV_A - 1):
        xc = xc + cw_ref[k:k + 1, :] * _delay(xa, prev, CONV_A - 1 - k, sh)
    new_prev = xa[rows - (CONV_A - 1) * sh:]
    prev_sc[...] = new_prev
    cso_ref[0] = new_prev

    row = lax.broadcasted_iota(jnp.int32, (rows, HEAD), 0)
    for hd in range(heads):
        cs = slice(hd * HEAD, (hd + 1) * HEAD)
        xh = xc[:, cs]
        xb = xh.astype(BF16)
        r = jax.nn.sigmoid(_dot(xb, wr_ref[hd]) + br_ref[:, cs])
        ig = jax.nn.sigmoid(_dot(xb, wi_ref[hd]) + bi_ref[:, cs])
        log_a = -C_RG * r * _softplus(-lam_ref[:, cs])
        mult = jnp.sqrt(1.0 - jnp.exp(2.0 * log_a))
        if reset_first:
            mult = jnp.where((row < sh) & (tb == 0), 1.0, mult)
        a_sc[:, cs] = jnp.exp(log_a)
        b_sc[:, cs] = mult * ig * xh

    if sh == 1:
        width = a_sc.shape[1]
        a3 = a_sc[...].reshape(rows // SUBLANE, SUBLANE, width)
        b3 = b_sc[...].reshape(rows // SUBLANE, SUBLANE, width)
        sub = lax.broadcasted_iota(jnp.int32, a3.shape, 1)
        d = 1
        while d < SUBLANE:
            keep = sub >= d
            b3 = b3 + a3 * jnp.where(keep, pltpu.roll(b3, d, 1), 0.0)
            a3 = a3 * jnp.where(keep, pltpu.roll(a3, d, 1), 1.0)
            d *= 2
        a_sc[...] = a3.reshape(rows, width)
        b_sc[...] = b3.reshape(rows, width)

        def tile_step(k, h):
            sl = pl.ds(pl.multiple_of(k * SUBLANE, SUBLANE), SUBLANE)
            ht = b_sc[sl, :] + a_sc[sl, :] * jnp.broadcast_to(h, (SUBLANE, width))
            b_sc[sl, :] = ht
            return ht[SUBLANE - 1:]

        h = lax.fori_loop(0, rows // SUBLANE, tile_step, h_sc[...], unroll=min(4, rows // SUBLANE))
    else:
        def step(t, h):
            sl = pl.ds(pl.multiple_of(t * sh, sh), sh)
            h = a_sc[sl, :] * h + b_sc[sl, :]
            b_sc[sl, :] = h
            return h

        h = lax.fori_loop(0, tt, step, h_sc[...], unroll=True)
    h_sc[...] = h
    ho_ref[0] = h
    oa_ref[...] = (b_sc[...] * jax.nn.gelu(ga_ref[...])).astype(oa_ref.dtype)


def _rglru(z, a_conv, a_h, prm, l, seg):
    G, sh, T = seg.G, seg.sh, seg.T
    tt = min(T, 512) if sh == 1 else T
    rows, nblk = tt * sh, T // tt
    heads = 8 if sh == 1 else 2
    cwid = heads * HEAD
    ncb = D_A // cwid
    (ac, acl), (ah, ahl) = a_conv, a_h
    gi = (lambda g: 0) if seg.share_state else (lambda g: g)
    row_blk = lambda c, g, tb: g * nblk + tb
    vec = lambda n: pl.BlockSpec((None, n, cwid), lambda c, g, tb: (l, 0, c))
    mat = pl.BlockSpec((None, heads, HEAD, HEAD), lambda c, g, tb: (l, c, 0, 0))
    return pl.pallas_call(
        functools.partial(_rglru_kernel, sh=sh, tt=tt, heads=heads, reset_first=seg.reset_first),
        grid=(ncb, G, nblk),
        in_specs=[pl.BlockSpec((rows, cwid), lambda c, g, tb: (row_blk(c, g, tb), c)),
                  pl.BlockSpec((rows, cwid), lambda c, g, tb: (row_blk(c, g, tb), ncb + c)),
                  pl.BlockSpec((None, None, (CONV_A - 1) * sh, cwid), lambda c, g, tb: (acl, gi(g), 0, c)),
                  pl.BlockSpec((None, None, sh, cwid), lambda c, g, tb: (ahl, gi(g), 0, c)),
                  vec(CONV_A), vec(1), mat, vec(1), mat, vec(1), vec(1)],
        out_specs=[pl.BlockSpec((rows, cwid), lambda c, g, tb: (row_blk(c, g, tb), c)),
                   pl.BlockSpec((1, (CONV_A - 1) * sh, cwid), lambda c, g, tb: (g, 0, c)),
                   pl.BlockSpec((1, sh, cwid), lambda c, g, tb: (g, 0, c))],
        out_shape=[jax.ShapeDtypeStruct((G * T * sh, D_A), BF16),
                   jax.ShapeDtypeStruct((G, (CONV_A - 1) * sh, D_A), F32),
                   jax.ShapeDtypeStruct((G, sh, D_A), F32)],
        scratch_shapes=[pltpu.VMEM(((CONV_A - 1) * sh, cwid), F32),
                        pltpu.VMEM((sh, cwid), F32),
                        pltpu.VMEM((rows, cwid), F32),
                        pltpu.VMEM((rows, cwid), F32)],
        compiler_params=_params(3),
        name="rglru",
    )(z, z, ac, ah, prm["conv_a_w"], prm["conv_a_b"], prm["w_ra"], prm["b_ra"], prm["w_ia"], prm["b_ia"],
      prm["lam_a"])


def _cumsum_rows(x):
    n = x.shape[0]
    row = lax.broadcasted_iota(jnp.int32, x.shape, 0)
    d = 1
    while d < n:
        x = x + jnp.where(row >= d, pltpu.roll(x, d, 0), 0.0)
        d *= 2
    return x


def _gla_pairwise(q, bc, k, v):
    n = q.shape[0]
    row = lax.broadcasted_iota(jnp.int32, (n, 1), 0)
    o = jnp.zeros((n, HEAD), F32)
    for s in range(n):
        diff = jnp.where(row >= s, bc - bc[s:s + 1], 0.0)
        w = q * k[s:s + 1] * jnp.exp(diff)
        a = jnp.where(row >= s, jnp.sum(w, axis=-1, keepdims=True), 0.0)
        o = o + a * v[s:s + 1]
    return o


def _gla_chunk_fast(q, bc, k, v, st):
    c = q.shape[0]
    h = c // 2
    b_mid = bc[h - 1:h]
    bl = jnp.concatenate([bc[:h], bc[h:] - b_mid], axis=0)
    qt = q * jnp.exp(bl)
    qtb = qt.astype(BF16)
    kt = (k * jnp.exp(-bl)).astype(BF16)
    vb = v.astype(BF16)
    row0 = lax.broadcasted_iota(jnp.int32, (h, h), 0)
    col0 = lax.broadcasted_iota(jnp.int32, (h, h), 1)
    a0 = jnp.where(row0 >= col0, _dot_nt(qtb[:h], kt[:h]), 0.0)
    o0 = _dot(a0.astype(BF16), vb[:h])
    kh = (k[:h] * jnp.exp(b_mid - bc[:h])).astype(BF16)
    row1 = lax.broadcasted_iota(jnp.int32, (h, c), 0) + h
    col1 = lax.broadcasted_iota(jnp.int32, (h, c), 1)
    a1 = jnp.where(row1 >= col1, _dot_nt(qtb[h:], jnp.concatenate([kh, kt[h:]], axis=0)), 0.0)
    o1 = _dot(a1.astype(BF16), vb)
    qs = jnp.concatenate([qt[:h], qt[h:] * jnp.exp(b_mid)], axis=0)
    o = jnp.concatenate([o0, o1], axis=0) + _dot_nt(qs.astype(BF16), st.astype(BF16))
    b_end = bc[c - 1:c]
    kd = k * jnp.exp(b_end - bc)
    st_new = st * jnp.exp(b_end) + _dot_tn(vb, kd.astype(BF16))
    return o, st_new


def _gla_chunk(q, bc, k, v, st, base):
    c = q.shape[0]
    o_state = _dot_nt((q * jnp.exp(bc)).astype(BF16), st.astype(BF16))
    parts = [_gla_pairwise(q[i:i + base], bc[i:i + base], k[i:i + base], v[i:i + base])
             for i in range(0, c, base)]
    s = base
    while s < c:
        for p in range(0, c, 2 * s):
            bm = bc[p + s - 1:p + s]
            qh = q[p + s:p + 2 * s] * jnp.exp(bc[p + s:p + 2 * s] - bm)
            kh = k[p:p + s] * jnp.exp(bm - bc[p:p + s])
            att = _dot_nt(qh.astype(BF16), kh.astype(BF16))
            upd = _dot(att.astype(BF16), v[p:p + s].astype(BF16))
            for i in range(s // base):
                idx = (p + s) // base + i
                parts[idx] = parts[idx] + upd[i * base:(i + 1) * base]
        s *= 2
    o = o_state + (parts[0] if len(parts) == 1 else jnp.concatenate(parts, axis=0))
    b_end = bc[c - 1:c]
    kd = k * jnp.exp(b_end - bc)
    st_new = st * jnp.exp(b_end) + _dot_tn(v.astype(BF16), kd.astype(BF16))
    return o, st_new


def _hgrn2_gates(fz, lb):
    la = jnp.log(jnp.maximum(lb, LB_FLOOR))
    lbv = jnp.log1p(-lb) + jnp.minimum(fz, 0.0) - jnp.log1p(jnp.exp(-jnp.abs(fz)))
    logf = jnp.maximum(la, lbv) + jnp.log1p(jnp.exp(-jnp.abs(la - lbv)))
    k = (1.0 - lb) * jax.nn.sigmoid(-fz)
    return logf, k


def _hgrn2_out(o, gn, gb):
    o = o * lax.rsqrt(jnp.mean(o * o, axis=-1, keepdims=True) + EPS)
    return o * gn * jax.nn.silu(gb)


def _hgrn2_long_kernel(q_ref, fz_ref, v_ref, gb_ref, s0_ref, lb_ref, gn_ref, ob_ref, so_ref, st_sc,
                       *, tt, chunk, base, nblk, hb):
    tb = pl.program_id(2)

    @pl.when(tb == 0)
    def _():
        for hd in range(hb):
            st_sc[hd] = s0_ref[hd].T

    def body(ci, carry):
        sl = pl.ds(pl.multiple_of(ci * chunk, chunk), chunk)
        heads = []
        for hd in range(hb):
            cs = slice(hd * HEAD, (hd + 1) * HEAD)
            logf, k = _hgrn2_gates(fz_ref[sl, cs], lb_ref[:, cs])
            heads.append((hd, cs, _cumsum_rows(logf), k))

        def run(chunk_fn):
            for hd, cs, bc, k in heads:
                o, st_new = chunk_fn(q_ref[sl, cs], bc, k, v_ref[sl, cs], st_sc[hd])
                st_sc[hd] = st_new
                ob_ref[sl, cs] = _hgrn2_out(o, gn_ref[:, cs], gb_ref[sl, cs]).astype(ob_ref.dtype)

        exact = functools.partial(_gla_chunk, base=base)
        if chunk < 2 * GLA_MIN_HALF:
            run(exact)
        else:
            half = chunk // 2
            decay = None
            for _, _, bc, _ in heads:
                d = jnp.maximum(-bc[half - 1:half], bc[half - 1:half] - bc[chunk - 1:chunk])
                decay = d if decay is None else jnp.maximum(decay, d)
            safe = jnp.max(decay) <= GLA_SAFE_DECAY
            pl.when(safe)(lambda: run(_gla_chunk_fast))
            pl.when(jnp.logical_not(safe))(lambda: run(exact))
        return carry

    lax.fori_loop(0, tt // chunk, body, 0)

    @pl.when(tb == nblk - 1)
    def _():
        for hd in range(hb):
            so_ref[0, hd] = st_sc[hd].T


def _hgrn2_wide_kernel(*refs, sh, T, nsb, unroll, first, n_slabs):
    if first:
        q_ref, fz_ref, v_ref, gb_ref, s0_ref, lb_ref, gn_ref, ob_ref, so_ref, o_sc = refs
        if n_slabs > 1:
            so_ref[1:] = jnp.zeros((n_slabs - 1, nsb, 1, HEAD, HEAD), F32)
        so = so_ref.at[0]
    else:
        q_ref, fz_ref, v_ref, gb_ref, s0_ref, lb_ref, gn_ref, _, ob_ref, so, o_sc = refs
    sb = pl.program_id(1)
    lb = lb_ref[...]

    def body(it, carry):
        for u in range(unroll):
            jj = it * unroll + u
            sl = pl.ds(sb * nsb + jj, T, stride=sh)
            logf, k = _hgrn2_gates(fz_ref[sl, :], lb)
            o, st_new = _gla_chunk(q_ref[sl, :], _cumsum_rows(logf), k, v_ref[sl, :], s0_ref[jj, 0].T, T)
            so[jj, 0] = st_new.T
            o_sc[sl, :] = o
        return carry

    lax.fori_loop(0, nsb // unroll, body, 0)

    @pl.when(sb == sh // nsb - 1)
    def _():
        ob_ref[...] = _hgrn2_out(o_sc[...], gn_ref[...], gb_ref[...]).astype(ob_ref.dtype)


def _hgrn2(z, s0, prm, l, seg, n_slabs=1, slab=0, s_buf=None):
    G, sh, T = seg.G, seg.sh, seg.T
    nseq = G * sh
    (sa, sl_) = s0
    if sh == 1:
        hb = 8
        wid = hb * HEAD
        q0, f0, v0, g0 = (2 * D_A // wid, (2 * D_A + D_B) // wid, (2 * D_A + 2 * D_B) // wid,
                          (2 * D_A + 3 * D_B) // wid)
        tt = min(T, 256)
        chunk = min(tt, 64)
        nblk = T // tt
        gi = (lambda g: 0) if seg.share_state else (lambda g: g)
        col = lambda c0: pl.BlockSpec((tt, wid), lambda h, g, tb: (g * nblk + tb, c0 + h))
        vec = pl.BlockSpec((None, 1, wid), lambda h, g, tb: (l, 0, h))
        return pl.pallas_call(
            functools.partial(_hgrn2_long_kernel, tt=tt, chunk=chunk, base=min(chunk, 16), nblk=nblk, hb=hb),
            grid=(H_B // hb, G, nblk),
            in_specs=[col(q0), col(f0), col(v0), col(g0),
                      pl.BlockSpec((None, None, hb, HEAD, HEAD), lambda h, g, tb: (sl_, gi(g), h, 0, 0)),
                      vec, vec],
            out_specs=[pl.BlockSpec((tt, wid), lambda h, g, tb: (g * nblk + tb, h)),
                       pl.BlockSpec((1, hb, HEAD, HEAD), lambda h, g, tb: (g, h, 0, 0))],
            out_shape=[jax.ShapeDtypeStruct((G * T * sh, D_B), BF16),
                       jax.ShapeDtypeStruct((nseq, H_B, HEAD, HEAD), F32)],
            scratch_shapes=[pltpu.VMEM((hb, HEAD, HEAD), F32)],
            compiler_params=_params(3),
            name="hgrn2_long",
        )(z, z, z, z, sa, prm["lb_all"], prm["gn_b"])
    assert G == 1 and T == SUBLANE
    q0, f0, v0, g0 = 2 * D_A // HEAD, (2 * D_A + D_B) // HEAD, (2 * D_A + 2 * D_B) // HEAD, (2 * D_A + 3 * D_B) // HEAD
    rows = T * sh
    nsb = min(sh, 32)
    unroll = 4 if nsb % 4 == 0 else 1
    col = lambda c0: pl.BlockSpec((rows, HEAD), lambda h, sb: (0, c0 + h))
    vec = pl.BlockSpec((None, 1, HEAD), lambda h, sb: (l, 0, h))
    in_specs = [col(q0), col(f0), col(v0), col(g0),
                pl.BlockSpec((None, nsb, 1, HEAD, HEAD), lambda h, sb: (sl_, sb, h, 0, 0)), vec, vec]
    args = [z, z, z, z, sa, prm["lb_all"], prm["gn_b"]]
    first = s_buf is None
    if first:
        so_spec = pl.BlockSpec((n_slabs, nsb, 1, HEAD, HEAD), lambda h, sb: (0, sb, h, 0, 0))
        aliases = {}
    else:
        in_specs.append(pl.BlockSpec(memory_space=pl.ANY))
        args.append(s_buf)
        so_spec = pl.BlockSpec((None, nsb, 1, HEAD, HEAD), lambda h, sb: (slab, sb, h, 0, 0))
        aliases = {len(args) - 1: 1}
    return pl.pallas_call(
        functools.partial(_hgrn2_wide_kernel, sh=sh, T=T, nsb=nsb, unroll=unroll, first=first, n_slabs=n_slabs),
        grid=(H_B, sh // nsb),
        in_specs=in_specs,
        out_specs=[pl.BlockSpec((rows, HEAD), lambda h, sb: (0, h)), so_spec],
        out_shape=[jax.ShapeDtypeStruct((G * T * sh, D_B), BF16),
                   jax.ShapeDtypeStruct((n_slabs, nseq, H_B, HEAD, HEAD), F32)],
        scratch_shapes=[pltpu.VMEM((rows, HEAD), F32)],
        input_output_aliases=aliases,
        compiler_params=_params(2),
        name="hgrn2_wide",
    )(*args)


def _cmul(ar, ai, br, bi):
    return ar * br - ai * bi, ar * bi + ai * br


def _s5_packed_scan(xr_sc, xi_sc, sr0, si0, ar, ai, seg_len):
    lanes = xr_sc.shape[1]
    row = lax.broadcasted_iota(jnp.int32, (SUBLANE, lanes), 0)
    art, ait = jnp.broadcast_to(ar, (SUBLANE, lanes)), jnp.broadcast_to(ai, (SUBLANE, lanes))

    def local(t, carry):
        sr, si = carry
        sl = pl.ds(pl.multiple_of(t * SUBLANE, SUBLANE), SUBLANE)
        pr, pi = _cmul(art, ait, sr, si)
        nr, ni = pr + xr_sc[sl, :], pi + xi_sc[sl, :]
        xr_sc[sl, :] = nr
        xi_sc[sl, :] = ni
        return nr, ni

    init = (jnp.where(row == 0, jnp.broadcast_to(sr0, (SUBLANE, lanes)), 0.0),
            jnp.where(row == 0, jnp.broadcast_to(si0, (SUBLANE, lanes)), 0.0))
    fr, fi = lax.fori_loop(0, seg_len, local, init, unroll=min(4, seg_len))

    alr, ali = ar, ai
    n = 1
    while n < seg_len:
        alr, ali = _cmul(alr, ali, alr, ali)
        n *= 2
    assert n == seg_len
    cr, ci = fr[0:1], fi[0:1]
    car_r, car_i = jnp.zeros((SUBLANE, lanes), F32), jnp.zeros((SUBLANE, lanes), F32)
    for i in range(1, SUBLANE):
        car_r = jnp.where(row == i, jnp.broadcast_to(cr, (SUBLANE, lanes)), car_r)
        car_i = jnp.where(row == i, jnp.broadcast_to(ci, (SUBLANE, lanes)), car_i)
        pr, pi = _cmul(alr, ali, cr, ci)
        cr, ci = fr[i:i + 1] + pr, fi[i:i + 1] + pi

    def fix(t, pw):
        pwr, pwi = pw
        sl = pl.ds(pl.multiple_of(t * SUBLANE, SUBLANE), SUBLANE)
        dr, di = _cmul(jnp.broadcast_to(pwr, (SUBLANE, lanes)), jnp.broadcast_to(pwi, (SUBLANE, lanes)),
                       car_r, car_i)
        xr_sc[sl, :] = xr_sc[sl, :] + dr
        xi_sc[sl, :] = xi_sc[sl, :] + di
        return _cmul(pwr, pwi, ar, ai)

    lax.fori_loop(0, seg_len, fix, (ar, ai), unroll=min(4, seg_len))
    return cr, ci


def _s5_kernel(*refs, sh, tt):
    if sh == 1:
        (u_ref, sr0_ref, si0_ref, ar_ref, ai_ref, bre_ref, bim_ref, cre_ref, cim_ref, d_ref,
         y_ref, sro_ref, sio_ref, sr_sc, si_sc, xr_sc, xi_sc, perm_sc) = refs
    else:
        (u_ref, sr0_ref, si0_ref, ar_ref, ai_ref, bre_ref, bim_ref, cre_ref, cim_ref, d_ref,
         y_ref, sro_ref, sio_ref, sr_sc, si_sc, xr_sc, xi_sc) = refs
    tb = pl.program_id(2)
    rows = tt * sh

    @pl.when(tb == 0)
    def _():
        sr_sc[...] = sr0_ref[...]
        si_sc[...] = si0_ref[...]

    u = u_ref[...]
    if sh == 1:
        seg_len = rows // SUBLANE

        @pl.when((pl.program_id(0) == 0) & (pl.program_id(1) == 0) & (tb == 0))
        def _():
            r = lax.broadcasted_iota(jnp.int32, (rows, rows), 0)
            c = lax.broadcasted_iota(jnp.int32, (rows, rows), 1)
            src = jnp.bitwise_and(r, SUBLANE - 1) * seg_len + jnp.right_shift(r, 3)
            perm_sc[...] = jnp.where(c == src, 1.0, 0.0).astype(BF16)

        perm = perm_sc[...]
        u_hi = u.astype(BF16)
        u_lo = (u - u_hi.astype(F32)).astype(BF16)
        up_hi = _dot(perm, u_hi)
        u_skip = up_hi + _dot(perm, u_lo)
        ub = up_hi.astype(BF16)
    else:
        u_skip = u
        ub = u.astype(BF16)
    xr_sc[...] = _dot(ub, bre_ref[...])
    xi_sc[...] = _dot(ub, bim_ref[...])

    if sh == 1:
        sr, si = _s5_packed_scan(xr_sc, xi_sc, sr_sc[...], si_sc[...], ar_ref[...], ai_ref[...], seg_len)
    else:
        ar = jnp.broadcast_to(ar_ref[...], (sh, S5_LANES))
        ai = jnp.broadcast_to(ai_ref[...], (sh, S5_LANES))

        def step(t, carry):
            sl = pl.ds(pl.multiple_of(t * sh, sh), sh)
            pr, pi = _cmul(ar, ai, carry[0], carry[1])
            nr, ni = pr + xr_sc[sl, :], pi + xi_sc[sl, :]
            xr_sc[sl, :] = nr
            xi_sc[sl, :] = ni
            return nr, ni

        sr, si = lax.fori_loop(0, tt, step, (sr_sc[...], si_sc[...]), unroll=True)
    sr_sc[...] = sr
    si_sc[...] = si
    sro_ref[0] = sr
    sio_ref[0] = si
    y = (_dot(xr_sc[...].astype(BF16), cre_ref[...]) - _dot(xi_sc[...].astype(BF16), cim_ref[...])
         + d_ref[...] * u_skip)
    out = jax.nn.gelu(y).astype(BF16)
    if sh == 1:
        out = _dot_tn(perm, out).astype(BF16)
    y_ref[...] = out


def _s5(u, s_re, s_im, prm, j, seg):
    G, sh, T = seg.G, seg.sh, seg.T
    tt = min(T, 512) if sh == 1 else T
    rows, nblk = tt * sh, T // tt
    ncb = D_C // LANE
    (sr, srl), (si, sil) = s_re, s_im
    gi = (lambda g: 0) if seg.share_state else (lambda g: g)
    st_in = lambda lay: pl.BlockSpec((None, None, sh, S5_LANES), lambda c, g, tb: (lay, gi(g), 0, c))
    st_out = pl.BlockSpec((1, sh, S5_LANES), lambda c, g, tb: (g, 0, c))
    vec = pl.BlockSpec((None, 1, S5_LANES), lambda c, g, tb: (j, 0, c))
    bmat = pl.BlockSpec((None, None, LANE, S5_LANES), lambda c, g, tb: (j, c, 0, 0))
    cmat = pl.BlockSpec((None, None, S5_LANES, LANE), lambda c, g, tb: (j, c, 0, 0))
    scratch = [pltpu.VMEM((sh, S5_LANES), F32), pltpu.VMEM((sh, S5_LANES), F32),
               pltpu.VMEM((rows, S5_LANES), F32), pltpu.VMEM((rows, S5_LANES), F32)]
    if sh == 1:
        scratch.append(pltpu.VMEM((rows, rows), BF16))
    return pl.pallas_call(
        functools.partial(_s5_kernel, sh=sh, tt=tt),
        grid=(ncb, G, nblk),
        in_specs=[pl.BlockSpec((rows, LANE), lambda c, g, tb: (g * nblk + tb, c)),
                  st_in(srl), st_in(sil), vec, vec, bmat, bmat, cmat, cmat,
                  pl.BlockSpec((None, 1, LANE), lambda c, g, tb: (j, 0, c))],
        out_specs=[pl.BlockSpec((rows, LANE), lambda c, g, tb: (g * nblk + tb, c)), st_out, st_out],
        out_shape=[jax.ShapeDtypeStruct((G * T * sh, D_C), BF16),
                   jax.ShapeDtypeStruct((G, sh, STATE_C), F32),
                   jax.ShapeDtypeStruct((G, sh, STATE_C), F32)],
        scratch_shapes=scratch,
        compiler_params=_params(3),
        name="s5",
    )(u, sr, si, prm["ar"], prm["ai"], prm["bre"], prm["bim"], prm["cre"], prm["cim"], prm["d"])


def _s5_params(lam_re, lam_im, log_dt, bmat_re, bmat_im, cmat_re, cmat_im, d_skip):
    n = lam_re.shape[0]
    lr, li = lam_re.astype(F32), lam_im.astype(F32)
    dt = jnp.exp(log_dt.astype(F32))[..., None]
    mag = jnp.exp(lr * dt)
    ar = mag * jnp.cos(li * dt)
    ai = mag * jnp.sin(li * dt)
    den = lr * lr + li * li
    zr = ((ar - 1.0) * lr + ai * li) / den
    zi = (ai * lr - (ar - 1.0) * li) / den
    br_, bi_ = bmat_re.astype(F32), bmat_im.astype(F32)
    bb_re = zr[..., None] * br_ - zi[..., None] * bi_
    bb_im = zr[..., None] * bi_ + zi[..., None] * br_
    gpb = LANE // GROUP_C
    nb = G_C // gpb
    eye = jnp.eye(gpb, dtype=F32)

    def pack_b(bb):
        bb = bb.reshape(n, nb, gpb, P_C, GROUP_C)
        return jnp.einsum("ag,nbapc->nbacgp", eye, bb).reshape(n, nb, LANE, S5_LANES).astype(BF16)

    def pack_c(cm):
        cm = cm.astype(F32).reshape(n, nb, gpb, GROUP_C, P_C)
        return jnp.einsum("ag,nbacp->nbapgc", eye, cm).reshape(n, nb, S5_LANES, LANE).astype(BF16)

    return {"ar": ar.reshape(n, 1, STATE_C), "ai": ai.reshape(n, 1, STATE_C),
            "bre": pack_b(bb_re), "bim": pack_b(bb_im), "cre": pack_c(cmat_re), "cim": pack_c(cmat_im),
            "d": d_skip.astype(F32).reshape(n, 1, D_C)}


def _trunk(x, seg, st, w, n_even):
    new = {k: [] for k in ("a_conv", "a_h", "b_S", "c_re", "c_im", "f_conv")}
    s_buf = None
    lay = lambda name, idx: (st[name], min(idx, st[name].shape[0] - 1))
    for l in range(DEPTH):
        if l % 2 == 0:
            i = l // 2
            z = _matmul([x], [(w["w_in_e"], i, 0)], 6 * D_A, tm=1024, tn=1024, norm_g=(w["g_mix"], l),
                        name="in_proj_even")
            out_a, nb, hl = _rglru(z, lay("a_conv", i), lay("a_h", i), w, i, seg)
            if seg.sh == 1:
                out_b, s_new = _hgrn2(z, lay("b_S", i), w, i, seg)
                new["b_S"].append(s_new)
            else:
                out_b, s_buf = _hgrn2(z, lay("b_S", i), w, i, seg, n_slabs=n_even, slab=i, s_buf=s_buf)
            x = _matmul([out_a, out_b], [(w["w_out_e"], i, 0), (w["w_out_e"], i, 1)], D_MODEL, tm=512, tn=D_MODEL, res=x,
                        name="out_proj_even")
            new["a_conv"].append(nb)
            new["a_h"].append(hl)
        else:
            j = l // 2
            u = _matmul([x], [(w["w_in_c"], j, 0)], D_C, tm=1024, tn=D_C, norm_g=(w["g_mix"], l), name="in_proj_odd")
            y, sr, si = _s5(u, lay("c_re", j), lay("c_im", j), w["s5"], j, seg)
            x = _matmul([y, y], [(w["w_glu_v"], j, 0), (w["w_glu_g"], j, 0)], D_MODEL, tm=512, tn=D_MODEL, res=x, glu=True,
                        name="glu_odd")
            new["c_re"].append(sr)
            new["c_im"].append(si)
        act, tail = _ffn_up(x, w["g_ffn"], lay("f_conv", l), w["w_up"], w["w_gate"], w["conv_f_w"],
                            w["conv_f_b"], l, seg)
        x = _matmul([act], [(w["w_down"], l, 0)], D_MODEL, tm=1024, tn=512, res=x, name="ffn_down")
        new["f_conv"].append(tail)
    if s_buf is not None:
        new["b_S"] = s_buf
    return _rmsnorm(x, w["g_final"]), new


def kernel(x_prompt, x_sample, state_a_conv, state_a_h, state_b_S, state_c_re, state_c_im, state_ffn_conv,
           meta_tokens, g_mix, w_in_e, conv_a_w, conv_a_b, w_ra, b_ra, w_ia, b_ia, lam_a, lb_logits, gn_b,
           w_out_e, w_in_c, lam_re, lam_im, log_dt, bmat_re, bmat_im, cmat_re, cmat_im, d_skip,
           w_glu_v, w_glu_g, g_ffn, w_up, w_gate, conv_f_w, conv_f_b, w_down, g_final):
    n_even, n_odd = w_in_e.shape[0], w_in_c.shape[0]
    batch, seq = x_prompt.shape[0], x_prompt.shape[1]
    dec_batch, dec_seq = x_sample.shape[0], x_sample.shape[1]

    sm = jax.nn.softmax(lb_logits.astype(F32), axis=0)
    lb_all = jnp.clip(jnp.clip(jnp.cumsum(sm, axis=0) - sm[0:1], 0.0, 1.0), 0.0, 1.0)
    w = {
        "g_mix": g_mix.reshape(DEPTH, 1, D_MODEL), "g_ffn": g_ffn.reshape(DEPTH, 1, D_MODEL), "g_final": g_final,
        "w_in_e": w_in_e.astype(BF16), "w_out_e": w_out_e.astype(BF16),
        "conv_a_w": conv_a_w, "conv_a_b": conv_a_b.reshape(n_even, 1, D_A),
        "w_ra": w_ra.astype(BF16), "b_ra": b_ra.reshape(n_even, 1, D_A),
        "w_ia": w_ia.astype(BF16), "b_ia": b_ia.reshape(n_even, 1, D_A), "lam_a": lam_a.reshape(n_even, 1, D_A),
        "lb_all": lb_all.reshape(n_even, 1, D_B), "gn_b": gn_b.reshape(n_even, 1, D_B),
        "w_in_c": w_in_c.astype(BF16), "w_glu_v": w_glu_v.astype(BF16), "w_glu_g": w_glu_g.astype(BF16),
        "s5": _s5_params(lam_re, lam_im, log_dt, bmat_re, bmat_im, cmat_re, cmat_im, d_skip),
        "w_up": w_up.astype(BF16), "w_gate": w_gate.astype(BF16), "w_down": w_down.astype(BF16),
        "conv_f_w": conv_f_w, "conv_f_b": conv_f_b.reshape(DEPTH, 1, D_FF),
    }

    meta_seg = Seg(G=1, sh=1, T=N_META, reset_first=True, share_state=False)
    zero = {
        "a_conv": jnp.zeros((1, 1, CONV_A - 1, D_A), F32),
        "a_h": jnp.zeros((1, 1, 1, D_A), F32),
        "b_S": jnp.zeros((1, 1, H_B, HEAD, HEAD), F32),
        "c_re": jnp.zeros((1, 1, 1, STATE_C), F32),
        "c_im": jnp.zeros((1, 1, 1, STATE_C), F32),
        "f_conv": jnp.zeros((1, 1, CONV_F - 1, D_FF), F32),
    }
    _, meta_st = _trunk(meta_tokens.astype(F32), meta_seg, zero, w, n_even)

    p_seg = Seg(G=batch, sh=1, T=seq, reset_first=False, share_state=True)
    p_init = {k: jnp.stack(v) for k, v in meta_st.items()}
    yp, p_st = _trunk(x_prompt.reshape(batch * seq, D_MODEL), p_seg, p_init, w, n_even)

    s_seg = Seg(G=1, sh=dec_batch, T=dec_seq, reset_first=False, share_state=False)
    s_init = {
        "a_conv": jnp.swapaxes(state_a_conv, 1, 2).reshape(n_even, 1, (CONV_A - 1) * dec_batch, D_A),
        "a_h": state_a_h.reshape(n_even, 1, dec_batch, D_A),
        "b_S": state_b_S,
        "c_re": state_c_re.reshape(n_odd, 1, dec_batch, STATE_C),
        "c_im": state_c_im.reshape(n_odd, 1, dec_batch, STATE_C),
        "f_conv": jnp.swapaxes(state_ffn_conv, 1, 2).reshape(DEPTH, 1, (CONV_F - 1) * dec_batch, D_FF),
    }
    ys, s_st = _trunk(jnp.swapaxes(x_sample, 0, 1).reshape(dec_seq * dec_batch, D_MODEL), s_seg, s_init, w, n_even)

    y_prompt = yp.reshape(batch, seq, D_MODEL)
    y_sample = jnp.swapaxes(ys.reshape(dec_seq, dec_batch, D_MODEL), 0, 1)
    p_out = (jnp.stack(p_st["a_conv"]),
             jnp.stack(p_st["a_h"]).reshape(n_even, batch, D_A),
             jnp.stack(p_st["b_S"]),
             jnp.stack(p_st["c_re"]).reshape(n_odd, batch, G_C, P_C),
             jnp.stack(p_st["c_im"]).reshape(n_odd, batch, G_C, P_C),
             jnp.stack(p_st["f_conv"]))
    s_out = (jnp.swapaxes(jnp.stack(s_st["a_conv"]).reshape(n_even, CONV_A - 1, dec_batch, D_A), 1, 2),
             jnp.stack(s_st["a_h"]).reshape(n_even, dec_batch, D_A),
             s_st["b_S"],
             jnp.stack(s_st["c_re"]).reshape(n_odd, dec_batch, G_C, P_C),
             jnp.stack(s_st["c_im"]).reshape(n_odd, dec_batch, G_C, P_C),
             jnp.swapaxes(jnp.stack(s_st["f_conv"]).reshape(DEPTH, CONV_F - 1, dec_batch, D_FF), 1, 2))
    return (y_prompt, y_sample) + p_out + s_out
```

```python
import functools
from typing import NamedTuple

import jax
import jax.numpy as jnp
from jax import lax
from jax.experimental import pallas as pl
from jax.experimental.pallas import tpu as pltpu

F32 = jnp.float32
BF16 = jnp.bfloat16

D_MODEL = 2048
DEPTH = 4
N_META = 16
D_A = 1024
CONV_A = 4
C_RG = 8.0
D_B = 1024
H_B = 8
HEAD = 128
LB_FLOOR = 1e-30
D_C = 1024
GROUP_C = 16
G_C = 64
P_C = 64
STATE_C = G_C * P_C
D_FF = 5504
CONV_F = 3
EPS = 1e-6
GLA_SAFE_DECAY = 60.0
GLA_MIN_HALF = 32

LANE = 128
SUBLANE = 8
S5_LANES = LANE * P_C // GROUP_C
VMEM_LIMIT = 52 * 1024 * 1024


class Seg(NamedTuple):
    G: int
    sh: int
    T: int
    reset_first: bool
    share_state: bool


def _params(n_axes):
    return pltpu.CompilerParams(dimension_semantics=("arbitrary",) * n_axes, vmem_limit_bytes=VMEM_LIMIT)


def _dot(a, b):
    return jnp.dot(a, b, preferred_element_type=F32)


def _dot_nt(a, b):
    return lax.dot_general(a, b, (((1,), (1,)), ((), ())), preferred_element_type=F32)


def _dot_tn(a, b):
    return lax.dot_general(a, b, (((0,), (0,)), ((), ())), preferred_element_type=F32)


def _softplus(x):
    return jnp.maximum(x, 0.0) + jnp.log1p(jnp.exp(-jnp.abs(x)))


def _rms(x, g):
    ms = jnp.mean(x * x, axis=-1, keepdims=True)
    return x * lax.rsqrt(ms + EPS) * g


def _delay(x, prev, k, sh):
    rows = x.shape[0]
    n = k * sh
    p = prev.shape[0]
    if sh % SUBLANE == 0:
        return jnp.concatenate([prev[p - n:], x[:rows - n]], axis=0)
    assert sh == 1
    y = pltpu.roll(x, n, 0)
    row = lax.broadcasted_iota(jnp.int32, x.shape, 0)
    for i in range(n):
        y = jnp.where(row == i, prev[p - n + i:p - n + i + 1], y)
    return y


def _row_tile(x, i):
    return jnp.broadcast_to(x[i:i + 1], (SUBLANE, x.shape[1]))


def _norm_kernel(x_ref, g_ref, o_ref):
    o_ref[...] = _rms(x_ref[...], g_ref[...]).astype(o_ref.dtype)


def _rmsnorm(x, g):
    rows = x.shape[0]
    tm = min(rows, 512)
    return pl.pallas_call(
        _norm_kernel,
        grid=(rows // tm,),
        in_specs=[pl.BlockSpec((tm, D_MODEL), lambda i: (i, 0)),
                  pl.BlockSpec((1, D_MODEL), lambda i: (0, 0))],
        out_specs=pl.BlockSpec((tm, D_MODEL), lambda i: (i, 0)),
        out_shape=jax.ShapeDtypeStruct((rows, D_MODEL), F32),
        compiler_params=_params(1),
        name="rmsnorm",
    )(x, g.reshape(1, D_MODEL))


def _mm_kernel(*refs, n_in, has_res, glu, norm):
    xs, ws = list(refs[:n_in]), refs[n_in:2 * n_in]
    pos = 2 * n_in
    if norm:
        g_ref, h_sc = refs[pos], refs[-1]
        pos += 1

        @pl.when(pl.program_id(1) == 0)
        def _():
            h_sc[...] = _rms(xs[0][...], g_ref[...]).astype(BF16)

        xs[0] = h_sc
    o_ref = refs[pos + (1 if has_res else 0)]
    if glu:
        acc = _dot(xs[0][...], ws[0][...]) * jax.nn.sigmoid(_dot(xs[1][...], ws[1][...]))
    else:
        acc = _dot(xs[0][...], ws[0][...])
        for x_ref, w_ref in zip(xs[1:], ws[1:]):
            acc = acc + _dot(x_ref[...], w_ref[...])
    if has_res:
        acc = refs[pos][...] + acc
    o_ref[...] = acc.astype(o_ref.dtype)


def _matmul(xs, ws, n_out, *, tm, tn, res=None, glu=False, norm_g=None, name="matmul"):
    rows = xs[0].shape[0]
    tm = min(rows, tm)
    in_specs = [pl.BlockSpec((tm, x.shape[1]), lambda i, j: (i, 0)) for x in xs]
    in_specs += [pl.BlockSpec((None, x.shape[1], tn), functools.partial(lambda i, j, l, kb: (l, kb, j), l=l, kb=kb))
                 for x, (_, l, kb) in zip(xs, ws)]
    args = list(xs) + [w for w, _, _ in ws]
    scratch = []
    if norm_g is not None:
        g, gl = norm_g
        in_specs.append(pl.BlockSpec((None, 1, D_MODEL), lambda i, j: (gl, 0, 0)))
        args.append(g)
        scratch.append(pltpu.VMEM((tm, D_MODEL), BF16))
    if res is not None:
        in_specs.append(pl.BlockSpec((tm, tn), lambda i, j: (i, j)))
        args.append(res)
    return pl.pallas_call(
        functools.partial(_mm_kernel, n_in=len(xs), has_res=res is not None, glu=glu, norm=norm_g is not None),
        grid=(rows // tm, n_out // tn),
        in_specs=in_specs,
        out_specs=pl.BlockSpec((tm, tn), lambda i, j: (i, j)),
        out_shape=jax.ShapeDtypeStruct((rows, n_out), F32),
        scratch_shapes=scratch,
        compiler_params=_params(2),
        name=name,
    )(*args)


def _ffn_up_kernel(*refs, sh, rows, nblk):
    if nblk > 1:
        x_ref, xp_ref, g_ref, fb_ref, wu_ref, wg_ref, cw_ref, cb_ref, act_ref, tail_ref, h_sc, hp_sc = refs
    else:
        x_ref, g_ref, fb_ref, wu_ref, wg_ref, cw_ref, cb_ref, act_ref, tail_ref, h_sc = refs

    @pl.when(pl.program_id(1) == 0)
    def _():
        h_sc[...] = _rms(x_ref[...], g_ref[...]).astype(BF16)
        if nblk > 1:
            hp_sc[...] = _rms(xp_ref[...], g_ref[...]).astype(BF16)

    h = h_sc[...]
    up = _dot(h, wu_ref[...])
    gate = _dot(h, wg_ref[...])
    prev = fb_ref[...]
    if nblk > 1:
        up_prev = _dot(hp_sc[...], wu_ref[...])[SUBLANE - 2:]
        prev = jnp.where(pl.program_id(0) % nblk == 0, prev, up_prev)
    upc = (cb_ref[...] + cw_ref[0:1, :] * _delay(up, prev, 2, sh)
           + cw_ref[1:2, :] * _delay(up, prev, 1, sh) + cw_ref[2:3, :] * up)
    act_ref[...] = (jax.nn.gelu(upc) * gate).astype(act_ref.dtype)
    tail_ref[0] = up[rows - 2 * sh:]


def _ffn_up(x, g, fbuf, w_up, w_gate, cw, cb, l, seg):
    G, sh, T = seg.G, seg.sh, seg.T
    tt = min(T, 1024 // sh)
    rows, nblk = tt * sh, T // tt
    tf = 512
    fb, fl = fbuf
    gi = (lambda i: 0) if seg.share_state else (lambda i: i // nblk)
    in_specs = [pl.BlockSpec((rows, D_MODEL), lambda i, j: (i, 0))]
    args = [x]
    scratch = [pltpu.VMEM((rows, D_MODEL), BF16)]
    if nblk > 1:
        per = rows // SUBLANE
        in_specs.append(pl.BlockSpec((SUBLANE, D_MODEL), lambda i, j: (jnp.maximum(i * per - 1, 0), 0)))
        args.append(x)
        scratch.append(pltpu.VMEM((SUBLANE, D_MODEL), BF16))
    in_specs += [pl.BlockSpec((None, 1, D_MODEL), lambda i, j: (l, 0, 0)),
                 pl.BlockSpec((None, None, 2 * sh, tf), lambda i, j: (fl, gi(i), 0, j)),
                 pl.BlockSpec((None, D_MODEL, tf), lambda i, j: (l, 0, j)),
                 pl.BlockSpec((None, D_MODEL, tf), lambda i, j: (l, 0, j)),
                 pl.BlockSpec((None, CONV_F, tf), lambda i, j: (l, 0, j)),
                 pl.BlockSpec((None, 1, tf), lambda i, j: (l, 0, j))]
    args += [g, fb, w_up, w_gate, cw, cb]
    act, tails = pl.pallas_call(
        functools.partial(_ffn_up_kernel, sh=sh, rows=rows, nblk=nblk),
        grid=(G * nblk, pl.cdiv(D_FF, tf)),
        in_specs=in_specs,
        out_specs=[pl.BlockSpec((rows, tf), lambda i, j: (i, j)),
                   pl.BlockSpec((1, 2 * sh, tf), lambda i, j: (i, 0, j))],
        out_shape=[jax.ShapeDtypeStruct((G * T * sh, D_FF), BF16),
                   jax.ShapeDtypeStruct((G * nblk, 2 * sh, D_FF), F32)],
        scratch_shapes=scratch,
        compiler_params=_params(2),
        name="ffn_up",
    )(*args)
    return act, tails[nblk - 1::nblk]


def _rglru_kernel(xa_ref, ga_ref, cst_ref, h0_ref, cw_ref, cb_ref, wr_ref, br_ref, wi_ref, bi_ref, lam_ref,
                  oa_ref, cso_ref, ho_ref, prev_sc, h_sc, a_sc, b_sc, *, sh, tt, heads, reset_first):
    tb = pl.program_id(2)
    rows = tt * sh

    @pl.when(tb == 0)
    def _():
        prev_sc[...] = cst_ref[...]
        h_sc[...] = h0_ref[...]

    xa = xa_ref[...]
    prev = prev_sc[...]
    xc = cb_ref[...] + cw_ref[CONV_A - 1:CONV_A, :] * xa
    for k in range(CONV_A - 1):
        xc = xc + cw_ref[k:k + 1, :] * _delay(xa, prev, CONV_A - 1 - k, sh)
    new_prev = xa[rows - (CONV_A - 1) * sh:]
    prev_sc[...] = new_prev
    cso_ref[0] = new_prev

    row = lax.broadcasted_iota(jnp.int32, (rows, HEAD), 0)
    for hd in range(heads):
        cs = slice(hd * HEAD, (hd + 1) * HEAD)
        xh = xc[:, cs]
        xb = xh.astype(BF16)
        r = jax.nn.sigmoid(_dot(xb, wr_ref[hd]) + br_ref[:, cs])
        ig = jax.nn.sigmoid(_dot(xb, wi_ref[hd]) + bi_ref[:, cs])
        a = jnp.exp(-C_RG * r * _softplus(-lam_ref[:, cs]))
        mult = jnp.sqrt(1.0 - a * a)
        if reset_first:
            mult = jnp.where((row < sh) & (tb == 0), 1.0, mult)
        a_sc[:, cs] = a
        b_sc[:, cs] = mult * ig * xh

    if sh == 1:
        width = a_sc.shape[1]
        a3 = a_sc[...].reshape(rows // SUBLANE, SUBLANE, width)
        b3 = b_sc[...].reshape(rows // SUBLANE, SUBLANE, width)
        sub = lax.broadcasted_iota(jnp.int32, a3.shape, 1)
        d = 1
        while d < SUBLANE:
            keep = sub >= d
            b3 = b3 + a3 * jnp.where(keep, pltpu.roll(b3, d, 1), 0.0)
            a3 = a3 * jnp.where(keep, pltpu.roll(a3, d, 1), 1.0)
            d *= 2
        a_sc[...] = a3.reshape(rows, width)
        b_sc[...] = b3.reshape(rows, width)

        def tile_step(k, h):
            sl = pl.ds(pl.multiple_of(k * SUBLANE, SUBLANE), SUBLANE)
            ht = b_sc[sl, :] + a_sc[sl, :] * jnp.broadcast_to(h, (SUBLANE, width))
            b_sc[sl, :] = ht
            return ht[SUBLANE - 1:]

        h = lax.fori_loop(0, rows // SUBLANE, tile_step, h_sc[...], unroll=min(4, rows // SUBLANE))
    else:
        def step(t, h):
            sl = pl.ds(pl.multiple_of(t * sh, sh), sh)
            h = a_sc[sl, :] * h + b_sc[sl, :]
            b_sc[sl, :] = h
            return h

        h = lax.fori_loop(0, tt, step, h_sc[...], unroll=True)
    h_sc[...] = h
    ho_ref[0] = h
    oa_ref[...] = (b_sc[...] * jax.nn.gelu(ga_ref[...])).astype(oa_ref.dtype)


def _rglru(z, a_conv, a_h, prm, l, seg):
    G, sh, T = seg.G, seg.sh, seg.T
    tt = min(T, 512) if sh == 1 else T
    rows, nblk = tt * sh, T // tt
    heads = 8 if sh == 1 else 2
    cwid = heads * HEAD
    ncb = D_A // cwid
    (ac, acl), (ah, ahl) = a_conv, a_h
    gi = (lambda g: 0) if seg.share_state else (lambda g: g)
    row_blk = lambda c, g, tb: g * nblk + tb
    vec = lambda n: pl.BlockSpec((None, n, cwid), lambda c, g, tb: (l, 0, c))
    mat = pl.BlockSpec((None, heads, HEAD, HEAD), lambda c, g, tb: (l, c, 0, 0))
    return pl.pallas_call(
        functools.partial(_rglru_kernel, sh=sh, tt=tt, heads=heads, reset_first=seg.reset_first),
        grid=(ncb, G, nblk),
        in_specs=[pl.BlockSpec((rows, cwid), lambda c, g, tb: (row_blk(c, g, tb), c)),
                  pl.BlockSpec((rows, cwid), lambda c, g, tb: (row_blk(c, g, tb), ncb + c)),
                  pl.BlockSpec((None, None, (CONV_A - 1) * sh, cwid), lambda c, g, tb: (acl, gi(g), 0, c)),
                  pl.BlockSpec((None, None, sh, cwid), lambda c, g, tb: (ahl, gi(g), 0, c)),
                  vec(CONV_A), vec(1), mat, vec(1), mat, vec(1), vec(1)],
        out_specs=[pl.BlockSpec((rows, cwid), lambda c, g, tb: (row_blk(c, g, tb), c)),
                   pl.BlockSpec((1, (CONV_A - 1) * sh, cwid), lambda c, g, tb: (g, 0, c)),
                   pl.BlockSpec((1, sh, cwid), lambda c, g, tb: (g, 0, c))],
        out_shape=[jax.ShapeDtypeStruct((G * T * sh, D_A), BF16),
                   jax.ShapeDtypeStruct((G, (CONV_A - 1) * sh, D_A), F32),
                   jax.ShapeDtypeStruct((G, sh, D_A), F32)],
        scratch_shapes=[pltpu.VMEM(((CONV_A - 1) * sh, cwid), F32),
                        pltpu.VMEM((sh, cwid), F32),
                        pltpu.VMEM((rows, cwid), F32),
                        pltpu.VMEM((rows, cwid), F32)],
        compiler_params=_params(3),
        name="rglru",
    )(z, z, ac, ah, prm["conv_a_w"], prm["conv_a_b"], prm["w_ra"], prm["b_ra"], prm["w_ia"], prm["b_ia"],
      prm["lam_a"])


def _cumsum_rows(x):
    n = x.shape[0]
    row = lax.broadcasted_iota(jnp.int32, x.shape, 0)
    d = 1
    while d < n:
        x = x + jnp.where(row >= d, pltpu.roll(x, d, 0), 0.0)
        d *= 2
    return x


def _gla_pairwise(q, bc, k, v):
    n = q.shape[0]
    row = lax.broadcasted_iota(jnp.int32, (n, 1), 0)
    o = jnp.zeros((n, HEAD), F32)
    for s in range(n):
        diff = jnp.where(row >= s, bc - bc[s:s + 1], 0.0)
        w = q * k[s:s + 1] * jnp.exp(diff)
        a = jnp.where(row >= s, jnp.sum(w, axis=-1, keepdims=True), 0.0)
        o = o + a * v[s:s + 1]
    return o


def _gla_chunk_fast(q, bc, k, v, st):
    c = q.shape[0]
    h = c // 2
    b_mid = bc[h - 1:h]
    bl = jnp.concatenate([bc[:h], bc[h:] - b_mid], axis=0)
    qt = q * jnp.exp(bl)
    qtb = qt.astype(BF16)
    kt = (k * jnp.exp(-bl)).astype(BF16)
    vb = v.astype(BF16)
    row0 = lax.broadcasted_iota(jnp.int32, (h, h), 0)
    col0 = lax.broadcasted_iota(jnp.int32, (h, h), 1)
    a0 = jnp.where(row0 >= col0, _dot_nt(qtb[:h], kt[:h]), 0.0)
    o0 = _dot(a0.astype(BF16), vb[:h])
    kh = (k[:h] * jnp.exp(b_mid - bc[:h])).astype(BF16)
    row1 = lax.broadcasted_iota(jnp.int32, (h, c), 0) + h
    col1 = lax.broadcasted_iota(jnp.int32, (h, c), 1)
    a1 = jnp.where(row1 >= col1, _dot_nt(qtb[h:], jnp.concatenate([kh, kt[h:]], axis=0)), 0.0)
    o1 = _dot(a1.astype(BF16), vb)
    qs = jnp.concatenate([qt[:h], qt[h:] * jnp.exp(b_mid)], axis=0)
    o = jnp.concatenate([o0, o1], axis=0) + _dot_nt(qs.astype(BF16), st.astype(BF16))
    b_end = bc[c - 1:c]
    kd = k * jnp.exp(b_end - bc)
    st_new = st * jnp.exp(b_end) + _dot_tn(vb, kd.astype(BF16))
    return o, st_new


def _gla_chunk(q, bc, k, v, st, base):
    c = q.shape[0]
    o_state = _dot_nt((q * jnp.exp(bc)).astype(BF16), st.astype(BF16))
    parts = [_gla_pairwise(q[i:i + base], bc[i:i + base], k[i:i + base], v[i:i + base])
             for i in range(0, c, base)]
    s = base
    while s < c:
        for p in range(0, c, 2 * s):
            bm = bc[p + s - 1:p + s]
            qh = q[p + s:p + 2 * s] * jnp.exp(bc[p + s:p + 2 * s] - bm)
            kh = k[p:p + s] * jnp.exp(bm - bc[p:p + s])
            att = _dot_nt(qh.astype(BF16), kh.astype(BF16))
            upd = _dot(att.astype(BF16), v[p:p + s].astype(BF16))
            for i in range(s // base):
                idx = (p + s) // base + i
                parts[idx] = parts[idx] + upd[i * base:(i + 1) * base]
        s *= 2
    o = o_state + (parts[0] if len(parts) == 1 else jnp.concatenate(parts, axis=0))
    b_end = bc[c - 1:c]
    kd = k * jnp.exp(b_end - bc)
    st_new = st * jnp.exp(b_end) + _dot_tn(v.astype(BF16), kd.astype(BF16))
    return o, st_new


def _hgrn2_gates(fz, lb):
    la = jnp.log(jnp.maximum(lb, LB_FLOOR))
    lbv = jnp.log1p(-lb) + jnp.minimum(fz, 0.0) - jnp.log1p(jnp.exp(-jnp.abs(fz)))
    logf = jnp.maximum(la, lbv) + jnp.log1p(jnp.exp(-jnp.abs(la - lbv)))
    k = (1.0 - lb) * jax.nn.sigmoid(-fz)
    return logf, k


def _hgrn2_out(o, gn, gb):
    o = o * lax.rsqrt(jnp.mean(o * o, axis=-1, keepdims=True) + EPS)
    return o * gn * jax.nn.silu(gb)


def _hgrn2_long_kernel(q_ref, fz_ref, v_ref, gb_ref, s0_ref, lb_ref, gn_ref, ob_ref, so_ref, st_sc,
                       *, tt, chunk, base, nblk, hb):
    tb = pl.program_id(2)

    @pl.when(tb == 0)
    def _():
        for hd in range(hb):
            st_sc[hd] = s0_ref[hd].T

    def body(ci, carry):
        sl = pl.ds(pl.multiple_of(ci * chunk, chunk), chunk)
        heads = []
        for hd in range(hb):
            cs = slice(hd * HEAD, (hd + 1) * HEAD)
            logf, k = _hgrn2_gates(fz_ref[sl, cs], lb_ref[:, cs])
            heads.append((hd, cs, _cumsum_rows(logf), k))

        def run(chunk_fn):
            for hd, cs, bc, k in heads:
                o, st_new = chunk_fn(q_ref[sl, cs], bc, k, v_ref[sl, cs], st_sc[hd])
                st_sc[hd] = st_new
                ob_ref[sl, cs] = _hgrn2_out(o, gn_ref[:, cs], gb_ref[sl, cs]).astype(ob_ref.dtype)

        exact = functools.partial(_gla_chunk, base=base)
        if chunk < 2 * GLA_MIN_HALF:
            run(exact)
        else:
            half = chunk // 2
            decay = None
            for _, _, bc, _ in heads:
                d = jnp.maximum(-bc[half - 1:half], bc[half - 1:half] - bc[chunk - 1:chunk])
                decay = d if decay is None else jnp.maximum(decay, d)
            safe = jnp.max(decay) <= GLA_SAFE_DECAY
            pl.when(safe)(lambda: run(_gla_chunk_fast))
            pl.when(jnp.logical_not(safe))(lambda: run(exact))
        return carry

    lax.fori_loop(0, tt // chunk, body, 0)

    @pl.when(tb == nblk - 1)
    def _():
        for hd in range(hb):
            so_ref[0, hd] = st_sc[hd].T


def _hgrn2_wide_kernel(*refs, sh, T, nsb, unroll, first, n_slabs):
    if first:
        q_ref, fz_ref, v_ref, gb_ref, s0_ref, lb_ref, gn_ref, ob_ref, so_ref, o_sc = refs
        if n_slabs > 1:
            so_ref[1:] = jnp.zeros((n_slabs - 1, nsb, 1, HEAD, HEAD), F32)
        so = so_ref.at[0]
    else:
        q_ref, fz_ref, v_ref, gb_ref, s0_ref, lb_ref, gn_ref, _, ob_ref, so, o_sc = refs
    sb = pl.program_id(1)
    lb = lb_ref[...]

    def body(it, carry):
        for u in range(unroll):
            jj = it * unroll + u
            sl = pl.ds(sb * nsb + jj, T, stride=sh)
            logf, k = _hgrn2_gates(fz_ref[sl, :], lb)
            o, st_new = _gla_chunk(q_ref[sl, :], _cumsum_rows(logf), k, v_ref[sl, :], s0_ref[jj, 0].T, T)
            so[jj, 0] = st_new.T
            o_sc[sl, :] = o
        return carry

    lax.fori_loop(0, nsb // unroll, body, 0)

    @pl.when(sb == sh // nsb - 1)
    def _():
        ob_ref[...] = _hgrn2_out(o_sc[...], gn_ref[...], gb_ref[...]).astype(ob_ref.dtype)


def _hgrn2(z, s0, prm, l, seg, n_slabs=1, slab=0, s_buf=None):
    G, sh, T = seg.G, seg.sh, seg.T
    nseq = G * sh
    (sa, sl_) = s0
    if sh == 1:
        hb = 8
        wid = hb * HEAD
        q0, f0, v0, g0 = (2 * D_A // wid, (2 * D_A + D_B) // wid, (2 * D_A + 2 * D_B) // wid,
                          (2 * D_A + 3 * D_B) // wid)
        tt = min(T, 256)
        chunk = min(tt, 64)
        nblk = T // tt
        gi = (lambda g: 0) if seg.share_state else (lambda g: g)
        col = lambda c0: pl.BlockSpec((tt, wid), lambda h, g, tb: (g * nblk + tb, c0 + h))
        vec = pl.BlockSpec((None, 1, wid), lambda h, g, tb: (l, 0, h))
        return pl.pallas_call(
            functools.partial(_hgrn2_long_kernel, tt=tt, chunk=chunk, base=min(chunk, 16), nblk=nblk, hb=hb),
            grid=(H_B // hb, G, nblk),
            in_specs=[col(q0), col(f0), col(v0), col(g0),
                      pl.BlockSpec((None, None, hb, HEAD, HEAD), lambda h, g, tb: (sl_, gi(g), h, 0, 0)),
                      vec, vec],
            out_specs=[pl.BlockSpec((tt, wid), lambda h, g, tb: (g * nblk + tb, h)),
                       pl.BlockSpec((1, hb, HEAD, HEAD), lambda h, g, tb: (g, h, 0, 0))],
            out_shape=[jax.ShapeDtypeStruct((G * T * sh, D_B), BF16),
                       jax.ShapeDtypeStruct((nseq, H_B, HEAD, HEAD), F32)],
            scratch_shapes=[pltpu.VMEM((hb, HEAD, HEAD), F32)],
            compiler_params=_params(3),
            name="hgrn2_long",
        )(z, z, z, z, sa, prm["lb_all"], prm["gn_b"])
    assert G == 1 and T == SUBLANE
    q0, f0, v0, g0 = 2 * D_A // HEAD, (2 * D_A + D_B) // HEAD, (2 * D_A + 2 * D_B) // HEAD, (2 * D_A + 3 * D_B) // HEAD
    rows = T * sh
    nsb = min(sh, 32)
    unroll = 4 if nsb % 4 == 0 else 1
    col = lambda c0: pl.BlockSpec((rows, HEAD), lambda h, sb: (0, c0 + h))
    vec = pl.BlockSpec((None, 1, HEAD), lambda h, sb: (l, 0, h))
    in_specs = [col(q0), col(f0), col(v0), col(g0),
                pl.BlockSpec((None, nsb, 1, HEAD, HEAD), lambda h, sb: (sl_, sb, h, 0, 0)), vec, vec]
    args = [z, z, z, z, sa, prm["lb_all"], prm["gn_b"]]
    first = s_buf is None
    if first:
        so_spec = pl.BlockSpec((n_slabs, nsb, 1, HEAD, HEAD), lambda h, sb: (0, sb, h, 0, 0))
        aliases = {}
    else:
        in_specs.append(pl.BlockSpec(memory_space=pl.ANY))
        args.append(s_buf)
        so_spec = pl.BlockSpec((None, nsb, 1, HEAD, HEAD), lambda h, sb: (slab, sb, h, 0, 0))
        aliases = {len(args) - 1: 1}
    return pl.pallas_call(
        functools.partial(_hgrn2_wide_kernel, sh=sh, T=T, nsb=nsb, unroll=unroll, first=first, n_slabs=n_slabs),
        grid=(H_B, sh // nsb),
        in_specs=in_specs,
        out_specs=[pl.BlockSpec((rows, HEAD), lambda h, sb: (0, h)), so_spec],
        out_shape=[jax.ShapeDtypeStruct((G * T * sh, D_B), BF16),
                   jax.ShapeDtypeStruct((n_slabs, nseq, H_B, HEAD, HEAD), F32)],
        scratch_shapes=[pltpu.VMEM((rows, HEAD), F32)],
        input_output_aliases=aliases,
        compiler_params=_params(2),
        name="hgrn2_wide",
    )(*args)


def _cmul(ar, ai, br, bi):
    return ar * br - ai * bi, ar * bi + ai * br


def _s5_packed_scan(xr_sc, xi_sc, pwr_sc, pwi_sc, sr0, si0, ar, ai, seg_len):
    lanes = xr_sc.shape[1]
    row = lax.broadcasted_iota(jnp.int32, (SUBLANE, lanes), 0)
    art, ait = jnp.broadcast_to(ar, (SUBLANE, lanes)), jnp.broadcast_to(ai, (SUBLANE, lanes))

    def local(t, carry):
        sr, si = carry
        sl = pl.ds(pl.multiple_of(t * SUBLANE, SUBLANE), SUBLANE)
        pr, pi = _cmul(art, ait, sr, si)
        nr, ni = pr + xr_sc[sl, :], pi + xi_sc[sl, :]
        xr_sc[sl, :] = nr
        xi_sc[sl, :] = ni
        return nr, ni

    init = (jnp.where(row == 0, jnp.broadcast_to(sr0, (SUBLANE, lanes)), 0.0),
            jnp.where(row == 0, jnp.broadcast_to(si0, (SUBLANE, lanes)), 0.0))
    fr, fi = lax.fori_loop(0, seg_len, local, init, unroll=min(4, seg_len))

    last = seg_len * SUBLANE - 1
    alr, ali = pwr_sc[last:last + 1, :], pwi_sc[last:last + 1, :]
    cr, ci = fr[0:1], fi[0:1]
    car_r, car_i = jnp.zeros((SUBLANE, lanes), F32), jnp.zeros((SUBLANE, lanes), F32)
    for i in range(1, SUBLANE):
        car_r = jnp.where(row == i, jnp.broadcast_to(cr, (SUBLANE, lanes)), car_r)
        car_i = jnp.where(row == i, jnp.broadcast_to(ci, (SUBLANE, lanes)), car_i)
        pr, pi = _cmul(alr, ali, cr, ci)
        cr, ci = fr[i:i + 1] + pr, fi[i:i + 1] + pi

    def fix(t, carry):
        sl = pl.ds(pl.multiple_of(t * SUBLANE, SUBLANE), SUBLANE)
        dr, di = _cmul(pwr_sc[sl, :], pwi_sc[sl, :], car_r, car_i)
        xr_sc[sl, :] = xr_sc[sl, :] + dr
        xi_sc[sl, :] = xi_sc[sl, :] + di
        return carry

    lax.fori_loop(0, seg_len, fix, 0, unroll=min(4, seg_len))
    return cr, ci


def _s5_kernel(*refs, sh, tt):
    if sh == 1:
        (u_ref, sr0_ref, si0_ref, ar_ref, ai_ref, bre_ref, bim_ref, cre_ref, cim_ref, d_ref,
         y_ref, sro_ref, sio_ref, sr_sc, si_sc, xr_sc, xi_sc, up_sc, yp_sc, yt_sc, pwr_sc, pwi_sc) = refs
    else:
        (u_ref, sr0_ref, si0_ref, ar_ref, ai_ref, bre_ref, bim_ref, cre_ref, cim_ref, d_ref,
         y_ref, sro_ref, sio_ref, sr_sc, si_sc, xr_sc, xi_sc) = refs
    tb = pl.program_id(2)
    rows = tt * sh

    @pl.when(tb == 0)
    def _():
        sr_sc[...] = sr0_ref[...]
        si_sc[...] = si0_ref[...]

    if sh == 1:
        seg_len = rows // SUBLANE

        @pl.when((pl.program_id(1) == 0) & (tb == 0))
        def _():
            def grow(t, pw):
                sl = pl.ds(pl.multiple_of(t * SUBLANE, SUBLANE), SUBLANE)
                pwr_sc[sl, :] = jnp.broadcast_to(pw[0], (SUBLANE, S5_LANES))
                pwi_sc[sl, :] = jnp.broadcast_to(pw[1], (SUBLANE, S5_LANES))
                return _cmul(pw[0], pw[1], ar_ref[...], ai_ref[...])

            lax.fori_loop(0, seg_len, grow, (ar_ref[...], ai_ref[...]))

        def pack(t, carry):
            up_sc[pl.ds(pl.multiple_of(t * SUBLANE, SUBLANE), SUBLANE), :] = u_ref[pl.ds(t, SUBLANE, stride=seg_len), :]
            return carry

        lax.fori_loop(0, seg_len, pack, 0, unroll=min(8, seg_len))
        u_skip = up_sc[...]
    else:
        u_skip = u_ref[...]
    ub = u_skip.astype(BF16)
    xr_sc[...] = _dot(ub, bre_ref[...])
    xi_sc[...] = _dot(ub, bim_ref[...])

    if sh == 1:
        sr, si = _s5_packed_scan(xr_sc, xi_sc, pwr_sc, pwi_sc, sr_sc[...], si_sc[...], ar_ref[...], ai_ref[...],
                                 seg_len)
    else:
        ar = jnp.broadcast_to(ar_ref[...], (sh, S5_LANES))
        ai = jnp.broadcast_to(ai_ref[...], (sh, S5_LANES))

        def step(t, carry):
            sl = pl.ds(pl.multiple_of(t * sh, sh), sh)
            pr, pi = _cmul(ar, ai, carry[0], carry[1])
            nr, ni = pr + xr_sc[sl, :], pi + xi_sc[sl, :]
            xr_sc[sl, :] = nr
            xi_sc[sl, :] = ni
            return nr, ni

        sr, si = lax.fori_loop(0, tt, step, (sr_sc[...], si_sc[...]), unroll=True)
    sr_sc[...] = sr
    si_sc[...] = si
    sro_ref[0] = sr
    sio_ref[0] = si
    y = (_dot(xr_sc[...].astype(BF16), cre_ref[...]) - _dot(xi_sc[...].astype(BF16), cim_ref[...])
         + d_ref[...] * u_skip)
    out = jax.nn.gelu(y)
    if sh == 1:
        yp_sc[...] = out

        def unpack(t, carry):
            yt_sc[pl.ds(t, SUBLANE, stride=seg_len), :] = yp_sc[pl.ds(pl.multiple_of(t * SUBLANE, SUBLANE), SUBLANE), :]
            return carry

        lax.fori_loop(0, seg_len, unpack, 0, unroll=min(8, seg_len))
        out = yt_sc[...]
    y_ref[...] = out.astype(BF16)


def _s5(u, s_re, s_im, prm, j, seg):
    G, sh, T = seg.G, seg.sh, seg.T
    tt = min(T, 512) if sh == 1 else T
    rows, nblk = tt * sh, T // tt
    ncb = D_C // LANE
    (sr, srl), (si, sil) = s_re, s_im
    gi = (lambda g: 0) if seg.share_state else (lambda g: g)
    st_in = lambda lay: pl.BlockSpec((None, None, sh, S5_LANES), lambda c, g, tb: (lay, gi(g), 0, c))
    st_out = pl.BlockSpec((1, sh, S5_LANES), lambda c, g, tb: (g, 0, c))
    vec = pl.BlockSpec((None, 1, S5_LANES), lambda c, g, tb: (j, 0, c))
    bmat = pl.BlockSpec((None, None, LANE, S5_LANES), lambda c, g, tb: (j, c, 0, 0))
    cmat = pl.BlockSpec((None, None, S5_LANES, LANE), lambda c, g, tb: (j, c, 0, 0))
    scratch = [pltpu.VMEM((sh, S5_LANES), F32), pltpu.VMEM((sh, S5_LANES), F32),
               pltpu.VMEM((rows, S5_LANES), F32), pltpu.VMEM((rows, S5_LANES), F32)]
    if sh == 1:
        scratch += [pltpu.VMEM((rows, LANE), F32)] * 3
        scratch += [pltpu.VMEM((rows, S5_LANES), F32)] * 2
    return pl.pallas_call(
        functools.partial(_s5_kernel, sh=sh, tt=tt),
        grid=(ncb, G, nblk),
        in_specs=[pl.BlockSpec((rows, LANE), lambda c, g, tb: (g * nblk + tb, c)),
                  st_in(srl), st_in(sil), vec, vec, bmat, bmat, cmat, cmat,
                  pl.BlockSpec((None, 1, LANE), lambda c, g, tb: (j, 0, c))],
        out_specs=[pl.BlockSpec((rows, LANE), lambda c, g, tb: (g * nblk + tb, c)), st_out, st_out],
        out_shape=[jax.ShapeDtypeStruct((G * T * sh, D_C), BF16),
                   jax.ShapeDtypeStruct((G, sh, STATE_C), F32),
                   jax.ShapeDtypeStruct((G, sh, STATE_C), F32)],
        scratch_shapes=scratch,
        compiler_params=_params(3),
        name="s5",
    )(u, sr, si, prm["ar"], prm["ai"], prm["bre"], prm["bim"], prm["cre"], prm["cim"], prm["d"])


def _s5_params(lam_re, lam_im, log_dt, bmat_re, bmat_im, cmat_re, cmat_im, d_skip):
    n = lam_re.shape[0]
    lr, li = lam_re.astype(F32), lam_im.astype(F32)
    dt = jnp.exp(log_dt.astype(F32))[..., None]
    mag = jnp.exp(lr * dt)
    ar = mag * jnp.cos(li * dt)
    ai = mag * jnp.sin(li * dt)
    den = lr * lr + li * li
    zr = ((ar - 1.0) * lr + ai * li) / den
    zi = (ai * lr - (ar - 1.0) * li) / den
    br_, bi_ = bmat_re.astype(F32), bmat_im.astype(F32)
    bb_re = zr[..., None] * br_ - zi[..., None] * bi_
    bb_im = zr[..., None] * bi_ + zi[..., None] * br_
    gpb = LANE // GROUP_C
    nb = G_C // gpb
    eye = jnp.eye(gpb, dtype=F32)

    def pack_b(bb):
        bb = bb.reshape(n, nb, gpb, P_C, GROUP_C)
        return jnp.einsum("ag,nbapc->nbacgp", eye, bb).reshape(n, nb, LANE, S5_LANES).astype(BF16)

    def pack_c(cm):
        cm = cm.astype(F32).reshape(n, nb, gpb, GROUP_C, P_C)
        return jnp.einsum("ag,nbacp->nbapgc", eye, cm).reshape(n, nb, S5_LANES, LANE).astype(BF16)

    return {"ar": ar.reshape(n, 1, STATE_C), "ai": ai.reshape(n, 1, STATE_C),
            "bre": pack_b(bb_re), "bim": pack_b(bb_im), "cre": pack_c(cmat_re), "cim": pack_c(cmat_im),
            "d": d_skip.astype(F32).reshape(n, 1, D_C)}


def _trunk(x, seg, st, w, n_even):
    new = {k: [] for k in ("a_conv", "a_h", "b_S", "c_re", "c_im", "f_conv")}
    s_buf = None
    lay = lambda name, idx: (st[name], min(idx, st[name].shape[0] - 1))
    for l in range(DEPTH):
        if l % 2 == 0:
            i = l // 2
            z = _matmul([x], [(w["w_in_e"], i, 0)], 6 * D_A, tm=1024, tn=1024, norm_g=(w["g_mix"], l),
                        name="in_proj_even")
            out_a, nb, hl = _rglru(z, lay("a_conv", i), lay("a_h", i), w, i, seg)
            if seg.sh == 1:
                out_b, s_new = _hgrn2(z, lay("b_S", i), w, i, seg)
                new["b_S"].append(s_new)
            else:
                out_b, s_buf = _hgrn2(z, lay("b_S", i), w, i, seg, n_slabs=n_even, slab=i, s_buf=s_buf)
            x = _matmul([out_a, out_b], [(w["w_out_e"], i, 0), (w["w_out_e"], i, 1)], D_MODEL, tm=512, tn=D_MODEL, res=x,
                        name="out_proj_even")
            new["a_conv"].append(nb)
            new["a_h"].append(hl)
        else:
            j = l // 2
            u = _matmul([x], [(w["w_in_c"], j, 0)], D_C, tm=1024, tn=D_C, norm_g=(w["g_mix"], l), name="in_proj_odd")
            y, sr, si = _s5(u, lay("c_re", j), lay("c_im", j), w["s5"], j, seg)
            x = _matmul([y, y], [(w["w_glu_v"], j, 0), (w["w_glu_g"], j, 0)], D_MODEL, tm=512, tn=D_MODEL, res=x, glu=True,
                        name="glu_odd")
            new["c_re"].append(sr)
            new["c_im"].append(si)
        act, tail = _ffn_up(x, w["g_ffn"], lay("f_conv", l), w["w_up"], w["w_gate"], w["conv_f_w"],
                            w["conv_f_b"], l, seg)
        x = _matmul([act], [(w["w_down"], l, 0)], D_MODEL, tm=1024, tn=512, res=x, name="ffn_down")
        new["f_conv"].append(tail)
    if s_buf is not None:
        new["b_S"] = s_buf
    return _rmsnorm(x, w["g_final"]), new


def kernel(x_prompt, x_sample, state_a_conv, state_a_h, state_b_S, state_c_re, state_c_im, state_ffn_conv,
           meta_tokens, g_mix, w_in_e, conv_a_w, conv_a_b, w_ra, b_ra, w_ia, b_ia, lam_a, lb_logits, gn_b,
           w_out_e, w_in_c, lam_re, lam_im, log_dt, bmat_re, bmat_im, cmat_re, cmat_im, d_skip,
           w_glu_v, w_glu_g, g_ffn, w_up, w_gate, conv_f_w, conv_f_b, w_down, g_final):
    n_even, n_odd = w_in_e.shape[0], w_in_c.shape[0]
    batch, seq = x_prompt.shape[0], x_prompt.shape[1]
    dec_batch, dec_seq = x_sample.shape[0], x_sample.shape[1]

    sm = jax.nn.softmax(lb_logits.astype(F32), axis=0)
    lb_all = jnp.clip(jnp.clip(jnp.cumsum(sm, axis=0) - sm[0:1], 0.0, 1.0), 0.0, 1.0)
    w = {
        "g_mix": g_mix.reshape(DEPTH, 1, D_MODEL), "g_ffn": g_ffn.reshape(DEPTH, 1, D_MODEL), "g_final": g_final,
        "w_in_e": w_in_e.astype(BF16), "w_out_e": w_out_e.astype(BF16),
        "conv_a_w": conv_a_w, "conv_a_b": conv_a_b.reshape(n_even, 1, D_A),
        "w_ra": w_ra.astype(BF16), "b_ra": b_ra.reshape(n_even, 1, D_A),
        "w_ia": w_ia.astype(BF16), "b_ia": b_ia.reshape(n_even, 1, D_A), "lam_a": lam_a.reshape(n_even, 1, D_A),
        "lb_all": lb_all.reshape(n_even, 1, D_B), "gn_b": gn_b.reshape(n_even, 1, D_B),
        "w_in_c": w_in_c.astype(BF16), "w_glu_v": w_glu_v.astype(BF16), "w_glu_g": w_glu_g.astype(BF16),
        "s5": _s5_params(lam_re, lam_im, log_dt, bmat_re, bmat_im, cmat_re, cmat_im, d_skip),
        "w_up": w_up.astype(BF16), "w_gate": w_gate.astype(BF16), "w_down": w_down.astype(BF16),
        "conv_f_w": conv_f_w, "conv_f_b": conv_f_b.reshape(DEPTH, 1, D_FF),
    }

    meta_seg = Seg(G=1, sh=1, T=N_META, reset_first=True, share_state=False)
    zero = {
        "a_conv": jnp.zeros((1, 1, CONV_A - 1, D_A), F32),
        "a_h": jnp.zeros((1, 1, 1, D_A), F32),
        "b_S": jnp.zeros((1, 1, H_B, HEAD, HEAD), F32),
        "c_re": jnp.zeros((1, 1, 1, STATE_C), F32),
        "c_im": jnp.zeros((1, 1, 1, STATE_C), F32),
        "f_conv": jnp.zeros((1, 1, CONV_F - 1, D_FF), F32),
    }
    _, meta_st = _trunk(meta_tokens.astype(F32), meta_seg, zero, w, n_even)

    p_seg = Seg(G=batch, sh=1, T=seq, reset_first=False, share_state=True)
    p_init = {k: jnp.stack(v) for k, v in meta_st.items()}
    yp, p_st = _trunk(x_prompt.reshape(batch * seq, D_MODEL), p_seg, p_init, w, n_even)

    s_seg = Seg(G=1, sh=dec_batch, T=dec_seq, reset_first=False, share_state=False)
    s_init = {
        "a_conv": jnp.swapaxes(state_a_conv, 1, 2).reshape(n_even, 1, (CONV_A - 1) * dec_batch, D_A),
        "a_h": state_a_h.reshape(n_even, 1, dec_batch, D_A),
        "b_S": state_b_S,
        "c_re": state_c_re.reshape(n_odd, 1, dec_batch, STATE_C),
        "c_im": state_c_im.reshape(n_odd, 1, dec_batch, STATE_C),
        "f_conv": jnp.swapaxes(state_ffn_conv, 1, 2).reshape(DEPTH, 1, (CONV_F - 1) * dec_batch, D_FF),
    }
    ys, s_st = _trunk(jnp.swapaxes(x_sample, 0, 1).reshape(dec_seq * dec_batch, D_MODEL), s_seg, s_init, w, n_even)

    y_prompt = yp.reshape(batch, seq, D_MODEL)
    y_sample = jnp.swapaxes(ys.reshape(dec_seq, dec_batch, D_MODEL), 0, 1)
    p_out = (jnp.stack(p_st["a_conv"]),
             jnp.stack(p_st["a_h"]).reshape(n_even, batch, D_A),
             jnp.stack(p_st["b_S"]),
             jnp.stack(p_st["c_re"]).reshape(n_odd, batch, G_C, P_C),
             jnp.stack(p_st["c_im"]).reshape(n_odd, batch, G_C, P_C),
             jnp.stack(p_st["f_conv"]))
    s_out = (jnp.swapaxes(jnp.stack(s_st["a_conv"]).reshape(n_even, CONV_A - 1, dec_batch, D_A), 1, 2),
             jnp.stack(s_st["a_h"]).reshape(n_even, dec_batch, D_A),
             s_st["b_S"],
             jnp.stack(s_st["c_re"]).reshape(n_odd, dec_batch, G_C, P_C),
             jnp.stack(s_st["c_im"]).reshape(n_odd, dec_batch, G_C, P_C),
             jnp.swapaxes(jnp.stack(s_st["f_conv"]).reshape(DEPTH, CONV_F - 1, dec_batch, D_FF), 1, 2))
    return (y_prompt, y_sample) + p_out + s_out
```

```python
import functools
from typing import NamedTuple

import jax
import jax.numpy as jnp
from jax import lax
from jax.experimental import pallas as pl
from jax.experimental.pallas import tpu as pltpu

F32 = jnp.float32
BF16 = jnp.bfloat16

D_MODEL = 2048
DEPTH = 4
N_META = 16
D_A = 1024
CONV_A = 4
C_RG = 8.0
D_B = 1024
H_B = 8
HEAD = 128
LB_FLOOR = 1e-30
D_C = 1024
GROUP_C = 16
G_C = 64
P_C = 64
STATE_C = G_C * P_C
D_FF = 5504
CONV_F = 3
EPS = 1e-6
GLA_SAFE_DECAY = 60.0
GLA_MIN_HALF = 32

LANE = 128
SUBLANE = 8
S5_LANES = LANE * P_C // GROUP_C
VMEM_LIMIT = 52 * 1024 * 1024


class Seg(NamedTuple):
    G: int
    sh: int
    T: int
    reset_first: bool
    share_state: bool


def _params(n_axes):
    return pltpu.CompilerParams(dimension_semantics=("arbitrary",) * n_axes, vmem_limit_bytes=VMEM_LIMIT)


def _dot(a, b):
    return jnp.dot(a, b, preferred_element_type=F32)


def _dot_nt(a, b):
    return lax.dot_general(a, b, (((1,), (1,)), ((), ())), preferred_element_type=F32)


def _dot_tn(a, b):
    return lax.dot_general(a, b, (((0,), (0,)), ((), ())), preferred_element_type=F32)


def _softplus(x):
    return jnp.maximum(x, 0.0) + jnp.log1p(jnp.exp(-jnp.abs(x)))


def _rms(x, g):
    ms = jnp.mean(x * x, axis=-1, keepdims=True)
    return x * lax.rsqrt(ms + EPS) * g


def _delay(x, prev, k, sh):
    rows = x.shape[0]
    n = k * sh
    p = prev.shape[0]
    if sh % SUBLANE == 0:
        return jnp.concatenate([prev[p - n:], x[:rows - n]], axis=0)
    assert sh == 1
    y = pltpu.roll(x, n, 0)
    row = lax.broadcasted_iota(jnp.int32, x.shape, 0)
    for i in range(n):
        y = jnp.where(row == i, prev[p - n + i:p - n + i + 1], y)
    return y


def _row_tile(x, i):
    return jnp.broadcast_to(x[i:i + 1], (SUBLANE, x.shape[1]))


def _norm_kernel(x_ref, g_ref, o_ref):
    o_ref[...] = _rms(x_ref[...], g_ref[...]).astype(o_ref.dtype)


def _rmsnorm(x, g):
    rows = x.shape[0]
    tm = min(rows, 512)
    return pl.pallas_call(
        _norm_kernel,
        grid=(rows // tm,),
        in_specs=[pl.BlockSpec((tm, D_MODEL), lambda i: (i, 0)),
                  pl.BlockSpec((1, D_MODEL), lambda i: (0, 0))],
        out_specs=pl.BlockSpec((tm, D_MODEL), lambda i: (i, 0)),
        out_shape=jax.ShapeDtypeStruct((rows, D_MODEL), F32),
        compiler_params=_params(1),
        name="rmsnorm",
    )(x, g.reshape(1, D_MODEL))


def _mm_kernel(*refs, n_sets, n_in, has_res, glu, norm):
    pos = n_sets * n_in
    ws = refs[pos:pos + n_in]
    pos += n_in
    if norm:
        g_ref = refs[pos]
        pos += 1
    res_refs = refs[pos:pos + n_sets] if has_res else (None,) * n_sets
    pos += n_sets if has_res else 0
    out_refs = refs[pos:pos + n_sets]
    h_scs = refs[pos + n_sets:]
    for s in range(n_sets):
        xs = list(refs[s * n_in:(s + 1) * n_in])
        if norm:
            @pl.when(pl.program_id(1) == 0)
            def _(x_ref=xs[0], h_sc=h_scs[s]):
                h_sc[...] = _rms(x_ref[...], g_ref[...]).astype(BF16)

            xs[0] = h_scs[s]
        if glu:
            acc = _dot(xs[0][...], ws[0][...]) * jax.nn.sigmoid(_dot(xs[1][...], ws[1][...]))
        else:
            acc = _dot(xs[0][...], ws[0][...])
            for x_ref, w_ref in zip(xs[1:], ws[1:]):
                acc = acc + _dot(x_ref[...], w_ref[...])
        if has_res:
            acc = res_refs[s][...] + acc
        out_refs[s][...] = acc.astype(out_refs[s].dtype)


def _matmul(xs_sets, ws, n_out, *, tm, tn, res_sets=None, glu=False, norm_g=None, name="matmul"):
    n_sets, n_in = len(xs_sets), len(ws)
    rows = [xs[0].shape[0] for xs in xs_sets]
    tms = [min(r, tm) for r in rows] if n_sets == 1 else rows
    in_specs, args, scratch = [], [], []
    for xs, t in zip(xs_sets, tms):
        in_specs += [pl.BlockSpec((t, x.shape[1]), lambda i, j: (i, 0)) for x in xs]
        args += list(xs)
    in_specs += [pl.BlockSpec((None, x.shape[1], tn), functools.partial(lambda i, j, l, kb: (l, kb, j), l=l, kb=kb))
                 for x, (_, l, kb) in zip(xs_sets[0], ws)]
    args += [w for w, _, _ in ws]
    if norm_g is not None:
        g, gl = norm_g
        in_specs.append(pl.BlockSpec((None, 1, D_MODEL), lambda i, j: (gl, 0, 0)))
        args.append(g)
        scratch = [pltpu.VMEM((t, D_MODEL), BF16) for t in tms]
    if res_sets is not None:
        in_specs += [pl.BlockSpec((t, tn), lambda i, j: (i, j)) for t in tms]
        args += list(res_sets)
    return pl.pallas_call(
        functools.partial(_mm_kernel, n_sets=n_sets, n_in=n_in, has_res=res_sets is not None, glu=glu,
                          norm=norm_g is not None),
        grid=(rows[0] // tms[0], n_out // tn),
        in_specs=in_specs,
        out_specs=[pl.BlockSpec((t, tn), lambda i, j: (i, j)) for t in tms],
        out_shape=[jax.ShapeDtypeStruct((r, n_out), F32) for r in rows],
        scratch_shapes=scratch,
        compiler_params=_params(2),
        name=name,
    )(*args)


def _ffn_up_kernel(*refs, shapes):
    n_in = sum(3 if nblk > 1 else 2 for _, _, nblk in shapes)
    g_ref, wu_ref, wg_ref, cw_ref, cb_ref = refs[n_in:n_in + 5]
    ins, outs, scs = list(refs[:n_in]), list(refs[n_in + 5:n_in + 5 + 2 * len(shapes)]), \
        list(refs[n_in + 5 + 2 * len(shapes):])
    for sh, rows, nblk in shapes:
        x_ref = ins.pop(0)
        xp_ref = ins.pop(0) if nblk > 1 else None
        fb_ref = ins.pop(0)
        act_ref, tail_ref = outs.pop(0), outs.pop(0)
        h_sc = scs.pop(0)
        hp_sc = scs.pop(0) if nblk > 1 else None

        @pl.when(pl.program_id(1) == 0)
        def _(x_ref=x_ref, xp_ref=xp_ref, h_sc=h_sc, hp_sc=hp_sc):
            h_sc[...] = _rms(x_ref[...], g_ref[...]).astype(BF16)
            if xp_ref is not None:
                hp_sc[...] = _rms(xp_ref[...], g_ref[...]).astype(BF16)

        h = h_sc[...]
        up = _dot(h, wu_ref[...])
        gate = _dot(h, wg_ref[...])
        prev = fb_ref[...]
        if nblk > 1:
            up_prev = _dot(hp_sc[...], wu_ref[...])[SUBLANE - 2:]
            prev = jnp.where(pl.program_id(0) % nblk == 0, prev, up_prev)
        upc = (cb_ref[...] + cw_ref[0:1, :] * _delay(up, prev, 2, sh)
               + cw_ref[1:2, :] * _delay(up, prev, 1, sh) + cw_ref[2:3, :] * up)
        act_ref[...] = (jax.nn.gelu(upc) * gate).astype(act_ref.dtype)
        tail_ref[0] = up[rows - 2 * sh:]


def _ffn_up(xs, g, fbufs, w_up, w_gate, cw, cb, l, segs):
    tf = 512
    shapes, in_specs, args, scratch, out_specs, out_shape = [], [], [], [], [], []
    for x, (fb, fl), seg in zip(xs, fbufs, segs):
        G, sh, T = seg.G, seg.sh, seg.T
        tt = min(T, 1024 // sh)
        rows, nblk = tt * sh, T // tt
        assert len(xs) == 1 or G * nblk == 1
        shapes.append((sh, rows, nblk))
        gi = (lambda i: 0) if seg.share_state else functools.partial(lambda i, nblk: i // nblk, nblk=nblk)
        in_specs.append(pl.BlockSpec((rows, D_MODEL), lambda i, j: (i, 0)))
        args.append(x)
        scratch.append(pltpu.VMEM((rows, D_MODEL), BF16))
        if nblk > 1:
            in_specs.append(pl.BlockSpec((SUBLANE, D_MODEL), functools.partial(
                lambda i, j, per: (jnp.maximum(i * per - 1, 0), 0), per=rows // SUBLANE)))
            args.append(x)
            scratch.append(pltpu.VMEM((SUBLANE, D_MODEL), BF16))
        in_specs.append(pl.BlockSpec((None, None, 2 * sh, tf),
                                     functools.partial(lambda i, j, fl, gi: (fl, gi(i), 0, j), fl=fl, gi=gi)))
        args.append(fb)
        out_specs += [pl.BlockSpec((rows, tf), lambda i, j: (i, j)),
                      pl.BlockSpec((1, 2 * sh, tf), lambda i, j: (i, 0, j))]
        out_shape += [jax.ShapeDtypeStruct((G * T * sh, D_FF), BF16),
                      jax.ShapeDtypeStruct((G * nblk, 2 * sh, D_FF), F32)]
    in_specs += [pl.BlockSpec((None, 1, D_MODEL), lambda i, j: (l, 0, 0)),
                 pl.BlockSpec((None, D_MODEL, tf), lambda i, j: (l, 0, j)),
                 pl.BlockSpec((None, D_MODEL, tf), lambda i, j: (l, 0, j)),
                 pl.BlockSpec((None, CONV_F, tf), lambda i, j: (l, 0, j)),
                 pl.BlockSpec((None, 1, tf), lambda i, j: (l, 0, j))]
    args += [g, w_up, w_gate, cw, cb]
    n_row_blocks = segs[0].G * shapes[0][2]
    outs = pl.pallas_call(
        functools.partial(_ffn_up_kernel, shapes=tuple(shapes)),
        grid=(n_row_blocks, pl.cdiv(D_FF, tf)),
        in_specs=in_specs,
        out_specs=out_specs,
        out_shape=out_shape,
        scratch_shapes=scratch,
        compiler_params=_params(2),
        name="ffn_up",
    )(*args)
    return [(outs[2 * s], outs[2 * s + 1][nblk - 1::nblk]) for s, (_, _, nblk) in enumerate(shapes)]


def _rglru_kernel(xa_ref, ga_ref, cst_ref, h0_ref, cw_ref, cb_ref, wr_ref, br_ref, wi_ref, bi_ref, lam_ref,
                  oa_ref, cso_ref, ho_ref, prev_sc, h_sc, a_sc, b_sc, *, sh, tt, heads, reset_first):
    tb = pl.program_id(2)
    rows = tt * sh

    @pl.when(tb == 0)
    def _():
        prev_sc[...] = cst_ref[...]
        h_sc[...] = h0_ref[...]

    xa = xa_ref[...]
    prev = prev_sc[...]
    xc = cb_ref[...] + cw_ref[CONV_A - 1:CONV_A, :] * xa
    for k in range(CONV_A - 1):
        xc = xc + cw_ref[k:k + 1, :] * _delay(xa, prev, CONV_A - 1 - k, sh)
    new_prev = xa[rows - (CONV_A - 1) * sh:]
    prev_sc[...] = new_prev
    cso_ref[0] = new_prev

    row = lax.broadcasted_iota(jnp.int32, (rows, HEAD), 0)
    for hd in range(heads):
        cs = slice(hd * HEAD, (hd + 1) * HEAD)
        xh = xc[:, cs]
        xb = xh.astype(BF16)
        r = jax.nn.sigmoid(_dot(xb, wr_ref[hd]) + br_ref[:, cs])
        ig = jax.nn.sigmoid(_dot(xb, wi_ref[hd]) + bi_ref[:, cs])
        a = jnp.exp(-C_RG * r * _softplus(-lam_ref[:, cs]))
        mult = jnp.sqrt(1.0 - a * a)
        if reset_first:
            mult = jnp.where((row < sh) & (tb == 0), 1.0, mult)
        a_sc[:, cs] = a
        b_sc[:, cs] = mult * ig * xh

    if sh == 1:
        width = a_sc.shape[1]
        a3 = a_sc[...].reshape(rows // SUBLANE, SUBLANE, width)
        b3 = b_sc[...].reshape(rows // SUBLANE, SUBLANE, width)
        sub = lax.broadcasted_iota(jnp.int32, a3.shape, 1)
        d = 1
        while d < SUBLANE:
            keep = sub >= d
            b3 = b3 + a3 * jnp.where(keep, pltpu.roll(b3, d, 1), 0.0)
            a3 = a3 * jnp.where(keep, pltpu.roll(a3, d, 1), 1.0)
            d *= 2
        a_sc[...] = a3.reshape(rows, width)
        b_sc[...] = b3.reshape(rows, width)

        def tile_step(k, h):
            sl = pl.ds(pl.multiple_of(k * SUBLANE, SUBLANE), SUBLANE)
            ht = b_sc[sl, :] + a_sc[sl, :] * jnp.broadcast_to(h, (SUBLANE, width))
            b_sc[sl, :] = ht
            return ht[SUBLANE - 1:]

        h = lax.fori_loop(0, rows // SUBLANE, tile_step, h_sc[...], unroll=min(4, rows // SUBLANE))
    else:
        def step(t, h):
            sl = pl.ds(pl.multiple_of(t * sh, sh), sh)
            h = a_sc[sl, :] * h + b_sc[sl, :]
            b_sc[sl, :] = h
            return h

        h = lax.fori_loop(0, tt, step, h_sc[...], unroll=True)
    h_sc[...] = h
    ho_ref[0] = h
    oa_ref[...] = (b_sc[...] * jax.nn.gelu(ga_ref[...])).astype(oa_ref.dtype)


def _rglru(z, a_conv, a_h, prm, l, seg):
    G, sh, T = seg.G, seg.sh, seg.T
    tt = min(T, 512) if sh == 1 else T
    rows, nblk = tt * sh, T // tt
    heads = 8 if sh == 1 else 2
    cwid = heads * HEAD
    ncb = D_A // cwid
    (ac, acl), (ah, ahl) = a_conv, a_h
    gi = (lambda g: 0) if seg.share_state else (lambda g: g)
    row_blk = lambda c, g, tb: g * nblk + tb
    vec = lambda n: pl.BlockSpec((None, n, cwid), lambda c, g, tb: (l, 0, c))
    mat = pl.BlockSpec((None, heads, HEAD, HEAD), lambda c, g, tb: (l, c, 0, 0))
    return pl.pallas_call(
        functools.partial(_rglru_kernel, sh=sh, tt=tt, heads=heads, reset_first=seg.reset_first),
        grid=(ncb, G, nblk),
        in_specs=[pl.BlockSpec((rows, cwid), lambda c, g, tb: (row_blk(c, g, tb), c)),
                  pl.BlockSpec((rows, cwid), lambda c, g, tb: (row_blk(c, g, tb), ncb + c)),
                  pl.BlockSpec((None, None, (CONV_A - 1) * sh, cwid), lambda c, g, tb: (acl, gi(g), 0, c)),
                  pl.BlockSpec((None, None, sh, cwid), lambda c, g, tb: (ahl, gi(g), 0, c)),
                  vec(CONV_A), vec(1), mat, vec(1), mat, vec(1), vec(1)],
        out_specs=[pl.BlockSpec((rows, cwid), lambda c, g, tb: (row_blk(c, g, tb), c)),
                   pl.BlockSpec((1, (CONV_A - 1) * sh, cwid), lambda c, g, tb: (g, 0, c)),
                   pl.BlockSpec((1, sh, cwid), lambda c, g, tb: (g, 0, c))],
        out_shape=[jax.ShapeDtypeStruct((G * T * sh, D_A), BF16),
                   jax.ShapeDtypeStruct((G, (CONV_A - 1) * sh, D_A), F32),
                   jax.ShapeDtypeStruct((G, sh, D_A), F32)],
        scratch_shapes=[pltpu.VMEM(((CONV_A - 1) * sh, cwid), F32),
                        pltpu.VMEM((sh, cwid), F32),
                        pltpu.VMEM((rows, cwid), F32),
                        pltpu.VMEM((rows, cwid), F32)],
        compiler_params=_params(3),
        name="rglru",
    )(z, z, ac, ah, prm["conv_a_w"], prm["conv_a_b"], prm["w_ra"], prm["b_ra"], prm["w_ia"], prm["b_ia"],
      prm["lam_a"])


def _cumsum_rows(x):
    n = x.shape[0]
    row = lax.broadcasted_iota(jnp.int32, x.shape, 0)
    d = 1
    while d < n:
        x = x + jnp.where(row >= d, pltpu.roll(x, d, 0), 0.0)
        d *= 2
    return x


def _gla_pairwise(q, bc, k, v):
    n = q.shape[0]
    row = lax.broadcasted_iota(jnp.int32, (n, 1), 0)
    o = jnp.zeros((n, HEAD), F32)
    for s in range(n):
        diff = jnp.where(row >= s, bc - bc[s:s + 1], 0.0)
        w = q * k[s:s + 1] * jnp.exp(diff)
        a = jnp.where(row >= s, jnp.sum(w, axis=-1, keepdims=True), 0.0)
        o = o + a * v[s:s + 1]
    return o


def _gla_chunk_fast(q, bc, k, v, st):
    c = q.shape[0]
    h = c // 2
    b_mid = bc[h - 1:h]
    bl = jnp.concatenate([bc[:h], bc[h:] - b_mid], axis=0)
    qt = q * jnp.exp(bl)
    qtb = qt.astype(BF16)
    kt = (k * jnp.exp(-bl)).astype(BF16)
    vb = v.astype(BF16)
    row0 = lax.broadcasted_iota(jnp.int32, (h, h), 0)
    col0 = lax.broadcasted_iota(jnp.int32, (h, h), 1)
    a0 = jnp.where(row0 >= col0, _dot_nt(qtb[:h], kt[:h]), 0.0)
    o0 = _dot(a0.astype(BF16), vb[:h])
    kh = (k[:h] * jnp.exp(b_mid - bc[:h])).astype(BF16)
    row1 = lax.broadcasted_iota(jnp.int32, (h, c), 0) + h
    col1 = lax.broadcasted_iota(jnp.int32, (h, c), 1)
    a1 = jnp.where(row1 >= col1, _dot_nt(qtb[h:], jnp.concatenate([kh, kt[h:]], axis=0)), 0.0)
    o1 = _dot(a1.astype(BF16), vb)
    qs = jnp.concatenate([qt[:h], qt[h:] * jnp.exp(b_mid)], axis=0)
    o = jnp.concatenate([o0, o1], axis=0) + _dot_nt(qs.astype(BF16), st.astype(BF16))
    b_end = bc[c - 1:c]
    kd = k * jnp.exp(b_end - bc)
    st_new = st * jnp.exp(b_end) + _dot_tn(vb, kd.astype(BF16))
    return o, st_new


def _gla_chunk(q, bc, k, v, st, base):
    c = q.shape[0]
    o_state = _dot_nt((q * jnp.exp(bc)).astype(BF16), st.astype(BF16))
    parts = [_gla_pairwise(q[i:i + base], bc[i:i + base], k[i:i + base], v[i:i + base])
             for i in range(0, c, base)]
    s = base
    while s < c:
        for p in range(0, c, 2 * s):
            bm = bc[p + s - 1:p + s]
            qh = q[p + s:p + 2 * s] * jnp.exp(bc[p + s:p + 2 * s] - bm)
            kh = k[p:p + s] * jnp.exp(bm - bc[p:p + s])
            att = _dot_nt(qh.astype(BF16), kh.astype(BF16))
            upd = _dot(att.astype(BF16), v[p:p + s].astype(BF16))
            for i in range(s // base):
                idx = (p + s) // base + i
                parts[idx] = parts[idx] + upd[i * base:(i + 1) * base]
        s *= 2
    o = o_state + (parts[0] if len(parts) == 1 else jnp.concatenate(parts, axis=0))
    b_end = bc[c - 1:c]
    kd = k * jnp.exp(b_end - bc)
    st_new = st * jnp.exp(b_end) + _dot_tn(v.astype(BF16), kd.astype(BF16))
    return o, st_new


def _hgrn2_gates(fz, lb):
    la = jnp.log(jnp.maximum(lb, LB_FLOOR))
    lbv = jnp.log1p(-lb) + jnp.minimum(fz, 0.0) - jnp.log1p(jnp.exp(-jnp.abs(fz)))
    logf = jnp.maximum(la, lbv) + jnp.log1p(jnp.exp(-jnp.abs(la - lbv)))
    k = (1.0 - lb) * jax.nn.sigmoid(-fz)
    return logf, k


def _hgrn2_out(o, gn, gb):
    o = o * lax.rsqrt(jnp.mean(o * o, axis=-1, keepdims=True) + EPS)
    return o * gn * jax.nn.silu(gb)


def _hgrn2_long_kernel(q_ref, fz_ref, v_ref, gb_ref, s0_ref, lb_ref, gn_ref, ob_ref, so_ref, st_sc,
                       *, tt, chunk, base, nblk, hb):
    tb = pl.program_id(2)

    @pl.when(tb == 0)
    def _():
        for hd in range(hb):
            st_sc[hd] = s0_ref[hd].T

    def body(ci, carry):
        sl = pl.ds(pl.multiple_of(ci * chunk, chunk), chunk)
        heads = []
        for hd in range(hb):
            cs = slice(hd * HEAD, (hd + 1) * HEAD)
            logf, k = _hgrn2_gates(fz_ref[sl, cs], lb_ref[:, cs])
            heads.append((hd, cs, _cumsum_rows(logf), k))

        def run(chunk_fn):
            for hd, cs, bc, k in heads:
                o, st_new = chunk_fn(q_ref[sl, cs], bc, k, v_ref[sl, cs], st_sc[hd])
                st_sc[hd] = st_new
                ob_ref[sl, cs] = _hgrn2_out(o, gn_ref[:, cs], gb_ref[sl, cs]).astype(ob_ref.dtype)

        exact = functools.partial(_gla_chunk, base=base)
        if chunk < 2 * GLA_MIN_HALF:
            run(exact)
        else:
            half = chunk // 2
            decay = None
            for _, _, bc, _ in heads:
                d = jnp.maximum(-bc[half - 1:half], bc[half - 1:half] - bc[chunk - 1:chunk])
                decay = d if decay is None else jnp.maximum(decay, d)
            safe = jnp.max(decay) <= GLA_SAFE_DECAY
            pl.when(safe)(lambda: run(_gla_chunk_fast))
            pl.when(jnp.logical_not(safe))(lambda: run(exact))
        return carry

    lax.fori_loop(0, tt // chunk, body, 0)

    @pl.when(tb == nblk - 1)
    def _():
        for hd in range(hb):
            so_ref[0, hd] = st_sc[hd].T


def _hgrn2_wide_kernel(*refs, sh, T, nsb, unroll, first, n_slabs):
    if first:
        q_ref, fz_ref, v_ref, gb_ref, s0_ref, lb_ref, gn_ref, ob_ref, so_ref, o_sc = refs
        if n_slabs > 1:
            so_ref[1:] = jnp.zeros((n_slabs - 1, nsb, 1, HEAD, HEAD), F32)
        so = so_ref.at[0]
    else:
        q_ref, fz_ref, v_ref, gb_ref, s0_ref, lb_ref, gn_ref, _, ob_ref, so, o_sc = refs
    sb = pl.program_id(1)
    lb = lb_ref[...]

    def body(it, carry):
        for u in range(unroll):
            jj = it * unroll + u
            sl = pl.ds(sb * nsb + jj, T, stride=sh)
            logf, k = _hgrn2_gates(fz_ref[sl, :], lb)
            o, st_new = _gla_chunk(q_ref[sl, :], _cumsum_rows(logf), k, v_ref[sl, :], s0_ref[jj, 0].T, T)
            so[jj, 0] = st_new.T
            o_sc[sl, :] = o
        return carry

    lax.fori_loop(0, nsb // unroll, body, 0)

    @pl.when(sb == sh // nsb - 1)
    def _():
        ob_ref[...] = _hgrn2_out(o_sc[...], gn_ref[...], gb_ref[...]).astype(ob_ref.dtype)


def _hgrn2(z, s0, prm, l, seg, n_slabs=1, slab=0, s_buf=None):
    G, sh, T = seg.G, seg.sh, seg.T
    nseq = G * sh
    (sa, sl_) = s0
    if sh == 1:
        hb = 8
        wid = hb * HEAD
        q0, f0, v0, g0 = (2 * D_A // wid, (2 * D_A + D_B) // wid, (2 * D_A + 2 * D_B) // wid,
                          (2 * D_A + 3 * D_B) // wid)
        tt = min(T, 256)
        chunk = min(tt, 64)
        nblk = T // tt
        gi = (lambda g: 0) if seg.share_state else (lambda g: g)
        col = lambda c0: pl.BlockSpec((tt, wid), lambda h, g, tb: (g * nblk + tb, c0 + h))
        vec = pl.BlockSpec((None, 1, wid), lambda h, g, tb: (l, 0, h))
        return pl.pallas_call(
            functools.partial(_hgrn2_long_kernel, tt=tt, chunk=chunk, base=min(chunk, 16), nblk=nblk, hb=hb),
            grid=(H_B // hb, G, nblk),
            in_specs=[col(q0), col(f0), col(v0), col(g0),
                      pl.BlockSpec((None, None, hb, HEAD, HEAD), lambda h, g, tb: (sl_, gi(g), h, 0, 0)),
                      vec, vec],
            out_specs=[pl.BlockSpec((tt, wid), lambda h, g, tb: (g * nblk + tb, h)),
                       pl.BlockSpec((1, hb, HEAD, HEAD), lambda h, g, tb: (g, h, 0, 0))],
            out_shape=[jax.ShapeDtypeStruct((G * T * sh, D_B), BF16),
                       jax.ShapeDtypeStruct((nseq, H_B, HEAD, HEAD), F32)],
            scratch_shapes=[pltpu.VMEM((hb, HEAD, HEAD), F32)],
            compiler_params=_params(3),
            name="hgrn2_long",
        )(z, z, z, z, sa, prm["lb_all"], prm["gn_b"])
    assert G == 1 and T == SUBLANE
    q0, f0, v0, g0 = 2 * D_A // HEAD, (2 * D_A + D_B) // HEAD, (2 * D_A + 2 * D_B) // HEAD, (2 * D_A + 3 * D_B) // HEAD
    rows = T * sh
    nsb = min(sh, 32)
    unroll = 8 if nsb % 8 == 0 else 1
    col = lambda c0: pl.BlockSpec((rows, HEAD), lambda h, sb: (0, c0 + h))
    vec = pl.BlockSpec((None, 1, HEAD), lambda h, sb: (l, 0, h))
    in_specs = [col(q0), col(f0), col(v0), col(g0),
                pl.BlockSpec((None, nsb, 1, HEAD, HEAD), lambda h, sb: (sl_, sb, h, 0, 0)), vec, vec]
    args = [z, z, z, z, sa, prm["lb_all"], prm["gn_b"]]
    first = s_buf is None
    if first:
        so_spec = pl.BlockSpec((n_slabs, nsb, 1, HEAD, HEAD), lambda h, sb: (0, sb, h, 0, 0))
        aliases = {}
    else:
        in_specs.append(pl.BlockSpec(memory_space=pl.ANY))
        args.append(s_buf)
        so_spec = pl.BlockSpec((None, nsb, 1, HEAD, HEAD), lambda h, sb: (slab, sb, h, 0, 0))
        aliases = {len(args) - 1: 1}
    return pl.pallas_call(
        functools.partial(_hgrn2_wide_kernel, sh=sh, T=T, nsb=nsb, unroll=unroll, first=first, n_slabs=n_slabs),
        grid=(H_B, sh // nsb),
        in_specs=in_specs,
        out_specs=[pl.BlockSpec((rows, HEAD), lambda h, sb: (0, h)), so_spec],
        out_shape=[jax.ShapeDtypeStruct((G * T * sh, D_B), BF16),
                   jax.ShapeDtypeStruct((n_slabs, nseq, H_B, HEAD, HEAD), F32)],
        scratch_shapes=[pltpu.VMEM((rows, HEAD), F32)],
        input_output_aliases=aliases,
        compiler_params=_params(2),
        name="hgrn2_wide",
    )(*args)


def _cmul(ar, ai, br, bi):
    return ar * br - ai * bi, ar * bi + ai * br


def _s5_packed_scan(xr_sc, xi_sc, pwr_sc, pwi_sc, sr0, si0, ar, ai, seg_len):
    lanes = xr_sc.shape[1]
    row = lax.broadcasted_iota(jnp.int32, (SUBLANE, lanes), 0)
    art, ait = jnp.broadcast_to(ar, (SUBLANE, lanes)), jnp.broadcast_to(ai, (SUBLANE, lanes))

    def local(t, carry):
        sr, si = carry
        sl = pl.ds(pl.multiple_of(t * SUBLANE, SUBLANE), SUBLANE)
        pr, pi = _cmul(art, ait, sr, si)
        nr, ni = pr + xr_sc[sl, :], pi + xi_sc[sl, :]
        xr_sc[sl, :] = nr
        xi_sc[sl, :] = ni
        return nr, ni

    init = (jnp.where(row == 0, jnp.broadcast_to(sr0, (SUBLANE, lanes)), 0.0),
            jnp.where(row == 0, jnp.broadcast_to(si0, (SUBLANE, lanes)), 0.0))
    fr, fi = lax.fori_loop(0, seg_len, local, init, unroll=min(4, seg_len))

    last = seg_len * SUBLANE - 1
    alr, ali = pwr_sc[last:last + 1, :], pwi_sc[last:last + 1, :]
    cr, ci = fr[0:1], fi[0:1]
    car_r, car_i = jnp.zeros((SUBLANE, lanes), F32), jnp.zeros((SUBLANE, lanes), F32)
    for i in range(1, SUBLANE):
        car_r = jnp.where(row == i, jnp.broadcast_to(cr, (SUBLANE, lanes)), car_r)
        car_i = jnp.where(row == i, jnp.broadcast_to(ci, (SUBLANE, lanes)), car_i)
        pr, pi = _cmul(alr, ali, cr, ci)
        cr, ci = fr[i:i + 1] + pr, fi[i:i + 1] + pi

    def fix(t, carry):
        sl = pl.ds(pl.multiple_of(t * SUBLANE, SUBLANE), SUBLANE)
        dr, di = _cmul(pwr_sc[sl, :], pwi_sc[sl, :], car_r, car_i)
        xr_sc[sl, :] = xr_sc[sl, :] + dr
        xi_sc[sl, :] = xi_sc[sl, :] + di
        return carry

    lax.fori_loop(0, seg_len, fix, 0, unroll=min(4, seg_len))
    return cr, ci


def _s5_kernel(*refs, sh, tt):
    if sh == 1:
        (u_ref, sr0_ref, si0_ref, ar_ref, ai_ref, bre_ref, bim_ref, cre_ref, cim_ref, d_ref,
         y_ref, sro_ref, sio_ref, sr_sc, si_sc, xr_sc, xi_sc, up_sc, yp_sc, yt_sc, pwr_sc, pwi_sc) = refs
    else:
        (u_ref, sr0_ref, si0_ref, ar_ref, ai_ref, bre_ref, bim_ref, cre_ref, cim_ref, d_ref,
         y_ref, sro_ref, sio_ref, sr_sc, si_sc, xr_sc, xi_sc) = refs
    tb = pl.program_id(2)
    rows = tt * sh

    @pl.when(tb == 0)
    def _():
        sr_sc[...] = sr0_ref[...]
        si_sc[...] = si0_ref[...]

    if sh == 1:
        seg_len = rows // SUBLANE

        @pl.when((pl.program_id(1) == 0) & (tb == 0))
        def _():
            def grow(t, pw):
                sl = pl.ds(pl.multiple_of(t * SUBLANE, SUBLANE), SUBLANE)
                pwr_sc[sl, :] = jnp.broadcast_to(pw[0], (SUBLANE, S5_LANES))
                pwi_sc[sl, :] = jnp.broadcast_to(pw[1], (SUBLANE, S5_LANES))
                return _cmul(pw[0], pw[1], ar_ref[...], ai_ref[...])

            lax.fori_loop(0, seg_len, grow, (ar_ref[...], ai_ref[...]))

        def pack(t, carry):
            up_sc[pl.ds(pl.multiple_of(t * SUBLANE, SUBLANE), SUBLANE), :] = u_ref[pl.ds(t, SUBLANE, stride=seg_len), :]
            return carry

        lax.fori_loop(0, seg_len, pack, 0, unroll=min(8, seg_len))
        u_skip = up_sc[...]
    else:
        u_skip = u_ref[...]
    ub = u_skip.astype(BF16)
    xr_sc[...] = _dot(ub, bre_ref[...])
    xi_sc[...] = _dot(ub, bim_ref[...])

    if sh == 1:
        sr, si = _s5_packed_scan(xr_sc, xi_sc, pwr_sc, pwi_sc, sr_sc[...], si_sc[...], ar_ref[...], ai_ref[...],
                                 seg_len)
    else:
        ar = jnp.broadcast_to(ar_ref[...], (sh, S5_LANES))
        ai = jnp.broadcast_to(ai_ref[...], (sh, S5_LANES))

        def step(t, carry):
            sl = pl.ds(pl.multiple_of(t * sh, sh), sh)
            pr, pi = _cmul(ar, ai, carry[0], carry[1])
            nr, ni = pr + xr_sc[sl, :], pi + xi_sc[sl, :]
            xr_sc[sl, :] = nr
            xi_sc[sl, :] = ni
            return nr, ni

        sr, si = lax.fori_loop(0, tt, step, (sr_sc[...], si_sc[...]), unroll=True)
    sr_sc[...] = sr
    si_sc[...] = si
    sro_ref[0] = sr
    sio_ref[0] = si
    y = (_dot(xr_sc[...].astype(BF16), cre_ref[...]) - _dot(xi_sc[...].astype(BF16), cim_ref[...])
         + d_ref[...] * u_skip)
    out = jax.nn.gelu(y)
    if sh == 1:
        yp_sc[...] = out

        def unpack(t, carry):
            yt_sc[pl.ds(t, SUBLANE, stride=seg_len), :] = yp_sc[pl.ds(pl.multiple_of(t * SUBLANE, SUBLANE), SUBLANE), :]
            return carry

        lax.fori_loop(0, seg_len, unpack, 0, unroll=min(8, seg_len))
        out = yt_sc[...]
    y_ref[...] = out.astype(BF16)


def _s5(u, s_re, s_im, prm, j, seg):
    G, sh, T = seg.G, seg.sh, seg.T
    tt = min(T, 512) if sh == 1 else T
    rows, nblk = tt * sh, T // tt
    ncb = D_C // LANE
    (sr, srl), (si, sil) = s_re, s_im
    gi = (lambda g: 0) if seg.share_state else (lambda g: g)
    st_in = lambda lay: pl.BlockSpec((None, None, sh, S5_LANES), lambda c, g, tb: (lay, gi(g), 0, c))
    st_out = pl.BlockSpec((1, sh, S5_LANES), lambda c, g, tb: (g, 0, c))
    vec = pl.BlockSpec((None, 1, S5_LANES), lambda c, g, tb: (j, 0, c))
    bmat = pl.BlockSpec((None, None, LANE, S5_LANES), lambda c, g, tb: (j, c, 0, 0))
    cmat = pl.BlockSpec((None, None, S5_LANES, LANE), lambda c, g, tb: (j, c, 0, 0))
    scratch = [pltpu.VMEM((sh, S5_LANES), F32), pltpu.VMEM((sh, S5_LANES), F32),
               pltpu.VMEM((rows, S5_LANES), F32), pltpu.VMEM((rows, S5_LANES), F32)]
    if sh == 1:
        scratch += [pltpu.VMEM((rows, LANE), F32)] * 3
        scratch += [pltpu.VMEM((rows, S5_LANES), F32)] * 2
    return pl.pallas_call(
        functools.partial(_s5_kernel, sh=sh, tt=tt),
        grid=(ncb, G, nblk),
        in_specs=[pl.BlockSpec((rows, LANE), lambda c, g, tb: (g * nblk + tb, c)),
                  st_in(srl), st_in(sil), vec, vec, bmat, bmat, cmat, cmat,
                  pl.BlockSpec((None, 1, LANE), lambda c, g, tb: (j, 0, c))],
        out_specs=[pl.BlockSpec((rows, LANE), lambda c, g, tb: (g * nblk + tb, c)), st_out, st_out],
        out_shape=[jax.ShapeDtypeStruct((G * T * sh, D_C), BF16),
                   jax.ShapeDtypeStruct((G, sh, STATE_C), F32),
                   jax.ShapeDtypeStruct((G, sh, STATE_C), F32)],
        scratch_shapes=scratch,
        compiler_params=_params(3),
        name="s5",
    )(u, sr, si, prm["ar"], prm["ai"], prm["bre"], prm["bim"], prm["cre"], prm["cim"], prm["d"])


def _s5_params(lam_re, lam_im, log_dt, bmat_re, bmat_im, cmat_re, cmat_im, d_skip):
    n = lam_re.shape[0]
    lr, li = lam_re.astype(F32), lam_im.astype(F32)
    dt = jnp.exp(log_dt.astype(F32))[..., None]
    mag = jnp.exp(lr * dt)
    ar = mag * jnp.cos(li * dt)
    ai = mag * jnp.sin(li * dt)
    den = lr * lr + li * li
    zr = ((ar - 1.0) * lr + ai * li) / den
    zi = (ai * lr - (ar - 1.0) * li) / den
    br_, bi_ = bmat_re.astype(F32), bmat_im.astype(F32)
    bb_re = zr[..., None] * br_ - zi[..., None] * bi_
    bb_im = zr[..., None] * bi_ + zi[..., None] * br_
    gpb = LANE // GROUP_C
    nb = G_C // gpb
    eye = jnp.eye(gpb, dtype=F32)

    def pack_b(bb):
        bb = bb.reshape(n, nb, gpb, P_C, GROUP_C)
        return jnp.einsum("ag,nbapc->nbacgp", eye, bb).reshape(n, nb, LANE, S5_LANES).astype(BF16)

    def pack_c(cm):
        cm = cm.astype(F32).reshape(n, nb, gpb, GROUP_C, P_C)
        return jnp.einsum("ag,nbacp->nbapgc", eye, cm).reshape(n, nb, S5_LANES, LANE).astype(BF16)

    return {"ar": ar.reshape(n, 1, STATE_C), "ai": ai.reshape(n, 1, STATE_C),
            "bre": pack_b(bb_re), "bim": pack_b(bb_im), "cre": pack_c(cmat_re), "cim": pack_c(cmat_im),
            "d": d_skip.astype(F32).reshape(n, 1, D_C)}


def _trunk(xs, segs, sts, w, n_even):
    n = len(xs)
    news = [{k: [] for k in ("a_conv", "a_h", "b_S", "c_re", "c_im", "f_conv")} for _ in range(n)]
    s_bufs = [None] * n
    lay = lambda s, name, idx: (sts[s][name], min(idx, sts[s][name].shape[0] - 1))
    tiled = n == 1
    for l in range(DEPTH):
        if l % 2 == 0:
            i = l // 2
            zs = _matmul([[x] for x in xs], [(w["w_in_e"], i, 0)], 6 * D_A, tm=1024, tn=1024,
                         norm_g=(w["g_mix"], l), name="in_proj_even")
            mixed = []
            for s, (z, seg) in enumerate(zip(zs, segs)):
                out_a, nb, hl = _rglru(z, lay(s, "a_conv", i), lay(s, "a_h", i), w, i, seg)
                if seg.sh == 1:
                    out_b, s_new = _hgrn2(z, lay(s, "b_S", i), w, i, seg)
                    news[s]["b_S"].append(s_new)
                else:
                    out_b, s_bufs[s] = _hgrn2(z, lay(s, "b_S", i), w, i, seg, n_slabs=n_even, slab=i,
                                              s_buf=s_bufs[s])
                mixed.append([out_a, out_b])
                news[s]["a_conv"].append(nb)
                news[s]["a_h"].append(hl)
            xs = _matmul(mixed, [(w["w_out_e"], i, 0), (w["w_out_e"], i, 1)], D_MODEL, tm=512,
                         tn=D_MODEL if tiled else 512, res_sets=xs, name="out_proj_even")
        else:
            j = l // 2
            us = _matmul([[x] for x in xs], [(w["w_in_c"], j, 0)], D_C, tm=1024, tn=D_C if tiled else 512,
                         norm_g=(w["g_mix"], l), name="in_proj_odd")
            mixed = []
            for s, (u, seg) in enumerate(zip(us, segs)):
                y, sr, si = _s5(u, lay(s, "c_re", j), lay(s, "c_im", j), w["s5"], j, seg)
                mixed.append([y, y])
                news[s]["c_re"].append(sr)
                news[s]["c_im"].append(si)
            xs = _matmul(mixed, [(w["w_glu_v"], j, 0), (w["w_glu_g"], j, 0)], D_MODEL, tm=512,
                         tn=D_MODEL if tiled else 512, res_sets=xs, glu=True, name="glu_odd")
        ffn = _ffn_up(xs, w["g_ffn"], [lay(s, "f_conv", l) for s in range(n)], w["w_up"], w["w_gate"],
                      w["conv_f_w"], w["conv_f_b"], l, segs)
        xs = _matmul([[act] for act, _ in ffn], [(w["w_down"], l, 0)], D_MODEL, tm=1024, tn=512, res_sets=xs,
                     name="ffn_down")
        for s, (_, tail) in enumerate(ffn):
            news[s]["f_conv"].append(tail)
    for s in range(n):
        if s_bufs[s] is not None:
            news[s]["b_S"] = s_bufs[s]
    return [_rmsnorm(x, w["g_final"]) for x in xs], news


def kernel(x_prompt, x_sample, state_a_conv, state_a_h, state_b_S, state_c_re, state_c_im, state_ffn_conv,
           meta_tokens, g_mix, w_in_e, conv_a_w, conv_a_b, w_ra, b_ra, w_ia, b_ia, lam_a, lb_logits, gn_b,
           w_out_e, w_in_c, lam_re, lam_im, log_dt, bmat_re, bmat_im, cmat_re, cmat_im, d_skip,
           w_glu_v, w_glu_g, g_ffn, w_up, w_gate, conv_f_w, conv_f_b, w_down, g_final):
    n_even, n_odd = w_in_e.shape[0], w_in_c.shape[0]
    batch, seq = x_prompt.shape[0], x_prompt.shape[1]
    dec_batch, dec_seq = x_sample.shape[0], x_sample.shape[1]

    sm = jax.nn.softmax(lb_logits.astype(F32), axis=0)
    lb_all = jnp.clip(jnp.clip(jnp.cumsum(sm, axis=0) - sm[0:1], 0.0, 1.0), 0.0, 1.0)
    w = {
        "g_mix": g_mix.reshape(DEPTH, 1, D_MODEL), "g_ffn": g_ffn.reshape(DEPTH, 1, D_MODEL), "g_final": g_final,
        "w_in_e": w_in_e.astype(BF16), "w_out_e": w_out_e.astype(BF16),
        "conv_a_w": conv_a_w, "conv_a_b": conv_a_b.reshape(n_even, 1, D_A),
        "w_ra": w_ra.astype(BF16), "b_ra": b_ra.reshape(n_even, 1, D_A),
        "w_ia": w_ia.astype(BF16), "b_ia": b_ia.reshape(n_even, 1, D_A), "lam_a": lam_a.reshape(n_even, 1, D_A),
        "lb_all": lb_all.reshape(n_even, 1, D_B), "gn_b": gn_b.reshape(n_even, 1, D_B),
        "w_in_c": w_in_c.astype(BF16), "w_glu_v": w_glu_v.astype(BF16), "w_glu_g": w_glu_g.astype(BF16),
        "s5": _s5_params(lam_re, lam_im, log_dt, bmat_re, bmat_im, cmat_re, cmat_im, d_skip),
        "w_up": w_up.astype(BF16), "w_gate": w_gate.astype(BF16), "w_down": w_down.astype(BF16),
        "conv_f_w": conv_f_w, "conv_f_b": conv_f_b.reshape(DEPTH, 1, D_FF),
    }

    meta_seg = Seg(G=1, sh=1, T=N_META, reset_first=True, share_state=False)
    zero = {
        "a_conv": jnp.zeros((1, 1, CONV_A - 1, D_A), F32),
        "a_h": jnp.zeros((1, 1, 1, D_A), F32),
        "b_S": jnp.zeros((1, 1, H_B, HEAD, HEAD), F32),
        "c_re": jnp.zeros((1, 1, 1, STATE_C), F32),
        "c_im": jnp.zeros((1, 1, 1, STATE_C), F32),
        "f_conv": jnp.zeros((1, 1, CONV_F - 1, D_FF), F32),
    }

    s_seg = Seg(G=1, sh=dec_batch, T=dec_seq, reset_first=False, share_state=False)
    s_init = {
        "a_conv": jnp.swapaxes(state_a_conv, 1, 2).reshape(n_even, 1, (CONV_A - 1) * dec_batch, D_A),
        "a_h": state_a_h.reshape(n_even, 1, dec_batch, D_A),
        "b_S": state_b_S,
        "c_re": state_c_re.reshape(n_odd, 1, dec_batch, STATE_C),
        "c_im": state_c_im.reshape(n_odd, 1, dec_batch, STATE_C),
        "f_conv": jnp.swapaxes(state_ffn_conv, 1, 2).reshape(DEPTH, 1, (CONV_F - 1) * dec_batch, D_FF),
    }
    (ys, _), (s_st, meta_st) = _trunk(
        [jnp.swapaxes(x_sample, 0, 1).reshape(dec_seq * dec_batch, D_MODEL), meta_tokens.astype(F32)],
        [s_seg, meta_seg], [s_init, zero], w, n_even)

    p_seg = Seg(G=batch, sh=1, T=seq, reset_first=False, share_state=True)
    p_init = {k: jnp.stack(v) for k, v in meta_st.items()}
    (yp,), (p_st,) = _trunk([x_prompt.reshape(batch * seq, D_MODEL)], [p_seg], [p_init], w, n_even)

    y_prompt = yp.reshape(batch, seq, D_MODEL)
    y_sample = jnp.swapaxes(ys.reshape(dec_seq, dec_batch, D_MODEL), 0, 1)
    p_out = (jnp.stack(p_st["a_conv"]),
             jnp.stack(p_st["a_h"]).reshape(n_even, batch, D_A),
             jnp.stack(p_st["b_S"]),
             jnp.stack(p_st["c_re"]).reshape(n_odd, batch, G_C, P_C),
             jnp.stack(p_st["c_im"]).reshape(n_odd, batch, G_C, P_C),
             jnp.stack(p_st["f_conv"]))
    s_out = (jnp.swapaxes(jnp.stack(s_st["a_conv"]).reshape(n_even, CONV_A - 1, dec_batch, D_A), 1, 2),
             jnp.stack(s_st["a_h"]).reshape(n_even, dec_batch, D_A),
             s_st["b_S"],
             jnp.stack(s_st["c_re"]).reshape(n_odd, dec_batch, G_C, P_C),
             jnp.stack(s_st["c_im"]).reshape(n_odd, dec_batch, G_C, P_C),
             jnp.swapaxes(jnp.stack(s_st["f_conv"]).reshape(DEPTH, CONV_F - 1, dec_batch, D_FF), 1, 2))
    return (y_prompt, y_sample) + p_out + s_out
```

```python
import functools
from typing import NamedTuple

import jax
import jax.numpy as jnp
from jax import lax
from jax.experimental import pallas as pl
from jax.experimental.pallas import tpu as pltpu

F32 = jnp.float32
BF16 = jnp.bfloat16

D_MODEL = 2048
DEPTH = 4
N_META = 16
D_A = 1024
CONV_A = 4
C_RG = 8.0
D_B = 1024
H_B = 8
HEAD = 128
LB_FLOOR = 1e-30
D_C = 1024
GROUP_C = 16
G_C = 64
P_C = 64
STATE_C = G_C * P_C
D_FF = 5504
CONV_F = 3
EPS = 1e-6
GLA_SAFE_DECAY = 60.0
GLA_MIN_HALF = 32

LANE = 128
SUBLANE = 8
S5_LANES = LANE * P_C // GROUP_C
VMEM_LIMIT = 52 * 1024 * 1024


class Seg(NamedTuple):
    G: int
    sh: int
    T: int
    reset_first: bool
    share_state: bool


def _params(n_axes):
    return pltpu.CompilerParams(dimension_semantics=("arbitrary",) * n_axes, vmem_limit_bytes=VMEM_LIMIT)


def _dot(a, b):
    return jnp.dot(a, b, preferred_element_type=F32)


def _dot_nt(a, b):
    return lax.dot_general(a, b, (((1,), (1,)), ((), ())), preferred_element_type=F32)


def _dot_tn(a, b):
    return lax.dot_general(a, b, (((0,), (0,)), ((), ())), preferred_element_type=F32)


def _softplus(x):
    return jnp.maximum(x, 0.0) + jnp.log1p(jnp.exp(-jnp.abs(x)))


def _rms(x, g):
    ms = jnp.mean(x * x, axis=-1, keepdims=True)
    return x * lax.rsqrt(ms + EPS) * g


def _delay(x, prev, k, sh):
    rows = x.shape[0]
    n = k * sh
    p = prev.shape[0]
    if sh % SUBLANE == 0:
        return jnp.concatenate([prev[p - n:], x[:rows - n]], axis=0)
    assert sh == 1
    y = pltpu.roll(x, n, 0)
    row = lax.broadcasted_iota(jnp.int32, x.shape, 0)
    for i in range(n):
        y = jnp.where(row == i, prev[p - n + i:p - n + i + 1], y)
    return y


def _row_tile(x, i):
    return jnp.broadcast_to(x[i:i + 1], (SUBLANE, x.shape[1]))


def _norm_kernel(x_ref, g_ref, o_ref):
    o_ref[...] = _rms(x_ref[...], g_ref[...]).astype(o_ref.dtype)


def _rmsnorm(x, g):
    rows = x.shape[0]
    tm = min(rows, 512)
    return pl.pallas_call(
        _norm_kernel,
        grid=(rows // tm,),
        in_specs=[pl.BlockSpec((tm, D_MODEL), lambda i: (i, 0)),
                  pl.BlockSpec((1, D_MODEL), lambda i: (0, 0))],
        out_specs=pl.BlockSpec((tm, D_MODEL), lambda i: (i, 0)),
        out_shape=jax.ShapeDtypeStruct((rows, D_MODEL), F32),
        compiler_params=_params(1),
        name="rmsnorm",
    )(x, g.reshape(1, D_MODEL))


def _mm_kernel(*refs, n_sets, n_in, has_res, glu, norm):
    pos = n_sets * n_in
    ws = refs[pos:pos + n_in]
    pos += n_in
    if norm:
        g_ref = refs[pos]
        pos += 1
    res_refs = refs[pos:pos + n_sets] if has_res else (None,) * n_sets
    pos += n_sets if has_res else 0
    out_refs = refs[pos:pos + n_sets]
    h_scs = refs[pos + n_sets:]
    for s in range(n_sets):
        xs = list(refs[s * n_in:(s + 1) * n_in])
        if norm:
            @pl.when(pl.program_id(1) == 0)
            def _(x_ref=xs[0], h_sc=h_scs[s]):
                h_sc[...] = _rms(x_ref[...], g_ref[...]).astype(BF16)

            xs[0] = h_scs[s]
        if glu:
            acc = _dot(xs[0][...], ws[0][...]) * jax.nn.sigmoid(_dot(xs[1][...], ws[1][...]))
        else:
            acc = _dot(xs[0][...], ws[0][...])
            for x_ref, w_ref in zip(xs[1:], ws[1:]):
                acc = acc + _dot(x_ref[...], w_ref[...])
        if has_res:
            acc = res_refs[s][...] + acc
        out_refs[s][...] = acc.astype(out_refs[s].dtype)


def _matmul(xs_sets, ws, n_out, *, tm, tn, res_sets=None, glu=False, norm_g=None, name="matmul"):
    n_sets, n_in = len(xs_sets), len(ws)
    rows = [xs[0].shape[0] for xs in xs_sets]
    tms = [min(r, tm) for r in rows] if n_sets == 1 else rows
    in_specs, args, scratch = [], [], []
    for xs, t in zip(xs_sets, tms):
        in_specs += [pl.BlockSpec((t, x.shape[1]), lambda i, j: (i, 0)) for x in xs]
        args += list(xs)
    in_specs += [pl.BlockSpec((None, x.shape[1], tn), functools.partial(lambda i, j, l, kb: (l, kb, j), l=l, kb=kb))
                 for x, (_, l, kb) in zip(xs_sets[0], ws)]
    args += [w for w, _, _ in ws]
    if norm_g is not None:
        g, gl = norm_g
        in_specs.append(pl.BlockSpec((None, 1, D_MODEL), lambda i, j: (gl, 0, 0)))
        args.append(g)
        scratch = [pltpu.VMEM((t, D_MODEL), BF16) for t in tms]
    if res_sets is not None:
        in_specs += [pl.BlockSpec((t, tn), lambda i, j: (i, j)) for t in tms]
        args += list(res_sets)
    return pl.pallas_call(
        functools.partial(_mm_kernel, n_sets=n_sets, n_in=n_in, has_res=res_sets is not None, glu=glu,
                          norm=norm_g is not None),
        grid=(rows[0] // tms[0], n_out // tn),
        in_specs=in_specs,
        out_specs=[pl.BlockSpec((t, tn), lambda i, j: (i, j)) for t in tms],
        out_shape=[jax.ShapeDtypeStruct((r, n_out), F32) for r in rows],
        scratch_shapes=scratch,
        compiler_params=_params(2),
        name=name,
    )(*args)


def _ffn_up_kernel(*refs, shapes):
    n_in = sum(3 if nblk > 1 else 2 for _, _, nblk in shapes)
    g_ref, wu_ref, wg_ref, cw_ref, cb_ref = refs[n_in:n_in + 5]
    ins, outs, scs = list(refs[:n_in]), list(refs[n_in + 5:n_in + 5 + 2 * len(shapes)]), \
        list(refs[n_in + 5 + 2 * len(shapes):])
    for sh, rows, nblk in shapes:
        x_ref = ins.pop(0)
        xp_ref = ins.pop(0) if nblk > 1 else None
        fb_ref = ins.pop(0)
        act_ref, tail_ref = outs.pop(0), outs.pop(0)
        h_sc = scs.pop(0)
        hp_sc = scs.pop(0) if nblk > 1 else None

        @pl.when(pl.program_id(1) == 0)
        def _(x_ref=x_ref, xp_ref=xp_ref, h_sc=h_sc, hp_sc=hp_sc):
            h_sc[...] = _rms(x_ref[...], g_ref[...]).astype(BF16)
            if xp_ref is not None:
                hp_sc[...] = _rms(xp_ref[...], g_ref[...]).astype(BF16)

        h = h_sc[...]
        up = _dot(h, wu_ref[...])
        gate = _dot(h, wg_ref[...])
        prev = fb_ref[...]
        if nblk > 1:
            up_prev = _dot(hp_sc[...], wu_ref[...])[SUBLANE - 2:]
            prev = jnp.where(pl.program_id(0) % nblk == 0, prev, up_prev)
        upc = (cb_ref[...] + cw_ref[0:1, :] * _delay(up, prev, 2, sh)
               + cw_ref[1:2, :] * _delay(up, prev, 1, sh) + cw_ref[2:3, :] * up)
        act_ref[...] = (jax.nn.gelu(upc) * gate).astype(act_ref.dtype)
        tail_ref[0] = up[rows - 2 * sh:]


def _ffn_up(xs, g, fbufs, w_up, w_gate, cw, cb, l, segs):
    tf = 512
    shapes, in_specs, args, scratch, out_specs, out_shape = [], [], [], [], [], []
    for x, (fb, fl), seg in zip(xs, fbufs, segs):
        G, sh, T = seg.G, seg.sh, seg.T
        tt = min(T, 1024 // sh)
        rows, nblk = tt * sh, T // tt
        assert len(xs) == 1 or G * nblk == 1
        shapes.append((sh, rows, nblk))
        gi = (lambda i: 0) if seg.share_state else functools.partial(lambda i, nblk: i // nblk, nblk=nblk)
        in_specs.append(pl.BlockSpec((rows, D_MODEL), lambda i, j: (i, 0)))
        args.append(x)
        scratch.append(pltpu.VMEM((rows, D_MODEL), BF16))
        if nblk > 1:
            in_specs.append(pl.BlockSpec((SUBLANE, D_MODEL), functools.partial(
                lambda i, j, per: (jnp.maximum(i * per - 1, 0), 0), per=rows // SUBLANE)))
            args.append(x)
            scratch.append(pltpu.VMEM((SUBLANE, D_MODEL), BF16))
        in_specs.append(pl.BlockSpec((None, None, 2 * sh, tf),
                                     functools.partial(lambda i, j, fl, gi: (fl, gi(i), 0, j), fl=fl, gi=gi)))
        args.append(fb)
        out_specs += [pl.BlockSpec((rows, tf), lambda i, j: (i, j)),
                      pl.BlockSpec((1, 2 * sh, tf), lambda i, j: (i, 0, j))]
        out_shape += [jax.ShapeDtypeStruct((G * T * sh, D_FF), BF16),
                      jax.ShapeDtypeStruct((G * nblk, 2 * sh, D_FF), F32)]
    in_specs += [pl.BlockSpec((None, 1, D_MODEL), lambda i, j: (l, 0, 0)),
                 pl.BlockSpec((None, D_MODEL, tf), lambda i, j: (l, 0, j)),
                 pl.BlockSpec((None, D_MODEL, tf), lambda i, j: (l, 0, j)),
                 pl.BlockSpec((None, CONV_F, tf), lambda i, j: (l, 0, j)),
                 pl.BlockSpec((None, 1, tf), lambda i, j: (l, 0, j))]
    args += [g, w_up, w_gate, cw, cb]
    n_row_blocks = segs[0].G * shapes[0][2]
    outs = pl.pallas_call(
        functools.partial(_ffn_up_kernel, shapes=tuple(shapes)),
        grid=(n_row_blocks, pl.cdiv(D_FF, tf)),
        in_specs=in_specs,
        out_specs=out_specs,
        out_shape=out_shape,
        scratch_shapes=scratch,
        compiler_params=_params(2),
        name="ffn_up",
    )(*args)
    return [(outs[2 * s], outs[2 * s + 1][nblk - 1::nblk]) for s, (_, _, nblk) in enumerate(shapes)]


def _rglru_kernel(xa_ref, ga_ref, cst_ref, h0_ref, cw_ref, cb_ref, wr_ref, br_ref, wi_ref, bi_ref, lam_ref,
                  oa_ref, cso_ref, ho_ref, prev_sc, h_sc, a_sc, b_sc, *, sh, tt, heads, reset_first):
    tb = pl.program_id(2)
    rows = tt * sh

    @pl.when(tb == 0)
    def _():
        prev_sc[...] = cst_ref[...]
        h_sc[...] = h0_ref[...]

    xa = xa_ref[...]
    prev = prev_sc[...]
    xc = cb_ref[...] + cw_ref[CONV_A - 1:CONV_A, :] * xa
    for k in range(CONV_A - 1):
        xc = xc + cw_ref[k:k + 1, :] * _delay(xa, prev, CONV_A - 1 - k, sh)
    new_prev = xa[rows - (CONV_A - 1) * sh:]
    prev_sc[...] = new_prev
    cso_ref[0] = new_prev

    row = lax.broadcasted_iota(jnp.int32, (rows, HEAD), 0)
    for hd in range(heads):
        cs = slice(hd * HEAD, (hd + 1) * HEAD)
        xh = xc[:, cs]
        xb = xh.astype(BF16)
        r = jax.nn.sigmoid(_dot(xb, wr_ref[hd]) + br_ref[:, cs])
        ig = jax.nn.sigmoid(_dot(xb, wi_ref[hd]) + bi_ref[:, cs])
        a = jnp.exp(-C_RG * r * _softplus(-lam_ref[:, cs]))
        mult = jnp.sqrt(1.0 - a * a)
        if reset_first:
            mult = jnp.where((row < sh) & (tb == 0), 1.0, mult)
        a_sc[:, cs] = a
        b_sc[:, cs] = mult * ig * xh

    if sh == 1:
        width = a_sc.shape[1]
        a3 = a_sc[...].reshape(rows // SUBLANE, SUBLANE, width)
        b3 = b_sc[...].reshape(rows // SUBLANE, SUBLANE, width)
        sub = lax.broadcasted_iota(jnp.int32, a3.shape, 1)
        d = 1
        while d < SUBLANE:
            keep = sub >= d
            b3 = b3 + a3 * jnp.where(keep, pltpu.roll(b3, d, 1), 0.0)
            a3 = a3 * jnp.where(keep, pltpu.roll(a3, d, 1), 1.0)
            d *= 2
        a_sc[...] = a3.reshape(rows, width)
        b_sc[...] = b3.reshape(rows, width)

        def tile_step(k, h):
            sl = pl.ds(pl.multiple_of(k * SUBLANE, SUBLANE), SUBLANE)
            ht = b_sc[sl, :] + a_sc[sl, :] * jnp.broadcast_to(h, (SUBLANE, width))
            b_sc[sl, :] = ht
            return ht[SUBLANE - 1:]

        h = lax.fori_loop(0, rows // SUBLANE, tile_step, h_sc[...], unroll=min(4, rows // SUBLANE))
    else:
        def step(t, h):
            sl = pl.ds(pl.multiple_of(t * sh, sh), sh)
            h = a_sc[sl, :] * h + b_sc[sl, :]
            b_sc[sl, :] = h
            return h

        h = lax.fori_loop(0, tt, step, h_sc[...], unroll=True)
    h_sc[...] = h
    ho_ref[0] = h
    oa_ref[...] = (b_sc[...] * jax.nn.gelu(ga_ref[...])).astype(oa_ref.dtype)


def _rglru(z, a_conv, a_h, prm, l, seg):
    G, sh, T = seg.G, seg.sh, seg.T
    tt = min(T, 512) if sh == 1 else T
    rows, nblk = tt * sh, T // tt
    heads = 8 if sh == 1 else 2
    cwid = heads * HEAD
    ncb = D_A // cwid
    (ac, acl), (ah, ahl) = a_conv, a_h
    gi = (lambda g: 0) if seg.share_state else (lambda g: g)
    row_blk = lambda c, g, tb: g * nblk + tb
    vec = lambda n: pl.BlockSpec((None, n, cwid), lambda c, g, tb: (l, 0, c))
    mat = pl.BlockSpec((None, heads, HEAD, HEAD), lambda c, g, tb: (l, c, 0, 0))
    return pl.pallas_call(
        functools.partial(_rglru_kernel, sh=sh, tt=tt, heads=heads, reset_first=seg.reset_first),
        grid=(ncb, G, nblk),
        in_specs=[pl.BlockSpec((rows, cwid), lambda c, g, tb: (row_blk(c, g, tb), c)),
                  pl.BlockSpec((rows, cwid), lambda c, g, tb: (row_blk(c, g, tb), ncb + c)),
                  pl.BlockSpec((None, None, (CONV_A - 1) * sh, cwid), lambda c, g, tb: (acl, gi(g), 0, c)),
                  pl.BlockSpec((None, None, sh, cwid), lambda c, g, tb: (ahl, gi(g), 0, c)),
                  vec(CONV_A), vec(1), mat, vec(1), mat, vec(1), vec(1)],
        out_specs=[pl.BlockSpec((rows, cwid), lambda c, g, tb: (row_blk(c, g, tb), c)),
                   pl.BlockSpec((1, (CONV_A - 1) * sh, cwid), lambda c, g, tb: (g, 0, c)),
                   pl.BlockSpec((1, sh, cwid), lambda c, g, tb: (g, 0, c))],
        out_shape=[jax.ShapeDtypeStruct((G * T * sh, D_A), BF16),
                   jax.ShapeDtypeStruct((G, (CONV_A - 1) * sh, D_A), F32),
                   jax.ShapeDtypeStruct((G, sh, D_A), F32)],
        scratch_shapes=[pltpu.VMEM(((CONV_A - 1) * sh, cwid), F32),
                        pltpu.VMEM((sh, cwid), F32),
                        pltpu.VMEM((rows, cwid), F32),
                        pltpu.VMEM((rows, cwid), F32)],
        compiler_params=_params(3),
        name="rglru",
    )(z, z, ac, ah, prm["conv_a_w"], prm["conv_a_b"], prm["w_ra"], prm["b_ra"], prm["w_ia"], prm["b_ia"],
      prm["lam_a"])


def _cumsum_rows(x):
    n = x.shape[0]
    row = lax.broadcasted_iota(jnp.int32, x.shape, 0)
    d = 1
    while d < n:
        x = x + jnp.where(row >= d, pltpu.roll(x, d, 0), 0.0)
        d *= 2
    return x


def _gla_pairwise(q, bc, k, v):
    n = q.shape[0]
    row = lax.broadcasted_iota(jnp.int32, (n, 1), 0)
    o = jnp.zeros((n, HEAD), F32)
    for s in range(n):
        diff = jnp.where(row >= s, bc - bc[s:s + 1], 0.0)
        w = q * k[s:s + 1] * jnp.exp(diff)
        a = jnp.where(row >= s, jnp.sum(w, axis=-1, keepdims=True), 0.0)
        o = o + a * v[s:s + 1]
    return o


def _gla_chunk_fast(q, bc, k, v, st):
    c = q.shape[0]
    h = c // 2
    b_mid = bc[h - 1:h]
    bl = jnp.concatenate([bc[:h], bc[h:] - b_mid], axis=0)
    qt = q * jnp.exp(bl)
    qtb = qt.astype(BF16)
    kt = (k * jnp.exp(-bl)).astype(BF16)
    vb = v.astype(BF16)
    row0 = lax.broadcasted_iota(jnp.int32, (h, h), 0)
    col0 = lax.broadcasted_iota(jnp.int32, (h, h), 1)
    a0 = jnp.where(row0 >= col0, _dot_nt(qtb[:h], kt[:h]), 0.0)
    o0 = _dot(a0.astype(BF16), vb[:h])
    kh = (k[:h] * jnp.exp(b_mid - bc[:h])).astype(BF16)
    row1 = lax.broadcasted_iota(jnp.int32, (h, c), 0) + h
    col1 = lax.broadcasted_iota(jnp.int32, (h, c), 1)
    a1 = jnp.where(row1 >= col1, _dot_nt(qtb[h:], jnp.concatenate([kh, kt[h:]], axis=0)), 0.0)
    o1 = _dot(a1.astype(BF16), vb)
    qs = jnp.concatenate([qt[:h], qt[h:] * jnp.exp(b_mid)], axis=0)
    o = jnp.concatenate([o0, o1], axis=0) + _dot_nt(qs.astype(BF16), st.astype(BF16))
    b_end = bc[c - 1:c]
    kd = k * jnp.exp(b_end - bc)
    st_new = st * jnp.exp(b_end) + _dot_tn(vb, kd.astype(BF16))
    return o, st_new


def _gla_chunk(q, bc, k, v, st, base):
    c = q.shape[0]
    o_state = _dot_nt((q * jnp.exp(bc)).astype(BF16), st.astype(BF16))
    parts = [_gla_pairwise(q[i:i + base], bc[i:i + base], k[i:i + base], v[i:i + base])
             for i in range(0, c, base)]
    s = base
    while s < c:
        for p in range(0, c, 2 * s):
            bm = bc[p + s - 1:p + s]
            qh = q[p + s:p + 2 * s] * jnp.exp(bc[p + s:p + 2 * s] - bm)
            kh = k[p:p + s] * jnp.exp(bm - bc[p:p + s])
            att = _dot_nt(qh.astype(BF16), kh.astype(BF16))
            upd = _dot(att.astype(BF16), v[p:p + s].astype(BF16))
            for i in range(s // base):
                idx = (p + s) // base + i
                parts[idx] = parts[idx] + upd[i * base:(i + 1) * base]
        s *= 2
    o = o_state + (parts[0] if len(parts) == 1 else jnp.concatenate(parts, axis=0))
    b_end = bc[c - 1:c]
    kd = k * jnp.exp(b_end - bc)
    st_new = st * jnp.exp(b_end) + _dot_tn(v.astype(BF16), kd.astype(BF16))
    return o, st_new


def _gla_block_kv(q, bc, k, v, s):
    n = q.shape[0]
    o = _dot((q * jnp.exp(bc)).astype(BF16), s.astype(BF16)) + _gla_pairwise(q, bc, k, v)
    b_end = bc[n - 1:n]
    kd = k * jnp.exp(b_end - bc)
    decay_col = jnp.broadcast_to(jnp.exp(b_end), (SUBLANE, HEAD)).T[:, 0:1]
    s_new = s * decay_col + _dot_tn(kd.astype(BF16), v.astype(BF16))
    return o, s_new


def _hgrn2_gates(fz, lb):
    la = jnp.log(jnp.maximum(lb, LB_FLOOR))
    lbv = jnp.log1p(-lb) + jnp.minimum(fz, 0.0) - jnp.log1p(jnp.exp(-jnp.abs(fz)))
    logf = jnp.maximum(la, lbv) + jnp.log1p(jnp.exp(-jnp.abs(la - lbv)))
    k = (1.0 - lb) * jax.nn.sigmoid(-fz)
    return logf, k


def _hgrn2_out(o, gn, gb):
    o = o * lax.rsqrt(jnp.mean(o * o, axis=-1, keepdims=True) + EPS)
    return o * gn * jax.nn.silu(gb)


def _hgrn2_long_kernel(q_ref, fz_ref, v_ref, gb_ref, s0_ref, lb_ref, gn_ref, ob_ref, so_ref, st_sc,
                       *, tt, chunk, base, nblk, hb):
    tb = pl.program_id(2)

    @pl.when(tb == 0)
    def _():
        for hd in range(hb):
            st_sc[hd] = s0_ref[hd].T

    def body(ci, carry):
        sl = pl.ds(pl.multiple_of(ci * chunk, chunk), chunk)
        heads = []
        for hd in range(hb):
            cs = slice(hd * HEAD, (hd + 1) * HEAD)
            logf, k = _hgrn2_gates(fz_ref[sl, cs], lb_ref[:, cs])
            heads.append((hd, cs, _cumsum_rows(logf), k))

        def run(chunk_fn):
            for hd, cs, bc, k in heads:
                o, st_new = chunk_fn(q_ref[sl, cs], bc, k, v_ref[sl, cs], st_sc[hd])
                st_sc[hd] = st_new
                ob_ref[sl, cs] = _hgrn2_out(o, gn_ref[:, cs], gb_ref[sl, cs]).astype(ob_ref.dtype)

        exact = functools.partial(_gla_chunk, base=base)
        if chunk < 2 * GLA_MIN_HALF:
            run(exact)
        else:
            half = chunk // 2
            decay = None
            for _, _, bc, _ in heads:
                d = jnp.maximum(-bc[half - 1:half], bc[half - 1:half] - bc[chunk - 1:chunk])
                decay = d if decay is None else jnp.maximum(decay, d)
            safe = jnp.max(decay) <= GLA_SAFE_DECAY
            pl.when(safe)(lambda: run(_gla_chunk_fast))
            pl.when(jnp.logical_not(safe))(lambda: run(exact))
        return carry

    lax.fori_loop(0, tt // chunk, body, 0)

    @pl.when(tb == nblk - 1)
    def _():
        for hd in range(hb):
            so_ref[0, hd] = st_sc[hd].T


def _hgrn2_wide_kernel(*refs, sh, T, nsb, unroll, first, n_slabs):
    if first:
        q_ref, fz_ref, v_ref, gb_ref, s0_ref, lb_ref, gn_ref, ob_ref, so_ref, o_sc = refs
        if n_slabs > 1:
            so_ref[1:] = jnp.zeros((n_slabs - 1, nsb, 1, HEAD, HEAD), F32)
        so = so_ref.at[0]
    else:
        q_ref, fz_ref, v_ref, gb_ref, s0_ref, lb_ref, gn_ref, _, ob_ref, so, o_sc = refs
    sb = pl.program_id(1)
    lb = lb_ref[...]

    def body(it, carry):
        for u in range(unroll):
            jj = it * unroll + u
            sl = pl.ds(sb * nsb + jj, T, stride=sh)
            logf, k = _hgrn2_gates(fz_ref[sl, :], lb)
            o, s_new = _gla_block_kv(q_ref[sl, :], _cumsum_rows(logf), k, v_ref[sl, :], s0_ref[jj, 0])
            so[jj, 0] = s_new
            o_sc[sl, :] = o
        return carry

    lax.fori_loop(0, nsb // unroll, body, 0)

    @pl.when(sb == sh // nsb - 1)
    def _():
        ob_ref[...] = _hgrn2_out(o_sc[...], gn_ref[...], gb_ref[...]).astype(ob_ref.dtype)


def _hgrn2(z, s0, prm, l, seg, n_slabs=1, slab=0, s_buf=None):
    G, sh, T = seg.G, seg.sh, seg.T
    nseq = G * sh
    (sa, sl_) = s0
    if sh == 1:
        hb = 8
        wid = hb * HEAD
        q0, f0, v0, g0 = (2 * D_A // wid, (2 * D_A + D_B) // wid, (2 * D_A + 2 * D_B) // wid,
                          (2 * D_A + 3 * D_B) // wid)
        tt = min(T, 256)
        chunk = min(tt, 64)
        nblk = T // tt
        gi = (lambda g: 0) if seg.share_state else (lambda g: g)
        col = lambda c0: pl.BlockSpec((tt, wid), lambda h, g, tb: (g * nblk + tb, c0 + h))
        vec = pl.BlockSpec((None, 1, wid), lambda h, g, tb: (l, 0, h))
        return pl.pallas_call(
            functools.partial(_hgrn2_long_kernel, tt=tt, chunk=chunk, base=min(chunk, 16), nblk=nblk, hb=hb),
            grid=(H_B // hb, G, nblk),
            in_specs=[col(q0), col(f0), col(v0), col(g0),
                      pl.BlockSpec((None, None, hb, HEAD, HEAD), lambda h, g, tb: (sl_, gi(g), h, 0, 0)),
                      vec, vec],
            out_specs=[pl.BlockSpec((tt, wid), lambda h, g, tb: (g * nblk + tb, h)),
                       pl.BlockSpec((1, hb, HEAD, HEAD), lambda h, g, tb: (g, h, 0, 0))],
            out_shape=[jax.ShapeDtypeStruct((G * T * sh, D_B), BF16),
                       jax.ShapeDtypeStruct((nseq, H_B, HEAD, HEAD), F32)],
            scratch_shapes=[pltpu.VMEM((hb, HEAD, HEAD), F32)],
            compiler_params=_params(3),
            name="hgrn2_long",
        )(z, z, z, z, sa, prm["lb_all"], prm["gn_b"])
    assert G == 1 and T == SUBLANE
    q0, f0, v0, g0 = 2 * D_A // HEAD, (2 * D_A + D_B) // HEAD, (2 * D_A + 2 * D_B) // HEAD, (2 * D_A + 3 * D_B) // HEAD
    rows = T * sh
    nsb = min(sh, 32)
    unroll = 8 if nsb % 8 == 0 else 1
    col = lambda c0: pl.BlockSpec((rows, HEAD), lambda h, sb: (0, c0 + h))
    vec = pl.BlockSpec((None, 1, HEAD), lambda h, sb: (l, 0, h))
    in_specs = [col(q0), col(f0), col(v0), col(g0),
                pl.BlockSpec((None, nsb, 1, HEAD, HEAD), lambda h, sb: (sl_, sb, h, 0, 0)), vec, vec]
    args = [z, z, z, z, sa, prm["lb_all"], prm["gn_b"]]
    first = s_buf is None
    if first:
        so_spec = pl.BlockSpec((n_slabs, nsb, 1, HEAD, HEAD), lambda h, sb: (0, sb, h, 0, 0))
        aliases = {}
    else:
        in_specs.append(pl.BlockSpec(memory_space=pl.ANY))
        args.append(s_buf)
        so_spec = pl.BlockSpec((None, nsb, 1, HEAD, HEAD), lambda h, sb: (slab, sb, h, 0, 0))
        aliases = {len(args) - 1: 1}
    return pl.pallas_call(
        functools.partial(_hgrn2_wide_kernel, sh=sh, T=T, nsb=nsb, unroll=unroll, first=first, n_slabs=n_slabs),
        grid=(H_B, sh // nsb),
        in_specs=in_specs,
        out_specs=[pl.BlockSpec((rows, HEAD), lambda h, sb: (0, h)), so_spec],
        out_shape=[jax.ShapeDtypeStruct((G * T * sh, D_B), BF16),
                   jax.ShapeDtypeStruct((n_slabs, nseq, H_B, HEAD, HEAD), F32)],
        scratch_shapes=[pltpu.VMEM((rows, HEAD), F32)],
        input_output_aliases=aliases,
        compiler_params=_params(2),
        name="hgrn2_wide",
    )(*args)


def _cmul(ar, ai, br, bi):
    return ar * br - ai * bi, ar * bi + ai * br


def _s5_packed_scan(xr_sc, xi_sc, pwr_sc, pwi_sc, sr0, si0, ar, ai, seg_len):
    lanes = xr_sc.shape[1]
    row = lax.broadcasted_iota(jnp.int32, (SUBLANE, lanes), 0)
    art, ait = jnp.broadcast_to(ar, (SUBLANE, lanes)), jnp.broadcast_to(ai, (SUBLANE, lanes))

    def local(t, carry):
        sr, si = carry
        sl = pl.ds(pl.multiple_of(t * SUBLANE, SUBLANE), SUBLANE)
        pr, pi = _cmul(art, ait, sr, si)
        nr, ni = pr + xr_sc[sl, :], pi + xi_sc[sl, :]
        xr_sc[sl, :] = nr
        xi_sc[sl, :] = ni
        return nr, ni

    init = (jnp.where(row == 0, jnp.broadcast_to(sr0, (SUBLANE, lanes)), 0.0),
            jnp.where(row == 0, jnp.broadcast_to(si0, (SUBLANE, lanes)), 0.0))
    fr, fi = lax.fori_loop(0, seg_len, local, init, unroll=True)

    last = seg_len * SUBLANE - 1
    alr, ali = pwr_sc[last:last + 1, :], pwi_sc[last:last + 1, :]
    cr, ci = fr[0:1], fi[0:1]
    car_r, car_i = jnp.zeros((SUBLANE, lanes), F32), jnp.zeros((SUBLANE, lanes), F32)
    for i in range(1, SUBLANE):
        car_r = jnp.where(row == i, jnp.broadcast_to(cr, (SUBLANE, lanes)), car_r)
        car_i = jnp.where(row == i, jnp.broadcast_to(ci, (SUBLANE, lanes)), car_i)
        pr, pi = _cmul(alr, ali, cr, ci)
        cr, ci = fr[i:i + 1] + pr, fi[i:i + 1] + pi

    def fix(t, carry):
        sl = pl.ds(pl.multiple_of(t * SUBLANE, SUBLANE), SUBLANE)
        dr, di = _cmul(pwr_sc[sl, :], pwi_sc[sl, :], car_r, car_i)
        xr_sc[sl, :] = xr_sc[sl, :] + dr
        xi_sc[sl, :] = xi_sc[sl, :] + di
        return carry

    lax.fori_loop(0, seg_len, fix, 0, unroll=True)
    return cr, ci


def _s5_kernel(*refs, sh, tt):
    if sh == 1:
        (u_ref, sr0_ref, si0_ref, ar_ref, ai_ref, bre_ref, bim_ref, cre_ref, cim_ref, d_ref,
         y_ref, sro_ref, sio_ref, sr_sc, si_sc, xr_sc, xi_sc, up_sc, yp_sc, yt_sc, pwr_sc, pwi_sc) = refs
    else:
        (u_ref, sr0_ref, si0_ref, ar_ref, ai_ref, bre_ref, bim_ref, cre_ref, cim_ref, d_ref,
         y_ref, sro_ref, sio_ref, sr_sc, si_sc, xr_sc, xi_sc) = refs
    tb = pl.program_id(2)
    rows = tt * sh

    @pl.when(tb == 0)
    def _():
        sr_sc[...] = sr0_ref[...]
        si_sc[...] = si0_ref[...]

    if sh == 1:
        seg_len = rows // SUBLANE

        @pl.when((pl.program_id(1) == 0) & (tb == 0))
        def _():
            def grow(t, pw):
                sl = pl.ds(pl.multiple_of(t * SUBLANE, SUBLANE), SUBLANE)
                pwr_sc[sl, :] = jnp.broadcast_to(pw[0], (SUBLANE, S5_LANES))
                pwi_sc[sl, :] = jnp.broadcast_to(pw[1], (SUBLANE, S5_LANES))
                return _cmul(pw[0], pw[1], ar_ref[...], ai_ref[...])

            lax.fori_loop(0, seg_len, grow, (ar_ref[...], ai_ref[...]))

        def pack(t, carry):
            up_sc[pl.ds(pl.multiple_of(t * SUBLANE, SUBLANE), SUBLANE), :] = u_ref[pl.ds(t, SUBLANE, stride=seg_len), :]
            return carry

        lax.fori_loop(0, seg_len, pack, 0, unroll=True)
        u_skip = up_sc[...]
    else:
        u_skip = u_ref[...]
    ub = u_skip.astype(BF16)
    xr_sc[...] = _dot(ub, bre_ref[...])
    xi_sc[...] = _dot(ub, bim_ref[...])

    if sh == 1:
        sr, si = _s5_packed_scan(xr_sc, xi_sc, pwr_sc, pwi_sc, sr_sc[...], si_sc[...], ar_ref[...], ai_ref[...],
                                 seg_len)
    else:
        ar = jnp.broadcast_to(ar_ref[...], (sh, S5_LANES))
        ai = jnp.broadcast_to(ai_ref[...], (sh, S5_LANES))

        def step(t, carry):
            sl = pl.ds(pl.multiple_of(t * sh, sh), sh)
            pr, pi = _cmul(ar, ai, carry[0], carry[1])
            nr, ni = pr + xr_sc[sl, :], pi + xi_sc[sl, :]
            xr_sc[sl, :] = nr
            xi_sc[sl, :] = ni
            return nr, ni

        sr, si = lax.fori_loop(0, tt, step, (sr_sc[...], si_sc[...]), unroll=True)
    sr_sc[...] = sr
    si_sc[...] = si
    sro_ref[0] = sr
    sio_ref[0] = si
    y = (_dot(xr_sc[...].astype(BF16), cre_ref[...]) - _dot(xi_sc[...].astype(BF16), cim_ref[...])
         + d_ref[...] * u_skip)
    out = jax.nn.gelu(y)
    if sh == 1:
        yp_sc[...] = out

        def unpack(t, carry):
            yt_sc[pl.ds(t, SUBLANE, stride=seg_len), :] = yp_sc[pl.ds(pl.multiple_of(t * SUBLANE, SUBLANE), SUBLANE), :]
            return carry

        lax.fori_loop(0, seg_len, unpack, 0, unroll=True)
        out = yt_sc[...]
    y_ref[...] = out.astype(BF16)


def _s5(u, s_re, s_im, prm, j, seg):
    G, sh, T = seg.G, seg.sh, seg.T
    tt = min(T, 512) if sh == 1 else T
    rows, nblk = tt * sh, T // tt
    ncb = D_C // LANE
    (sr, srl), (si, sil) = s_re, s_im
    gi = (lambda g: 0) if seg.share_state else (lambda g: g)
    st_in = lambda lay: pl.BlockSpec((None, None, sh, S5_LANES), lambda c, g, tb: (lay, gi(g), 0, c))
    st_out = pl.BlockSpec((1, sh, S5_LANES), lambda c, g, tb: (g, 0, c))
    vec = pl.BlockSpec((None, 1, S5_LANES), lambda c, g, tb: (j, 0, c))
    bmat = pl.BlockSpec((None, None, LANE, S5_LANES), lambda c, g, tb: (j, c, 0, 0))
    cmat = pl.BlockSpec((None, None, S5_LANES, LANE), lambda c, g, tb: (j, c, 0, 0))
    scratch = [pltpu.VMEM((sh, S5_LANES), F32), pltpu.VMEM((sh, S5_LANES), F32),
               pltpu.VMEM((rows, S5_LANES), F32), pltpu.VMEM((rows, S5_LANES), F32)]
    if sh == 1:
        scratch += [pltpu.VMEM((rows, LANE), F32)] * 3
        scratch += [pltpu.VMEM((rows, S5_LANES), F32)] * 2
    return pl.pallas_call(
        functools.partial(_s5_kernel, sh=sh, tt=tt),
        grid=(ncb, G, nblk),
        in_specs=[pl.BlockSpec((rows, LANE), lambda c, g, tb: (g * nblk + tb, c)),
                  st_in(srl), st_in(sil), vec, vec, bmat, bmat, cmat, cmat,
                  pl.BlockSpec((None, 1, LANE), lambda c, g, tb: (j, 0, c))],
        out_specs=[pl.BlockSpec((rows, LANE), lambda c, g, tb: (g * nblk + tb, c)), st_out, st_out],
        out_shape=[jax.ShapeDtypeStruct((G * T * sh, D_C), BF16),
                   jax.ShapeDtypeStruct((G, sh, STATE_C), F32),
                   jax.ShapeDtypeStruct((G, sh, STATE_C), F32)],
        scratch_shapes=scratch,
        compiler_params=_params(3),
        name="s5",
    )(u, sr, si, prm["ar"], prm["ai"], prm["bre"], prm["bim"], prm["cre"], prm["cim"], prm["d"])


def _s5_params(lam_re, lam_im, log_dt, bmat_re, bmat_im, cmat_re, cmat_im, d_skip):
    n = lam_re.shape[0]
    lr, li = lam_re.astype(F32), lam_im.astype(F32)
    dt = jnp.exp(log_dt.astype(F32))[..., None]
    mag = jnp.exp(lr * dt)
    ar = mag * jnp.cos(li * dt)
    ai = mag * jnp.sin(li * dt)
    den = lr * lr + li * li
    zr = ((ar - 1.0) * lr + ai * li) / den
    zi = (ai * lr - (ar - 1.0) * li) / den
    br_, bi_ = bmat_re.astype(F32), bmat_im.astype(F32)
    bb_re = zr[..., None] * br_ - zi[..., None] * bi_
    bb_im = zr[..., None] * bi_ + zi[..., None] * br_
    gpb = LANE // GROUP_C
    nb = G_C // gpb
    eye = jnp.eye(gpb, dtype=F32)

    def pack_b(bb):
        bb = bb.reshape(n, nb, gpb, P_C, GROUP_C)
        return jnp.einsum("ag,nbapc->nbacgp", eye, bb).reshape(n, nb, LANE, S5_LANES).astype(BF16)

    def pack_c(cm):
        cm = cm.astype(F32).reshape(n, nb, gpb, GROUP_C, P_C)
        return jnp.einsum("ag,nbacp->nbapgc", eye, cm).reshape(n, nb, S5_LANES, LANE).astype(BF16)

    return {"ar": ar.reshape(n, 1, STATE_C), "ai": ai.reshape(n, 1, STATE_C),
            "bre": pack_b(bb_re), "bim": pack_b(bb_im), "cre": pack_c(cmat_re), "cim": pack_c(cmat_im),
            "d": d_skip.astype(F32).reshape(n, 1, D_C)}


def _trunk(xs, segs, sts, w, n_even):
    n = len(xs)
    news = [{k: [] for k in ("a_conv", "a_h", "b_S", "c_re", "c_im", "f_conv")} for _ in range(n)]
    s_bufs = [None] * n
    lay = lambda s, name, idx: (sts[s][name], min(idx, sts[s][name].shape[0] - 1))
    tiled = n == 1
    for l in range(DEPTH):
        if l % 2 == 0:
            i = l // 2
            zs = _matmul([[x] for x in xs], [(w["w_in_e"], i, 0)], 6 * D_A, tm=1024, tn=1024,
                         norm_g=(w["g_mix"], l), name="in_proj_even")
            mixed = []
            for s, (z, seg) in enumerate(zip(zs, segs)):
                out_a, nb, hl = _rglru(z, lay(s, "a_conv", i), lay(s, "a_h", i), w, i, seg)
                if seg.sh == 1:
                    out_b, s_new = _hgrn2(z, lay(s, "b_S", i), w, i, seg)
                    news[s]["b_S"].append(s_new)
                else:
                    out_b, s_bufs[s] = _hgrn2(z, lay(s, "b_S", i), w, i, seg, n_slabs=n_even, slab=i,
                                              s_buf=s_bufs[s])
                mixed.append([out_a, out_b])
                news[s]["a_conv"].append(nb)
                news[s]["a_h"].append(hl)
            xs = _matmul(mixed, [(w["w_out_e"], i, 0), (w["w_out_e"], i, 1)], D_MODEL, tm=512,
                         tn=D_MODEL if tiled else 512, res_sets=xs, name="out_proj_even")
        else:
            j = l // 2
            us = _matmul([[x] for x in xs], [(w["w_in_c"], j, 0)], D_C, tm=1024, tn=D_C if tiled else 512,
                         norm_g=(w["g_mix"], l), name="in_proj_odd")
            mixed = []
            for s, (u, seg) in enumerate(zip(us, segs)):
                y, sr, si = _s5(u, lay(s, "c_re", j), lay(s, "c_im", j), w["s5"], j, seg)
                mixed.append([y, y])
                news[s]["c_re"].append(sr)
                news[s]["c_im"].append(si)
            xs = _matmul(mixed, [(w["w_glu_v"], j, 0), (w["w_glu_g"], j, 0)], D_MODEL, tm=512,
                         tn=D_MODEL if tiled else 512, res_sets=xs, glu=True, name="glu_odd")
        ffn = _ffn_up(xs, w["g_ffn"], [lay(s, "f_conv", l) for s in range(n)], w["w_up"], w["w_gate"],
                      w["conv_f_w"], w["conv_f_b"], l, segs)
        xs = _matmul([[act] for act, _ in ffn], [(w["w_down"], l, 0)], D_MODEL, tm=1024, tn=512, res_sets=xs,
                     name="ffn_down")
        for s, (_, tail) in enumerate(ffn):
            news[s]["f_conv"].append(tail)
    for s in range(n):
        if s_bufs[s] is not None:
            news[s]["b_S"] = s_bufs[s]
    return [_rmsnorm(x, w["g_final"]) for x in xs], news


def kernel(x_prompt, x_sample, state_a_conv, state_a_h, state_b_S, state_c_re, state_c_im, state_ffn_conv,
           meta_tokens, g_mix, w_in_e, conv_a_w, conv_a_b, w_ra, b_ra, w_ia, b_ia, lam_a, lb_logits, gn_b,
           w_out_e, w_in_c, lam_re, lam_im, log_dt, bmat_re, bmat_im, cmat_re, cmat_im, d_skip,
           w_glu_v, w_glu_g, g_ffn, w_up, w_gate, conv_f_w, conv_f_b, w_down, g_final):
    n_even, n_odd = w_in_e.shape[0], w_in_c.shape[0]
    batch, seq = x_prompt.shape[0], x_prompt.shape[1]
    dec_batch, dec_seq = x_sample.shape[0], x_sample.shape[1]

    sm = jax.nn.softmax(lb_logits.astype(F32), axis=0)
    lb_all = jnp.clip(jnp.clip(jnp.cumsum(sm, axis=0) - sm[0:1], 0.0, 1.0), 0.0, 1.0)
    w = {
        "g_mix": g_mix.reshape(DEPTH, 1, D_MODEL), "g_ffn": g_ffn.reshape(DEPTH, 1, D_MODEL), "g_final": g_final,
        "w_in_e": w_in_e.astype(BF16), "w_out_e": w_out_e.astype(BF16),
        "conv_a_w": conv_a_w, "conv_a_b": conv_a_b.reshape(n_even, 1, D_A),
        "w_ra": w_ra.astype(BF16), "b_ra": b_ra.reshape(n_even, 1, D_A),
        "w_ia": w_ia.astype(BF16), "b_ia": b_ia.reshape(n_even, 1, D_A), "lam_a": lam_a.reshape(n_even, 1, D_A),
        "lb_all": lb_all.reshape(n_even, 1, D_B), "gn_b": gn_b.reshape(n_even, 1, D_B),
        "w_in_c": w_in_c.astype(BF16), "w_glu_v": w_glu_v.astype(BF16), "w_glu_g": w_glu_g.astype(BF16),
        "s5": _s5_params(lam_re, lam_im, log_dt, bmat_re, bmat_im, cmat_re, cmat_im, d_skip),
        "w_up": w_up.astype(BF16), "w_gate": w_gate.astype(BF16), "w_down": w_down.astype(BF16),
        "conv_f_w": conv_f_w, "conv_f_b": conv_f_b.reshape(DEPTH, 1, D_FF),
    }

    meta_seg = Seg(G=1, sh=1, T=N_META, reset_first=True, share_state=False)
    zero = {
        "a_conv": jnp.zeros((1, 1, CONV_A - 1, D_A), F32),
        "a_h": jnp.zeros((1, 1, 1, D_A), F32),
        "b_S": jnp.zeros((1, 1, H_B, HEAD, HEAD), F32),
        "c_re": jnp.zeros((1, 1, 1, STATE_C), F32),
        "c_im": jnp.zeros((1, 1, 1, STATE_C), F32),
        "f_conv": jnp.zeros((1, 1, CONV_F - 1, D_FF), F32),
    }

    s_seg = Seg(G=1, sh=dec_batch, T=dec_seq, reset_first=False, share_state=False)
    s_init = {
        "a_conv": jnp.swapaxes(state_a_conv, 1, 2).reshape(n_even, 1, (CONV_A - 1) * dec_batch, D_A),
        "a_h": state_a_h.reshape(n_even, 1, dec_batch, D_A),
        "b_S": state_b_S,
        "c_re": state_c_re.reshape(n_odd, 1, dec_batch, STATE_C),
        "c_im": state_c_im.reshape(n_odd, 1, dec_batch, STATE_C),
        "f_conv": jnp.swapaxes(state_ffn_conv, 1, 2).reshape(DEPTH, 1, (CONV_F - 1) * dec_batch, D_FF),
    }
    (ys, _), (s_st, meta_st) = _trunk(
        [jnp.swapaxes(x_sample, 0, 1).reshape(dec_seq * dec_batch, D_MODEL), meta_tokens.astype(F32)],
        [s_seg, meta_seg], [s_init, zero], w, n_even)

    p_seg = Seg(G=batch, sh=1, T=seq, reset_first=False, share_state=True)
    p_init = {k: jnp.stack(v) for k, v in meta_st.items()}
    (yp,), (p_st,) = _trunk([x_prompt.reshape(batch * seq, D_MODEL)], [p_seg], [p_init], w, n_even)

    y_prompt = yp.reshape(batch, seq, D_MODEL)
    y_sample = jnp.swapaxes(ys.reshape(dec_seq, dec_batch, D_MODEL), 0, 1)
    p_out = (jnp.stack(p_st["a_conv"]),
             jnp.stack(p_st["a_h"]).reshape(n_even, batch, D_A),
             jnp.stack(p_st["b_S"]),
             jnp.stack(p_st["c_re"]).reshape(n_odd, batch, G_C, P_C),
             jnp.stack(p_st["c_im"]).reshape(n_odd, batch, G_C, P_C),
             jnp.stack(p_st["f_conv"]))
    s_out = (jnp.swapaxes(jnp.stack(s_st["a_conv"]).reshape(n_even, CONV_A - 1, dec_batch, D_A), 1, 2),
             jnp.stack(s_st["a_h"]).reshape(n_even, dec_batch, D_A),
             s_st["b_S"],
             jnp.stack(s_st["c_re"]).reshape(n_odd, dec_batch, G_C, P_C),
             jnp.stack(s_st["c_im"]).reshape(n_odd, dec_batch, G_C, P_C),
             jnp.swapaxes(jnp.stack(s_st["f_conv"]).reshape(DEPTH, CONV_F - 1, dec_batch, D_FF), 1, 2))
    return (y_prompt, y_sample) + p_out + s_out
```

```python
import functools
from typing import NamedTuple

import jax
import jax.numpy as jnp
from jax import lax
from jax.experimental import pallas as pl
from jax.experimental.pallas import tpu as pltpu

F32 = jnp.float32
BF16 = jnp.bfloat16

D_MODEL = 2048
DEPTH = 4
N_META = 16
D_A = 1024
CONV_A = 4
C_RG = 8.0
D_B = 1024
H_B = 8
HEAD = 128
LB_FLOOR = 1e-30
D_C = 1024
GROUP_C = 16
G_C = 64
P_C = 64
STATE_C = G_C * P_C
D_FF = 5504
CONV_F = 3
EPS = 1e-6
GLA_SAFE_DECAY = 60.0
GLA_MIN_HALF = 32

LANE = 128
SUBLANE = 8
S5_LANES = LANE * P_C // GROUP_C
VMEM_LIMIT = 52 * 1024 * 1024


class Seg(NamedTuple):
    G: int
    sh: int
    T: int
    reset_first: bool
    share_state: bool


def _params(n_axes):
    return pltpu.CompilerParams(dimension_semantics=("arbitrary",) * n_axes, vmem_limit_bytes=VMEM_LIMIT)


def _dot(a, b):
    return jnp.dot(a, b, preferred_element_type=F32)


def _dot_nt(a, b):
    return lax.dot_general(a, b, (((1,), (1,)), ((), ())), preferred_element_type=F32)


def _dot_tn(a, b):
    return lax.dot_general(a, b, (((0,), (0,)), ((), ())), preferred_element_type=F32)


def _softplus(x):
    return jnp.maximum(x, 0.0) + jnp.log1p(jnp.exp(-jnp.abs(x)))


def _rms(x, g):
    ms = jnp.mean(x * x, axis=-1, keepdims=True)
    return x * lax.rsqrt(ms + EPS) * g


def _delay(x, prev, k, sh):
    rows = x.shape[0]
    n = k * sh
    p = prev.shape[0]
    if sh % SUBLANE == 0:
        return jnp.concatenate([prev[p - n:], x[:rows - n]], axis=0)
    assert sh == 1
    y = pltpu.roll(x, n, 0)
    row = lax.broadcasted_iota(jnp.int32, x.shape, 0)
    for i in range(n):
        y = jnp.where(row == i, prev[p - n + i:p - n + i + 1], y)
    return y


class _Value:
    def __init__(self, value):
        self.value = value

    def __getitem__(self, idx):
        return self.value


def _norm_kernel(x_ref, g_ref, o_ref):
    o_ref[...] = _rms(x_ref[...], g_ref[...]).astype(o_ref.dtype)


def _rmsnorm(x, g):
    rows = x.shape[0]
    tm = min(rows, 512)
    return pl.pallas_call(
        _norm_kernel,
        grid=(rows // tm,),
        in_specs=[pl.BlockSpec((tm, D_MODEL), lambda i: (i, 0)),
                  pl.BlockSpec((1, D_MODEL), lambda i: (0, 0))],
        out_specs=pl.BlockSpec((tm, D_MODEL), lambda i: (i, 0)),
        out_shape=jax.ShapeDtypeStruct((rows, D_MODEL), F32),
        compiler_params=_params(1),
        name="rmsnorm",
    )(x, g.reshape(1, D_MODEL))


def _mm_kernel(*refs, n_sets, n_in, has_res, glu, norm, emit):
    pos = n_sets * n_in
    ws = refs[pos:pos + n_in]
    pos += n_in
    if norm:
        g_ref = refs[pos]
        pos += 1
    res_refs = refs[pos:pos + n_sets] if has_res else (None,) * n_sets
    pos += n_sets if has_res else 0
    out_refs = refs[pos:pos + n_sets]
    pos += n_sets
    if emit:
        tiles = [w_ref[...].astype(BF16) for w_ref in ws]
        for wb_ref, tile in zip(refs[pos:pos + n_in], tiles):
            wb_ref[...] = tile
        ws = [_Value(t) for t in tiles]
        pos += n_in
    h_scs = refs[pos:]
    for s in range(n_sets):
        xs = list(refs[s * n_in:(s + 1) * n_in])
        if norm:
            @pl.when(pl.program_id(1) == 0)
            def _(x_ref=xs[0], h_sc=h_scs[s]):
                h_sc[...] = _rms(x_ref[...], g_ref[...]).astype(BF16)

            xs[0] = h_scs[s]
        if glu:
            acc = _dot(xs[0][...], ws[0][...]) * jax.nn.sigmoid(_dot(xs[1][...], ws[1][...]))
        else:
            acc = _dot(xs[0][...], ws[0][...])
            for x_ref, w_ref in zip(xs[1:], ws[1:]):
                acc = acc + _dot(x_ref[...], w_ref[...])
        if has_res:
            acc = res_refs[s][...] + acc
        out_refs[s][...] = acc.astype(out_refs[s].dtype)


def _matmul(xs_sets, ws, n_out, *, tm, tn, res_sets=None, glu=False, norm_g=None, emit=False, name="matmul"):
    n_sets, n_in = len(xs_sets), len(ws)
    assert not emit or n_sets > 1
    rows = [xs[0].shape[0] for xs in xs_sets]
    tms = [min(r, tm) for r in rows] if n_sets == 1 else rows
    in_specs, args, scratch = [], [], []
    for xs, t in zip(xs_sets, tms):
        in_specs += [pl.BlockSpec((t, x.shape[1]), lambda i, j: (i, 0)) for x in xs]
        args += list(xs)
    in_specs += [pl.BlockSpec((None, x.shape[1], tn), functools.partial(lambda i, j, l, kb: (l, kb, j), l=l, kb=kb))
                 for x, (_, l, kb) in zip(xs_sets[0], ws)]
    args += [w for w, _, _ in ws]
    if norm_g is not None:
        g, gl = norm_g
        in_specs.append(pl.BlockSpec((None, 1, D_MODEL), lambda i, j: (gl, 0, 0)))
        args.append(g)
        scratch = [pltpu.VMEM((t, D_MODEL), BF16) for t in tms]
    if res_sets is not None:
        in_specs += [pl.BlockSpec((t, tn), lambda i, j: (i, j)) for t in tms]
        args += list(res_sets)
    out_specs = [pl.BlockSpec((t, tn), lambda i, j: (i, j)) for t in tms]
    out_shape = [jax.ShapeDtypeStruct((r, n_out), F32) for r in rows]
    if emit:
        out_specs += [pl.BlockSpec((None, x.shape[1], tn), lambda i, j: (0, 0, j)) for x in xs_sets[0]]
        out_shape += [jax.ShapeDtypeStruct((1, x.shape[1], n_out), BF16) for x in xs_sets[0]]
    outs = pl.pallas_call(
        functools.partial(_mm_kernel, n_sets=n_sets, n_in=n_in, has_res=res_sets is not None, glu=glu,
                          norm=norm_g is not None, emit=emit),
        grid=(rows[0] // tms[0], n_out // tn),
        in_specs=in_specs,
        out_specs=out_specs,
        out_shape=out_shape,
        scratch_shapes=scratch,
        compiler_params=_params(2),
        name=name,
    )(*args)
    return (outs[:n_sets], outs[n_sets:]) if emit else outs


def _ffn_up_kernel(*refs, shapes, emit):
    n_in = sum(3 if nblk > 1 else 2 for _, _, nblk in shapes)
    g_ref, wu_ref, wg_ref, cw_ref, cb_ref = refs[n_in:n_in + 5]
    n_out = 2 * len(shapes) + (2 if emit else 0)
    ins, outs, scs = list(refs[:n_in]), list(refs[n_in + 5:n_in + 5 + n_out]), list(refs[n_in + 5 + n_out:])
    if emit:
        wu_ref, wg_ref = _Value(wu_ref[...].astype(BF16)), _Value(wg_ref[...].astype(BF16))
        outs[-2][...] = wu_ref[...]
        outs[-1][...] = wg_ref[...]
    for sh, rows, nblk in shapes:
        x_ref = ins.pop(0)
        xp_ref = ins.pop(0) if nblk > 1 else None
        fb_ref = ins.pop(0)
        act_ref, tail_ref = outs.pop(0), outs.pop(0)
        h_sc = scs.pop(0)
        hp_sc = scs.pop(0) if nblk > 1 else None

        @pl.when(pl.program_id(1) == 0)
        def _(x_ref=x_ref, xp_ref=xp_ref, h_sc=h_sc, hp_sc=hp_sc):
            h_sc[...] = _rms(x_ref[...], g_ref[...]).astype(BF16)
            if xp_ref is not None:
                hp_sc[...] = _rms(xp_ref[...], g_ref[...]).astype(BF16)

        h = h_sc[...]
        up = _dot(h, wu_ref[...])
        gate = _dot(h, wg_ref[...])
        prev = fb_ref[...]
        if nblk > 1:
            up_prev = _dot(hp_sc[...], wu_ref[...])[SUBLANE - 2:]
            prev = jnp.where(pl.program_id(0) % nblk == 0, prev, up_prev)
        upc = (cb_ref[...] + cw_ref[0:1, :] * _delay(up, prev, 2, sh)
               + cw_ref[1:2, :] * _delay(up, prev, 1, sh) + cw_ref[2:3, :] * up)
        act_ref[...] = (jax.nn.gelu(upc) * gate).astype(act_ref.dtype)
        tail_ref[0] = up[rows - 2 * sh:]


def _ffn_up(xs, g, fbufs, w_up, w_gate, cw, cb, l, segs, *, tf, emit=False):
    (w_up, lu), (w_gate, lg) = w_up, w_gate
    assert not emit or len(xs) > 1
    shapes, in_specs, args, scratch, out_specs, out_shape = [], [], [], [], [], []
    for x, (fb, fl), seg in zip(xs, fbufs, segs):
        G, sh, T = seg.G, seg.sh, seg.T
        tt = min(T, 1024 // sh)
        rows, nblk = tt * sh, T // tt
        assert len(xs) == 1 or G * nblk == 1
        shapes.append((sh, rows, nblk))
        gi = (lambda i: 0) if seg.share_state else functools.partial(lambda i, nblk: i // nblk, nblk=nblk)
        in_specs.append(pl.BlockSpec((rows, D_MODEL), lambda i, j: (i, 0)))
        args.append(x)
        scratch.append(pltpu.VMEM((rows, D_MODEL), BF16))
        if nblk > 1:
            in_specs.append(pl.BlockSpec((SUBLANE, D_MODEL), functools.partial(
                lambda i, j, per: (jnp.maximum(i * per - 1, 0), 0), per=rows // SUBLANE)))
            args.append(x)
            scratch.append(pltpu.VMEM((SUBLANE, D_MODEL), BF16))
        in_specs.append(pl.BlockSpec((None, None, 2 * sh, tf),
                                     functools.partial(lambda i, j, fl, gi: (fl, gi(i), 0, j), fl=fl, gi=gi)))
        args.append(fb)
        out_specs += [pl.BlockSpec((rows, tf), lambda i, j: (i, j)),
                      pl.BlockSpec((1, 2 * sh, tf), lambda i, j: (i, 0, j))]
        out_shape += [jax.ShapeDtypeStruct((G * T * sh, D_FF), BF16),
                      jax.ShapeDtypeStruct((G * nblk, 2 * sh, D_FF), F32)]
    in_specs += [pl.BlockSpec((None, 1, D_MODEL), lambda i, j: (l, 0, 0)),
                 pl.BlockSpec((None, D_MODEL, tf), lambda i, j: (lu, 0, j)),
                 pl.BlockSpec((None, D_MODEL, tf), lambda i, j: (lg, 0, j)),
                 pl.BlockSpec((None, CONV_F, tf), lambda i, j: (l, 0, j)),
                 pl.BlockSpec((None, 1, tf), lambda i, j: (l, 0, j))]
    args += [g, w_up, w_gate, cw, cb]
    if emit:
        out_specs += [pl.BlockSpec((None, D_MODEL, tf), lambda i, j: (0, 0, j))] * 2
        out_shape += [jax.ShapeDtypeStruct((1, D_MODEL, D_FF), BF16)] * 2
    n_row_blocks = segs[0].G * shapes[0][2]
    outs = pl.pallas_call(
        functools.partial(_ffn_up_kernel, shapes=tuple(shapes), emit=emit),
        grid=(n_row_blocks, pl.cdiv(D_FF, tf)),
        in_specs=in_specs,
        out_specs=out_specs,
        out_shape=out_shape,
        scratch_shapes=scratch,
        compiler_params=_params(2),
        name="ffn_up",
    )(*args)
    per_set = [(outs[2 * s], outs[2 * s + 1][nblk - 1::nblk]) for s, (_, _, nblk) in enumerate(shapes)]
    return (per_set, outs[2 * len(shapes):]) if emit else per_set


def _rglru_kernel(xa_ref, ga_ref, cst_ref, h0_ref, cw_ref, cb_ref, wr_ref, br_ref, wi_ref, bi_ref, lam_ref,
                  oa_ref, cso_ref, ho_ref, prev_sc, h_sc, a_sc, b_sc, *, sh, tt, heads, reset_first):
    tb = pl.program_id(2)
    rows = tt * sh

    @pl.when(tb == 0)
    def _():
        prev_sc[...] = cst_ref[...]
        h_sc[...] = h0_ref[...]

    xa = xa_ref[...]
    prev = prev_sc[...]
    xc = cb_ref[...] + cw_ref[CONV_A - 1:CONV_A, :] * xa
    for k in range(CONV_A - 1):
        xc = xc + cw_ref[k:k + 1, :] * _delay(xa, prev, CONV_A - 1 - k, sh)
    new_prev = xa[rows - (CONV_A - 1) * sh:]
    prev_sc[...] = new_prev
    cso_ref[0] = new_prev

    row = lax.broadcasted_iota(jnp.int32, (rows, HEAD), 0)
    for hd in range(heads):
        cs = slice(hd * HEAD, (hd + 1) * HEAD)
        xh = xc[:, cs]
        xb = xh.astype(BF16)
        r = jax.nn.sigmoid(_dot(xb, wr_ref[hd]) + br_ref[:, cs])
        ig = jax.nn.sigmoid(_dot(xb, wi_ref[hd]) + bi_ref[:, cs])
        a = jnp.exp(-C_RG * r * _softplus(-lam_ref[:, cs]))
        mult = jnp.sqrt(1.0 - a * a)
        if reset_first:
            mult = jnp.where((row < sh) & (tb == 0), 1.0, mult)
        a_sc[:, cs] = a
        b_sc[:, cs] = mult * ig * xh

    if sh == 1:
        width = a_sc.shape[1]
        a3 = a_sc[...].reshape(rows // SUBLANE, SUBLANE, width)
        b3 = b_sc[...].reshape(rows // SUBLANE, SUBLANE, width)
        sub = lax.broadcasted_iota(jnp.int32, a3.shape, 1)
        d = 1
        while d < SUBLANE:
            keep = sub >= d
            b3 = b3 + a3 * jnp.where(keep, pltpu.roll(b3, d, 1), 0.0)
            a3 = a3 * jnp.where(keep, pltpu.roll(a3, d, 1), 1.0)
            d *= 2
        a_sc[...] = a3.reshape(rows, width)
        b_sc[...] = b3.reshape(rows, width)

        def tile_step(k, h):
            sl = pl.ds(pl.multiple_of(k * SUBLANE, SUBLANE), SUBLANE)
            ht = b_sc[sl, :] + a_sc[sl, :] * jnp.broadcast_to(h, (SUBLANE, width))
            b_sc[sl, :] = ht
            return ht[SUBLANE - 1:]

        h = lax.fori_loop(0, rows // SUBLANE, tile_step, h_sc[...], unroll=min(4, rows // SUBLANE))
    else:
        def step(t, h):
            sl = pl.ds(pl.multiple_of(t * sh, sh), sh)
            h = a_sc[sl, :] * h + b_sc[sl, :]
            b_sc[sl, :] = h
            return h

        h = lax.fori_loop(0, tt, step, h_sc[...], unroll=True)
    h_sc[...] = h
    ho_ref[0] = h
    oa_ref[...] = (b_sc[...] * jax.nn.gelu(ga_ref[...])).astype(oa_ref.dtype)


def _rglru(z, a_conv, a_h, prm, l, seg):
    G, sh, T = seg.G, seg.sh, seg.T
    tt = min(T, 512) if sh == 1 else T
    rows, nblk = tt * sh, T // tt
    heads = 8 if sh == 1 else 2
    cwid = heads * HEAD
    ncb = D_A // cwid
    (ac, acl), (ah, ahl) = a_conv, a_h
    gi = (lambda g: 0) if seg.share_state else (lambda g: g)
    row_blk = lambda c, g, tb: g * nblk + tb
    vec = lambda n: pl.BlockSpec((None, n, cwid), lambda c, g, tb: (l, 0, c))
    mat = pl.BlockSpec((None, heads, HEAD, HEAD), lambda c, g, tb: (l, c, 0, 0))
    return pl.pallas_call(
        functools.partial(_rglru_kernel, sh=sh, tt=tt, heads=heads, reset_first=seg.reset_first),
        grid=(ncb, G, nblk),
        in_specs=[pl.BlockSpec((rows, cwid), lambda c, g, tb: (row_blk(c, g, tb), c)),
                  pl.BlockSpec((rows, cwid), lambda c, g, tb: (row_blk(c, g, tb), ncb + c)),
                  pl.BlockSpec((None, None, (CONV_A - 1) * sh, cwid), lambda c, g, tb: (acl, gi(g), 0, c)),
                  pl.BlockSpec((None, None, sh, cwid), lambda c, g, tb: (ahl, gi(g), 0, c)),
                  vec(CONV_A), vec(1), mat, vec(1), mat, vec(1), vec(1)],
        out_specs=[pl.BlockSpec((rows, cwid), lambda c, g, tb: (row_blk(c, g, tb), c)),
                   pl.BlockSpec((1, (CONV_A - 1) * sh, cwid), lambda c, g, tb: (g, 0, c)),
                   pl.BlockSpec((1, sh, cwid), lambda c, g, tb: (g, 0, c))],
        out_shape=[jax.ShapeDtypeStruct((G * T * sh, D_A), BF16),
                   jax.ShapeDtypeStruct((G, (CONV_A - 1) * sh, D_A), F32),
                   jax.ShapeDtypeStruct((G, sh, D_A), F32)],
        scratch_shapes=[pltpu.VMEM(((CONV_A - 1) * sh, cwid), F32),
                        pltpu.VMEM((sh, cwid), F32),
                        pltpu.VMEM((rows, cwid), F32),
                        pltpu.VMEM((rows, cwid), F32)],
        compiler_params=_params(3),
        name="rglru",
    )(z, z, ac, ah, prm["conv_a_w"], prm["conv_a_b"], prm["w_ra"], prm["b_ra"], prm["w_ia"], prm["b_ia"],
      prm["lam_a"])


def _cumsum_rows(x):
    n = x.shape[0]
    row = lax.broadcasted_iota(jnp.int32, x.shape, 0)
    d = 1
    while d < n:
        x = x + jnp.where(row >= d, pltpu.roll(x, d, 0), 0.0)
        d *= 2
    return x


def _gla_pairwise(q, bc, k, v):
    n = q.shape[0]
    row = lax.broadcasted_iota(jnp.int32, (n, 1), 0)
    o = jnp.zeros((n, HEAD), F32)
    for s in range(n):
        diff = jnp.where(row >= s, bc - bc[s:s + 1], 0.0)
        w = q * k[s:s + 1] * jnp.exp(diff)
        a = jnp.where(row >= s, jnp.sum(w, axis=-1, keepdims=True), 0.0)
        o = o + a * v[s:s + 1]
    return o


def _gla_chunk_fast(q, bc, k, v, st):
    c = q.shape[0]
    h = c // 2
    b_mid = bc[h - 1:h]
    bl = jnp.concatenate([bc[:h], bc[h:] - b_mid], axis=0)
    qt = q * jnp.exp(bl)
    qtb = qt.astype(BF16)
    kt = (k * jnp.exp(-bl)).astype(BF16)
    vb = v.astype(BF16)
    row0 = lax.broadcasted_iota(jnp.int32, (h, h), 0)
    col0 = lax.broadcasted_iota(jnp.int32, (h, h), 1)
    a0 = jnp.where(row0 >= col0, _dot_nt(qtb[:h], kt[:h]), 0.0)
    o0 = _dot(a0.astype(BF16), vb[:h])
    kh = (k[:h] * jnp.exp(b_mid - bc[:h])).astype(BF16)
    row1 = lax.broadcasted_iota(jnp.int32, (h, c), 0) + h
    col1 = lax.broadcasted_iota(jnp.int32, (h, c), 1)
    a1 = jnp.where(row1 >= col1, _dot_nt(qtb[h:], jnp.concatenate([kh, kt[h:]], axis=0)), 0.0)
    o1 = _dot(a1.astype(BF16), vb)
    qs = jnp.concatenate([qt[:h], qt[h:] * jnp.exp(b_mid)], axis=0)
    o = jnp.concatenate([o0, o1], axis=0) + _dot_nt(qs.astype(BF16), st.astype(BF16))
    b_end = bc[c - 1:c]
    kd = k * jnp.exp(b_end - bc)
    st_new = st * jnp.exp(b_end) + _dot_tn(vb, kd.astype(BF16))
    return o, st_new


def _gla_chunk(q, bc, k, v, st, base):
    c = q.shape[0]
    o_state = _dot_nt((q * jnp.exp(bc)).astype(BF16), st.astype(BF16))
    parts = [_gla_pairwise(q[i:i + base], bc[i:i + base], k[i:i + base], v[i:i + base])
             for i in range(0, c, base)]
    s = base
    while s < c:
        for p in range(0, c, 2 * s):
            bm = bc[p + s - 1:p + s]
            qh = q[p + s:p + 2 * s] * jnp.exp(bc[p + s:p + 2 * s] - bm)
            kh = k[p:p + s] * jnp.exp(bm - bc[p:p + s])
            att = _dot_nt(qh.astype(BF16), kh.astype(BF16))
            upd = _dot(att.astype(BF16), v[p:p + s].astype(BF16))
            for i in range(s // base):
                idx = (p + s) // base + i
                parts[idx] = parts[idx] + upd[i * base:(i + 1) * base]
        s *= 2
    o = o_state + (parts[0] if len(parts) == 1 else jnp.concatenate(parts, axis=0))
    b_end = bc[c - 1:c]
    kd = k * jnp.exp(b_end - bc)
    st_new = st * jnp.exp(b_end) + _dot_tn(v.astype(BF16), kd.astype(BF16))
    return o, st_new


def _gla_block_kv(q, bc, k, v, s):
    n = q.shape[0]
    o = _dot((q * jnp.exp(bc)).astype(BF16), s.astype(BF16)) + _gla_pairwise(q, bc, k, v)
    b_end = bc[n - 1:n]
    kd = k * jnp.exp(b_end - bc)
    decay_col = jnp.broadcast_to(jnp.exp(b_end), (SUBLANE, HEAD)).T[:, 0:1]
    s_new = s * decay_col + _dot_tn(kd.astype(BF16), v.astype(BF16))
    return o, s_new


def _hgrn2_gates(fz, lb):
    la = jnp.log(jnp.maximum(lb, LB_FLOOR))
    lbv = jnp.log1p(-lb) + jnp.minimum(fz, 0.0) - jnp.log1p(jnp.exp(-jnp.abs(fz)))
    logf = jnp.maximum(la, lbv) + jnp.log1p(jnp.exp(-jnp.abs(la - lbv)))
    k = (1.0 - lb) * jax.nn.sigmoid(-fz)
    return logf, k


def _hgrn2_out(o, gn, gb):
    o = o * lax.rsqrt(jnp.mean(o * o, axis=-1, keepdims=True) + EPS)
    return o * gn * jax.nn.silu(gb)


def _hgrn2_long_kernel(q_ref, fz_ref, v_ref, gb_ref, s0_ref, lb_ref, gn_ref, ob_ref, so_ref, st_sc,
                       *, tt, chunk, base, nblk, hb):
    tb = pl.program_id(2)

    @pl.when(tb == 0)
    def _():
        for hd in range(hb):
            st_sc[hd] = s0_ref[hd].T

    def body(ci, carry):
        sl = pl.ds(pl.multiple_of(ci * chunk, chunk), chunk)
        heads = []
        for hd in range(hb):
            cs = slice(hd * HEAD, (hd + 1) * HEAD)
            logf, k = _hgrn2_gates(fz_ref[sl, cs], lb_ref[:, cs])
            heads.append((hd, cs, _cumsum_rows(logf), k))

        def run(chunk_fn):
            for hd, cs, bc, k in heads:
                o, st_new = chunk_fn(q_ref[sl, cs], bc, k, v_ref[sl, cs], st_sc[hd])
                st_sc[hd] = st_new
                ob_ref[sl, cs] = _hgrn2_out(o, gn_ref[:, cs], gb_ref[sl, cs]).astype(ob_ref.dtype)

        exact = functools.partial(_gla_chunk, base=base)
        if chunk < 2 * GLA_MIN_HALF:
            run(exact)
        else:
            half = chunk // 2
            decay = None
            for _, _, bc, _ in heads:
                d = jnp.maximum(-bc[half - 1:half], bc[half - 1:half] - bc[chunk - 1:chunk])
                decay = d if decay is None else jnp.maximum(decay, d)
            safe = jnp.max(decay) <= GLA_SAFE_DECAY
            pl.when(safe)(lambda: run(_gla_chunk_fast))
            pl.when(jnp.logical_not(safe))(lambda: run(exact))
        return carry

    lax.fori_loop(0, tt // chunk, body, 0)

    @pl.when(tb == nblk - 1)
    def _():
        for hd in range(hb):
            so_ref[0, hd] = st_sc[hd].T


def _hgrn2_wide_kernel(*refs, sh, T, nsb, unroll, first, n_slabs):
    if first:
        q_ref, fz_ref, v_ref, gb_ref, s0_ref, lb_ref, gn_ref, ob_ref, so_ref, o_sc = refs
        if n_slabs > 1:
            so_ref[1:] = jnp.zeros((n_slabs - 1, nsb, 1, HEAD, HEAD), F32)
        so = so_ref.at[0]
    else:
        q_ref, fz_ref, v_ref, gb_ref, s0_ref, lb_ref, gn_ref, _, ob_ref, so, o_sc = refs
    sb = pl.program_id(1)
    lb = lb_ref[...]

    def body(it, carry):
        for u in range(unroll):
            jj = it * unroll + u
            sl = pl.ds(sb * nsb + jj, T, stride=sh)
            logf, k = _hgrn2_gates(fz_ref[sl, :], lb)
            o, s_new = _gla_block_kv(q_ref[sl, :], _cumsum_rows(logf), k, v_ref[sl, :], s0_ref[jj, 0])
            so[jj, 0] = s_new
            o_sc[sl, :] = o
        return carry

    lax.fori_loop(0, nsb // unroll, body, 0)

    @pl.when(sb == sh // nsb - 1)
    def _():
        ob_ref[...] = _hgrn2_out(o_sc[...], gn_ref[...], gb_ref[...]).astype(ob_ref.dtype)


def _hgrn2(z, s0, prm, l, seg, n_slabs=1, slab=0, s_buf=None):
    G, sh, T = seg.G, seg.sh, seg.T
    nseq = G * sh
    (sa, sl_) = s0
    if sh == 1:
        hb = 8
        wid = hb * HEAD
        q0, f0, v0, g0 = (2 * D_A // wid, (2 * D_A + D_B) // wid, (2 * D_A + 2 * D_B) // wid,
                          (2 * D_A + 3 * D_B) // wid)
        tt = min(T, 256)
        chunk = min(tt, 64)
        nblk = T // tt
        gi = (lambda g: 0) if seg.share_state else (lambda g: g)
        col = lambda c0: pl.BlockSpec((tt, wid), lambda h, g, tb: (g * nblk + tb, c0 + h))
        vec = pl.BlockSpec((None, 1, wid), lambda h, g, tb: (l, 0, h))
        return pl.pallas_call(
            functools.partial(_hgrn2_long_kernel, tt=tt, chunk=chunk, base=min(chunk, 16), nblk=nblk, hb=hb),
            grid=(H_B // hb, G, nblk),
            in_specs=[col(q0), col(f0), col(v0), col(g0),
                      pl.BlockSpec((None, None, hb, HEAD, HEAD), lambda h, g, tb: (sl_, gi(g), h, 0, 0)),
                      vec, vec],
            out_specs=[pl.BlockSpec((tt, wid), lambda h, g, tb: (g * nblk + tb, h)),
                       pl.BlockSpec((1, hb, HEAD, HEAD), lambda h, g, tb: (g, h, 0, 0))],
            out_shape=[jax.ShapeDtypeStruct((G * T * sh, D_B), BF16),
                       jax.ShapeDtypeStruct((nseq, H_B, HEAD, HEAD), F32)],
            scratch_shapes=[pltpu.VMEM((hb, HEAD, HEAD), F32)],
            compiler_params=_params(3),
            name="hgrn2_long",
        )(z, z, z, z, sa, prm["lb_all"], prm["gn_b"])
    assert G == 1 and T == SUBLANE
    q0, f0, v0, g0 = 2 * D_A // HEAD, (2 * D_A + D_B) // HEAD, (2 * D_A + 2 * D_B) // HEAD, (2 * D_A + 3 * D_B) // HEAD
    rows = T * sh
    nsb = min(sh, 32)
    unroll = 8 if nsb % 8 == 0 else 1
    col = lambda c0: pl.BlockSpec((rows, HEAD), lambda h, sb: (0, c0 + h))
    vec = pl.BlockSpec((None, 1, HEAD), lambda h, sb: (l, 0, h))
    in_specs = [col(q0), col(f0), col(v0), col(g0),
                pl.BlockSpec((None, nsb, 1, HEAD, HEAD), lambda h, sb: (sl_, sb, h, 0, 0)), vec, vec]
    args = [z, z, z, z, sa, prm["lb_all"], prm["gn_b"]]
    first = s_buf is None
    if first:
        so_spec = pl.BlockSpec((n_slabs, nsb, 1, HEAD, HEAD), lambda h, sb: (0, sb, h, 0, 0))
        aliases = {}
    else:
        in_specs.append(pl.BlockSpec(memory_space=pl.ANY))
        args.append(s_buf)
        so_spec = pl.BlockSpec((None, nsb, 1, HEAD, HEAD), lambda h, sb: (slab, sb, h, 0, 0))
        aliases = {len(args) - 1: 1}
    return pl.pallas_call(
        functools.partial(_hgrn2_wide_kernel, sh=sh, T=T, nsb=nsb, unroll=unroll, first=first, n_slabs=n_slabs),
        grid=(H_B, sh // nsb),
        in_specs=in_specs,
        out_specs=[pl.BlockSpec((rows, HEAD), lambda h, sb: (0, h)), so_spec],
        out_shape=[jax.ShapeDtypeStruct((G * T * sh, D_B), BF16),
                   jax.ShapeDtypeStruct((n_slabs, nseq, H_B, HEAD, HEAD), F32)],
        scratch_shapes=[pltpu.VMEM((rows, HEAD), F32)],
        input_output_aliases=aliases,
        compiler_params=_params(2),
        name="hgrn2_wide",
    )(*args)


def _cmul(ar, ai, br, bi):
    return ar * br - ai * bi, ar * bi + ai * br


def _s5_packed_scan(xr_sc, xi_sc, pwr_sc, pwi_sc, sr0, si0, ar, ai, seg_len):
    lanes = xr_sc.shape[1]
    row = lax.broadcasted_iota(jnp.int32, (SUBLANE, lanes), 0)
    art, ait = jnp.broadcast_to(ar, (SUBLANE, lanes)), jnp.broadcast_to(ai, (SUBLANE, lanes))

    def local(t, carry):
        sr, si = carry
        sl = pl.ds(pl.multiple_of(t * SUBLANE, SUBLANE), SUBLANE)
        pr, pi = _cmul(art, ait, sr, si)
        nr, ni = pr + xr_sc[sl, :], pi + xi_sc[sl, :]
        xr_sc[sl, :] = nr
        xi_sc[sl, :] = ni
        return nr, ni

    init = (jnp.where(row == 0, jnp.broadcast_to(sr0, (SUBLANE, lanes)), 0.0),
            jnp.where(row == 0, jnp.broadcast_to(si0, (SUBLANE, lanes)), 0.0))
    fr, fi = lax.fori_loop(0, seg_len, local, init, unroll=True)

    last = seg_len * SUBLANE - 1
    alr, ali = pwr_sc[last:last + 1, :], pwi_sc[last:last + 1, :]
    cr, ci = fr[0:1], fi[0:1]
    car_r, car_i = jnp.zeros((SUBLANE, lanes), F32), jnp.zeros((SUBLANE, lanes), F32)
    for i in range(1, SUBLANE):
        car_r = jnp.where(row == i, jnp.broadcast_to(cr, (SUBLANE, lanes)), car_r)
        car_i = jnp.where(row == i, jnp.broadcast_to(ci, (SUBLANE, lanes)), car_i)
        pr, pi = _cmul(alr, ali, cr, ci)
        cr, ci = fr[i:i + 1] + pr, fi[i:i + 1] + pi

    def fix(t, carry):
        sl = pl.ds(pl.multiple_of(t * SUBLANE, SUBLANE), SUBLANE)
        dr, di = _cmul(pwr_sc[sl, :], pwi_sc[sl, :], car_r, car_i)
        xr_sc[sl, :] = xr_sc[sl, :] + dr
        xi_sc[sl, :] = xi_sc[sl, :] + di
        return carry

    lax.fori_loop(0, seg_len, fix, 0, unroll=True)
    return cr, ci


def _s5_kernel(*refs, sh, tt):
    if sh == 1:
        (u_ref, sr0_ref, si0_ref, ar_ref, ai_ref, bre_ref, bim_ref, cre_ref, cim_ref, d_ref,
         y_ref, sro_ref, sio_ref, sr_sc, si_sc, xr_sc, xi_sc, up_sc, yp_sc, yt_sc, pwr_sc, pwi_sc) = refs
    else:
        (u_ref, sr0_ref, si0_ref, ar_ref, ai_ref, bre_ref, bim_ref, cre_ref, cim_ref, d_ref,
         y_ref, sro_ref, sio_ref, sr_sc, si_sc, xr_sc, xi_sc) = refs
    tb = pl.program_id(2)
    rows = tt * sh

    @pl.when(tb == 0)
    def _():
        sr_sc[...] = sr0_ref[...]
        si_sc[...] = si0_ref[...]

    if sh == 1:
        seg_len = rows // SUBLANE

        @pl.when((pl.program_id(1) == 0) & (tb == 0))
        def _():
            def grow(t, pw):
                sl = pl.ds(pl.multiple_of(t * SUBLANE, SUBLANE), SUBLANE)
                pwr_sc[sl, :] = jnp.broadcast_to(pw[0], (SUBLANE, S5_LANES))
                pwi_sc[sl, :] = jnp.broadcast_to(pw[1], (SUBLANE, S5_LANES))
                return _cmul(pw[0], pw[1], ar_ref[...], ai_ref[...])

            lax.fori_loop(0, seg_len, grow, (ar_ref[...], ai_ref[...]))

        def pack(t, carry):
            up_sc[pl.ds(pl.multiple_of(t * SUBLANE, SUBLANE), SUBLANE), :] = u_ref[pl.ds(t, SUBLANE, stride=seg_len), :]
            return carry

        lax.fori_loop(0, seg_len, pack, 0, unroll=True)
        u_skip = up_sc[...]
    else:
        u_skip = u_ref[...]
    ub = u_skip.astype(BF16)
    xr_sc[...] = _dot(ub, bre_ref[...])
    xi_sc[...] = _dot(ub, bim_ref[...])

    if sh == 1:
        sr, si = _s5_packed_scan(xr_sc, xi_sc, pwr_sc, pwi_sc, sr_sc[...], si_sc[...], ar_ref[...], ai_ref[...],
                                 seg_len)
    else:
        ar = jnp.broadcast_to(ar_ref[...], (sh, S5_LANES))
        ai = jnp.broadcast_to(ai_ref[...], (sh, S5_LANES))

        def step(t, carry):
            sl = pl.ds(pl.multiple_of(t * sh, sh), sh)
            pr, pi = _cmul(ar, ai, carry[0], carry[1])
            nr, ni = pr + xr_sc[sl, :], pi + xi_sc[sl, :]
            xr_sc[sl, :] = nr
            xi_sc[sl, :] = ni
            return nr, ni

        sr, si = lax.fori_loop(0, tt, step, (sr_sc[...], si_sc[...]), unroll=True)
    sr_sc[...] = sr
    si_sc[...] = si
    sro_ref[0] = sr
    sio_ref[0] = si
    y = (_dot(xr_sc[...].astype(BF16), cre_ref[...]) - _dot(xi_sc[...].astype(BF16), cim_ref[...])
         + d_ref[...] * u_skip)
    out = jax.nn.gelu(y)
    if sh == 1:
        yp_sc[...] = out

        def unpack(t, carry):
            yt_sc[pl.ds(t, SUBLANE, stride=seg_len), :] = yp_sc[pl.ds(pl.multiple_of(t * SUBLANE, SUBLANE), SUBLANE), :]
            return carry

        lax.fori_loop(0, seg_len, unpack, 0, unroll=True)
        out = yt_sc[...]
    y_ref[...] = out.astype(BF16)


def _s5(u, s_re, s_im, prm, j, seg):
    G, sh, T = seg.G, seg.sh, seg.T
    tt = min(T, 512) if sh == 1 else T
    rows, nblk = tt * sh, T // tt
    ncb = D_C // LANE
    (sr, srl), (si, sil) = s_re, s_im
    gi = (lambda g: 0) if seg.share_state else (lambda g: g)
    st_in = lambda lay: pl.BlockSpec((None, None, sh, S5_LANES), lambda c, g, tb: (lay, gi(g), 0, c))
    st_out = pl.BlockSpec((1, sh, S5_LANES), lambda c, g, tb: (g, 0, c))
    vec = pl.BlockSpec((None, 1, S5_LANES), lambda c, g, tb: (j, 0, c))
    bmat = pl.BlockSpec((None, None, LANE, S5_LANES), lambda c, g, tb: (j, c, 0, 0))
    cmat = pl.BlockSpec((None, None, S5_LANES, LANE), lambda c, g, tb: (j, c, 0, 0))
    scratch = [pltpu.VMEM((sh, S5_LANES), F32), pltpu.VMEM((sh, S5_LANES), F32),
               pltpu.VMEM((rows, S5_LANES), F32), pltpu.VMEM((rows, S5_LANES), F32)]
    if sh == 1:
        scratch += [pltpu.VMEM((rows, LANE), F32)] * 3
        scratch += [pltpu.VMEM((rows, S5_LANES), F32)] * 2
    return pl.pallas_call(
        functools.partial(_s5_kernel, sh=sh, tt=tt),
        grid=(ncb, G, nblk),
        in_specs=[pl.BlockSpec((rows, LANE), lambda c, g, tb: (g * nblk + tb, c)),
                  st_in(srl), st_in(sil), vec, vec, bmat, bmat, cmat, cmat,
                  pl.BlockSpec((None, 1, LANE), lambda c, g, tb: (j, 0, c))],
        out_specs=[pl.BlockSpec((rows, LANE), lambda c, g, tb: (g * nblk + tb, c)), st_out, st_out],
        out_shape=[jax.ShapeDtypeStruct((G * T * sh, D_C), BF16),
                   jax.ShapeDtypeStruct((G, sh, STATE_C), F32),
                   jax.ShapeDtypeStruct((G, sh, STATE_C), F32)],
        scratch_shapes=scratch,
        compiler_params=_params(3),
        name="s5",
    )(u, sr, si, prm["ar"], prm["ai"], prm["bre"], prm["bim"], prm["cre"], prm["cim"], prm["d"])


def _s5_params(lam_re, lam_im, log_dt, bmat_re, bmat_im, cmat_re, cmat_im, d_skip):
    n = lam_re.shape[0]
    lr, li = lam_re.astype(F32), lam_im.astype(F32)
    dt = jnp.exp(log_dt.astype(F32))[..., None]
    mag = jnp.exp(lr * dt)
    ar = mag * jnp.cos(li * dt)
    ai = mag * jnp.sin(li * dt)
    den = lr * lr + li * li
    zr = ((ar - 1.0) * lr + ai * li) / den
    zi = (ai * lr - (ar - 1.0) * li) / den
    br_, bi_ = bmat_re.astype(F32), bmat_im.astype(F32)
    bb_re = zr[..., None] * br_ - zi[..., None] * bi_
    bb_im = zr[..., None] * bi_ + zi[..., None] * br_
    gpb = LANE // GROUP_C
    nb = G_C // gpb
    eye = jnp.eye(gpb, dtype=F32)

    def pack_b(bb):
        bb = bb.reshape(n, nb, gpb, P_C, GROUP_C)
        return jnp.einsum("ag,nbapc->nbacgp", eye, bb).reshape(n, nb, LANE, S5_LANES).astype(BF16)

    def pack_c(cm):
        cm = cm.astype(F32).reshape(n, nb, gpb, GROUP_C, P_C)
        return jnp.einsum("ag,nbacp->nbapgc", eye, cm).reshape(n, nb, S5_LANES, LANE).astype(BF16)

    return {"ar": ar.reshape(n, 1, STATE_C), "ai": ai.reshape(n, 1, STATE_C),
            "bre": pack_b(bb_re), "bim": pack_b(bb_im), "cre": pack_c(cmat_re), "cim": pack_c(cmat_im),
            "d": d_skip.astype(F32).reshape(n, 1, D_C)}


def _trunk(xs, segs, sts, w, n_even, wb=None):
    n = len(xs)
    emit = wb is None
    if emit:
        wb = {k: {} for k in ("w_in_e", "w_out_e", "w_in_c", "w_glu_v", "w_glu_g", "w_up", "w_gate", "w_down")}
    news = [{k: [] for k in ("a_conv", "a_h", "b_S", "c_re", "c_im", "f_conv")} for _ in range(n)]
    s_bufs = [None] * n
    lay = lambda s, name, idx: (sts[s][name], min(idx, sts[s][name].shape[0] - 1))

    def wsel(name, layer, kblocks=(0,)):
        if emit:
            return [(w[name], layer, kb) for kb in kblocks]
        return [(part, 0, 0) for part in wb[name][layer]]

    def mm(wname, layer, xs_sets, n_out, tn_one, tn_many, kblocks=(0,), **kw):
        outs = _matmul(xs_sets, wsel(wname, layer, kblocks), n_out, tn=tn_many if emit else tn_one, emit=emit, **kw)
        if emit:
            outs, wb[wname][layer] = outs
        return outs

    for l in range(DEPTH):
        if l % 2 == 0:
            i = l // 2
            zs = mm("w_in_e", i, [[x] for x in xs], 6 * D_A, 1024, 512, tm=1024, norm_g=(w["g_mix"], l),
                    name="in_proj_even")
            mixed = []
            for s, (z, seg) in enumerate(zip(zs, segs)):
                out_a, nb, hl = _rglru(z, lay(s, "a_conv", i), lay(s, "a_h", i), w, i, seg)
                if seg.sh == 1:
                    out_b, s_new = _hgrn2(z, lay(s, "b_S", i), w, i, seg)
                    news[s]["b_S"].append(s_new)
                else:
                    out_b, s_bufs[s] = _hgrn2(z, lay(s, "b_S", i), w, i, seg, n_slabs=n_even, slab=i,
                                              s_buf=s_bufs[s])
                mixed.append([out_a, out_b])
                news[s]["a_conv"].append(nb)
                news[s]["a_h"].append(hl)
            xs = mm("w_out_e", i, mixed, D_MODEL, D_MODEL, 512, kblocks=(0, 1), tm=512, res_sets=xs,
                    name="out_proj_even")
        else:
            j = l // 2
            us = mm("w_in_c", j, [[x] for x in xs], D_C, D_C, 512, tm=1024, norm_g=(w["g_mix"], l),
                    name="in_proj_odd")
            mixed = []
            for s, (u, seg) in enumerate(zip(us, segs)):
                y, sr, si = _s5(u, lay(s, "c_re", j), lay(s, "c_im", j), w["s5"], j, seg)
                mixed.append([y, y])
                news[s]["c_re"].append(sr)
                news[s]["c_im"].append(si)
            outs = _matmul(mixed, wsel("w_glu_v", j) + wsel("w_glu_g", j), D_MODEL, tm=512,
                           tn=512 if emit else D_MODEL, res_sets=xs, glu=True, emit=emit, name="glu_odd")
            if emit:
                outs, (wv, wg) = outs
                wb["w_glu_v"][j], wb["w_glu_g"][j] = [wv], [wg]
            xs = outs
        ffn = _ffn_up(xs, w["g_ffn"], [lay(s, "f_conv", l) for s in range(n)], wsel("w_up", l)[0][:2],
                      wsel("w_gate", l)[0][:2], w["conv_f_w"], w["conv_f_b"], l, segs, tf=256 if emit else 512,
                      emit=emit)
        if emit:
            ffn, (wu, wg) = ffn
            wb["w_up"][l], wb["w_gate"][l] = [wu], [wg]
        xs = mm("w_down", l, [[act] for act, _ in ffn], D_MODEL, 512, 256, tm=1024, res_sets=xs, name="ffn_down")
        for s, (_, tail) in enumerate(ffn):
            news[s]["f_conv"].append(tail)
    for s in range(n):
        if s_bufs[s] is not None:
            news[s]["b_S"] = s_bufs[s]
    return [_rmsnorm(x, w["g_final"]) for x in xs], news, wb


def kernel(x_prompt, x_sample, state_a_conv, state_a_h, state_b_S, state_c_re, state_c_im, state_ffn_conv,
           meta_tokens, g_mix, w_in_e, conv_a_w, conv_a_b, w_ra, b_ra, w_ia, b_ia, lam_a, lb_logits, gn_b,
           w_out_e, w_in_c, lam_re, lam_im, log_dt, bmat_re, bmat_im, cmat_re, cmat_im, d_skip,
           w_glu_v, w_glu_g, g_ffn, w_up, w_gate, conv_f_w, conv_f_b, w_down, g_final):
    n_even, n_odd = w_in_e.shape[0], w_in_c.shape[0]
    batch, seq = x_prompt.shape[0], x_prompt.shape[1]
    dec_batch, dec_seq = x_sample.shape[0], x_sample.shape[1]

    sm = jax.nn.softmax(lb_logits.astype(F32), axis=0)
    lb_all = jnp.clip(jnp.clip(jnp.cumsum(sm, axis=0) - sm[0:1], 0.0, 1.0), 0.0, 1.0)
    w = {
        "g_mix": g_mix.reshape(DEPTH, 1, D_MODEL), "g_ffn": g_ffn.reshape(DEPTH, 1, D_MODEL), "g_final": g_final,
        "w_in_e": w_in_e, "w_out_e": w_out_e,
        "conv_a_w": conv_a_w, "conv_a_b": conv_a_b.reshape(n_even, 1, D_A),
        "w_ra": w_ra.astype(BF16), "b_ra": b_ra.reshape(n_even, 1, D_A),
        "w_ia": w_ia.astype(BF16), "b_ia": b_ia.reshape(n_even, 1, D_A), "lam_a": lam_a.reshape(n_even, 1, D_A),
        "lb_all": lb_all.reshape(n_even, 1, D_B), "gn_b": gn_b.reshape(n_even, 1, D_B),
        "w_in_c": w_in_c, "w_glu_v": w_glu_v, "w_glu_g": w_glu_g,
        "s5": _s5_params(lam_re, lam_im, log_dt, bmat_re, bmat_im, cmat_re, cmat_im, d_skip),
        "w_up": w_up, "w_gate": w_gate, "w_down": w_down,
        "conv_f_w": conv_f_w, "conv_f_b": conv_f_b.reshape(DEPTH, 1, D_FF),
    }

    meta_seg = Seg(G=1, sh=1, T=N_META, reset_first=True, share_state=False)
    zero = {
        "a_conv": jnp.zeros((1, 1, CONV_A - 1, D_A), F32),
        "a_h": jnp.zeros((1, 1, 1, D_A), F32),
        "b_S": jnp.zeros((1, 1, H_B, HEAD, HEAD), F32),
        "c_re": jnp.zeros((1, 1, 1, STATE_C), F32),
        "c_im": jnp.zeros((1, 1, 1, STATE_C), F32),
        "f_conv": jnp.zeros((1, 1, CONV_F - 1, D_FF), F32),
    }

    s_seg = Seg(G=1, sh=dec_batch, T=dec_seq, reset_first=False, share_state=False)
    s_init = {
        "a_conv": jnp.swapaxes(state_a_conv, 1, 2).reshape(n_even, 1, (CONV_A - 1) * dec_batch, D_A),
        "a_h": state_a_h.reshape(n_even, 1, dec_batch, D_A),
        "b_S": state_b_S,
        "c_re": state_c_re.reshape(n_odd, 1, dec_batch, STATE_C),
        "c_im": state_c_im.reshape(n_odd, 1, dec_batch, STATE_C),
        "f_conv": jnp.swapaxes(state_ffn_conv, 1, 2).reshape(DEPTH, 1, (CONV_F - 1) * dec_batch, D_FF),
    }
    (ys, _), (s_st, meta_st), wb = _trunk(
        [jnp.swapaxes(x_sample, 0, 1).reshape(dec_seq * dec_batch, D_MODEL), meta_tokens.astype(F32)],
        [s_seg, meta_seg], [s_init, zero], w, n_even)

    p_seg = Seg(G=batch, sh=1, T=seq, reset_first=False, share_state=True)
    p_init = {k: jnp.stack(v) for k, v in meta_st.items()}
    (yp,), (p_st,), _ = _trunk([x_prompt.reshape(batch * seq, D_MODEL)], [p_seg], [p_init], w, n_even, wb)

    y_prompt = yp.reshape(batch, seq, D_MODEL)
    y_sample = jnp.swapaxes(ys.reshape(dec_seq, dec_batch, D_MODEL), 0, 1)
    p_out = (jnp.stack(p_st["a_conv"]),
             jnp.stack(p_st["a_h"]).reshape(n_even, batch, D_A),
             jnp.stack(p_st["b_S"]),
             jnp.stack(p_st["c_re"]).reshape(n_odd, batch, G_C, P_C),
             jnp.stack(p_st["c_im"]).reshape(n_odd, batch, G_C, P_C),
             jnp.stack(p_st["f_conv"]))
    s_out = (jnp.swapaxes(jnp.stack(s_st["a_conv"]).reshape(n_even, CONV_A - 1, dec_batch, D_A), 1, 2),
             jnp.stack(s_st["a_h"]).reshape(n_even, dec_batch, D_A),
             s_st["b_S"],
             jnp.stack(s_st["c_re"]).reshape(n_odd, dec_batch, G_C, P_C),
             jnp.stack(s_st["c_im"]).reshape(n_odd, dec_batch, G_C, P_C),
             jnp.swapaxes(jnp.stack(s_st["f_conv"]).reshape(DEPTH, CONV_F - 1, dec_batch, D_FF), 1, 2))
    return (y_prompt, y_sample) + p_out + s_out
```

```python
import functools
from typing import NamedTuple

import jax
import jax.numpy as jnp
from jax import lax
from jax.experimental import pallas as pl
from jax.experimental.pallas import tpu as pltpu

F32 = jnp.float32
BF16 = jnp.bfloat16

D_MODEL = 2048
DEPTH = 4
N_META = 16
D_A = 1024
CONV_A = 4
C_RG = 8.0
D_B = 1024
H_B = 8
HEAD = 128
LB_FLOOR = 1e-30
D_C = 1024
GROUP_C = 16
G_C = 64
P_C = 64
STATE_C = G_C * P_C
D_FF = 5504
CONV_F = 3
EPS = 1e-6
GLA_SAFE_DECAY = 60.0
GLA_MIN_HALF = 32

LANE = 128
SUBLANE = 8
S5_LANES = LANE * P_C // GROUP_C
VMEM_LIMIT = 52 * 1024 * 1024


class Seg(NamedTuple):
    G: int
    sh: int
    T: int
    reset_first: bool
    share_state: bool


def _params(n_axes):
    return pltpu.CompilerParams(dimension_semantics=("arbitrary",) * n_axes, vmem_limit_bytes=VMEM_LIMIT)


def _dot(a, b):
    return jnp.dot(a, b, preferred_element_type=F32)


def _dot_nt(a, b):
    return lax.dot_general(a, b, (((1,), (1,)), ((), ())), preferred_element_type=F32)


def _dot_tn(a, b):
    return lax.dot_general(a, b, (((0,), (0,)), ((), ())), preferred_element_type=F32)


def _softplus(x):
    return jnp.maximum(x, 0.0) + jnp.log1p(jnp.exp(-jnp.abs(x)))


def _rms(x, g):
    ms = jnp.mean(x * x, axis=-1, keepdims=True)
    return x * lax.rsqrt(ms + EPS) * g


def _delay(x, prev, k, sh):
    rows = x.shape[0]
    n = k * sh
    p = prev.shape[0]
    if sh % SUBLANE == 0:
        return jnp.concatenate([prev[p - n:], x[:rows - n]], axis=0)
    assert sh == 1
    y = pltpu.roll(x, n, 0)
    row = lax.broadcasted_iota(jnp.int32, x.shape, 0)
    for i in range(n):
        y = jnp.where(row == i, prev[p - n + i:p - n + i + 1], y)
    return y


class _Value:
    def __init__(self, value):
        self.value = value

    def __getitem__(self, idx):
        return self.value


def _norm_kernel(x_ref, g_ref, o_ref):
    o_ref[...] = _rms(x_ref[...], g_ref[...]).astype(o_ref.dtype)


def _rmsnorm(x, g):
    rows = x.shape[0]
    tm = min(rows, 512)
    return pl.pallas_call(
        _norm_kernel,
        grid=(rows // tm,),
        in_specs=[pl.BlockSpec((tm, D_MODEL), lambda i: (i, 0)),
                  pl.BlockSpec((1, D_MODEL), lambda i: (0, 0))],
        out_specs=pl.BlockSpec((tm, D_MODEL), lambda i: (i, 0)),
        out_shape=jax.ShapeDtypeStruct((rows, D_MODEL), F32),
        compiler_params=_params(1),
        name="rmsnorm",
    )(x, g.reshape(1, D_MODEL))


def _mm_kernel(*refs, n_sets, n_in, has_res, glu, norm, emit):
    pos = n_sets * n_in
    ws = refs[pos:pos + n_in]
    pos += n_in
    if norm:
        g_ref = refs[pos]
        pos += 1
    res_refs = refs[pos:pos + n_sets] if has_res else (None,) * n_sets
    pos += n_sets if has_res else 0
    out_refs = refs[pos:pos + n_sets]
    pos += n_sets
    if emit:
        tiles = [w_ref[...].astype(BF16) for w_ref in ws]
        for wb_ref, tile in zip(refs[pos:pos + n_in], tiles):
            wb_ref[...] = tile
        ws = [_Value(t) for t in tiles]
        pos += n_in
    h_scs = refs[pos:]
    for s in range(n_sets):
        xs = list(refs[s * n_in:(s + 1) * n_in])
        if norm:
            @pl.when(pl.program_id(1) == 0)
            def _(x_ref=xs[0], h_sc=h_scs[s]):
                h_sc[...] = _rms(x_ref[...], g_ref[...]).astype(BF16)

            xs[0] = h_scs[s]
        if glu:
            acc = _dot(xs[0][...], ws[0][...]) * jax.nn.sigmoid(_dot(xs[1][...], ws[1][...]))
        else:
            acc = _dot(xs[0][...], ws[0][...])
            for x_ref, w_ref in zip(xs[1:], ws[1:]):
                acc = acc + _dot(x_ref[...], w_ref[...])
        if has_res:
            acc = res_refs[s][...] + acc
        out_refs[s][...] = acc.astype(out_refs[s].dtype)


def _matmul(xs_sets, ws, n_out, *, tm, tn, res_sets=None, glu=False, norm_g=None, emit=False, name="matmul"):
    n_sets, n_in = len(xs_sets), len(ws)
    assert not emit or n_sets > 1
    rows = [xs[0].shape[0] for xs in xs_sets]
    tms = [min(r, tm) for r in rows] if n_sets == 1 else rows
    in_specs, args, scratch = [], [], []
    for xs, t in zip(xs_sets, tms):
        in_specs += [pl.BlockSpec((t, x.shape[1]), lambda i, j: (i, 0)) for x in xs]
        args += list(xs)
    in_specs += [pl.BlockSpec((None, x.shape[1], tn), functools.partial(lambda i, j, l, kb: (l, kb, j), l=l, kb=kb))
                 for x, (_, l, kb) in zip(xs_sets[0], ws)]
    args += [w for w, _, _ in ws]
    if norm_g is not None:
        g, gl = norm_g
        in_specs.append(pl.BlockSpec((None, 1, D_MODEL), lambda i, j: (gl, 0, 0)))
        args.append(g)
        scratch = [pltpu.VMEM((t, D_MODEL), BF16) for t in tms]
    if res_sets is not None:
        in_specs += [pl.BlockSpec((t, tn), lambda i, j: (i, j)) for t in tms]
        args += list(res_sets)
    out_specs = [pl.BlockSpec((t, tn), lambda i, j: (i, j)) for t in tms]
    out_shape = [jax.ShapeDtypeStruct((r, n_out), F32) for r in rows]
    if emit:
        out_specs += [pl.BlockSpec((None, x.shape[1], tn), lambda i, j: (0, 0, j)) for x in xs_sets[0]]
        out_shape += [jax.ShapeDtypeStruct((1, x.shape[1], n_out), BF16) for x in xs_sets[0]]
    outs = pl.pallas_call(
        functools.partial(_mm_kernel, n_sets=n_sets, n_in=n_in, has_res=res_sets is not None, glu=glu,
                          norm=norm_g is not None, emit=emit),
        grid=(rows[0] // tms[0], n_out // tn),
        in_specs=in_specs,
        out_specs=out_specs,
        out_shape=out_shape,
        scratch_shapes=scratch,
        compiler_params=_params(2),
        name=name,
    )(*args)
    return (outs[:n_sets], outs[n_sets:]) if emit else outs


def _ffn_up_kernel(*refs, shapes, emit):
    n_in = sum(3 if nblk > 1 else 2 for _, _, nblk in shapes)
    g_ref, wu_ref, wg_ref, cw_ref, cb_ref = refs[n_in:n_in + 5]
    n_out = 2 * len(shapes) + (2 if emit else 0)
    ins, outs, scs = list(refs[:n_in]), list(refs[n_in + 5:n_in + 5 + n_out]), list(refs[n_in + 5 + n_out:])
    if emit:
        wu_ref, wg_ref = _Value(wu_ref[...].astype(BF16)), _Value(wg_ref[...].astype(BF16))
        outs[-2][...] = wu_ref[...]
        outs[-1][...] = wg_ref[...]
    for sh, rows, nblk in shapes:
        x_ref = ins.pop(0)
        xp_ref = ins.pop(0) if nblk > 1 else None
        fb_ref = ins.pop(0)
        act_ref, tail_ref = outs.pop(0), outs.pop(0)
        h_sc = scs.pop(0)
        hp_sc = scs.pop(0) if nblk > 1 else None

        @pl.when(pl.program_id(1) == 0)
        def _(x_ref=x_ref, xp_ref=xp_ref, h_sc=h_sc, hp_sc=hp_sc):
            h_sc[...] = _rms(x_ref[...], g_ref[...]).astype(BF16)
            if xp_ref is not None:
                hp_sc[...] = _rms(xp_ref[...], g_ref[...]).astype(BF16)

        h = h_sc[...]
        up = _dot(h, wu_ref[...])
        gate = _dot(h, wg_ref[...])
        prev = fb_ref[...]
        if nblk > 1:
            up_prev = _dot(hp_sc[...], wu_ref[...])[SUBLANE - 2:]
            prev = jnp.where(pl.program_id(0) % nblk == 0, prev, up_prev)
        upc = (cb_ref[...] + cw_ref[0:1, :] * _delay(up, prev, 2, sh)
               + cw_ref[1:2, :] * _delay(up, prev, 1, sh) + cw_ref[2:3, :] * up)
        act_ref[...] = (jax.nn.gelu(upc) * gate).astype(act_ref.dtype)
        tail_ref[0] = up[rows - 2 * sh:]


def _ffn_up(xs, g, fbufs, w_up, w_gate, cw, cb, l, segs, *, tf, emit=False):
    (w_up, lu), (w_gate, lg) = w_up, w_gate
    assert not emit or len(xs) > 1
    shapes, in_specs, args, scratch, out_specs, out_shape = [], [], [], [], [], []
    for x, (fb, fl), seg in zip(xs, fbufs, segs):
        G, sh, T = seg.G, seg.sh, seg.T
        tt = min(T, 1024 // sh)
        rows, nblk = tt * sh, T // tt
        assert len(xs) == 1 or G * nblk == 1
        shapes.append((sh, rows, nblk))
        gi = (lambda i: 0) if seg.share_state else functools.partial(lambda i, nblk: i // nblk, nblk=nblk)
        in_specs.append(pl.BlockSpec((rows, D_MODEL), lambda i, j: (i, 0),
                                     **({"pipeline_mode": pl.Buffered(1)} if emit else {})))
        args.append(x)
        scratch.append(pltpu.VMEM((rows, D_MODEL), BF16))
        if nblk > 1:
            in_specs.append(pl.BlockSpec((SUBLANE, D_MODEL), functools.partial(
                lambda i, j, per: (jnp.maximum(i * per - 1, 0), 0), per=rows // SUBLANE)))
            args.append(x)
            scratch.append(pltpu.VMEM((SUBLANE, D_MODEL), BF16))
        in_specs.append(pl.BlockSpec((None, None, 2 * sh, tf),
                                     functools.partial(lambda i, j, fl, gi: (fl, gi(i), 0, j), fl=fl, gi=gi)))
        args.append(fb)
        out_specs += [pl.BlockSpec((rows, tf), lambda i, j: (i, j)),
                      pl.BlockSpec((1, 2 * sh, tf), lambda i, j: (i, 0, j))]
        out_shape += [jax.ShapeDtypeStruct((G * T * sh, D_FF), BF16),
                      jax.ShapeDtypeStruct((G * nblk, 2 * sh, D_FF), F32)]
    in_specs += [pl.BlockSpec((None, 1, D_MODEL), lambda i, j: (l, 0, 0)),
                 pl.BlockSpec((None, D_MODEL, tf), lambda i, j: (lu, 0, j)),
                 pl.BlockSpec((None, D_MODEL, tf), lambda i, j: (lg, 0, j)),
                 pl.BlockSpec((None, CONV_F, tf), lambda i, j: (l, 0, j)),
                 pl.BlockSpec((None, 1, tf), lambda i, j: (l, 0, j))]
    args += [g, w_up, w_gate, cw, cb]
    if emit:
        out_specs += [pl.BlockSpec((None, D_MODEL, tf), lambda i, j: (0, 0, j))] * 2
        out_shape += [jax.ShapeDtypeStruct((1, D_MODEL, D_FF), BF16)] * 2
    n_row_blocks = segs[0].G * shapes[0][2]
    outs = pl.pallas_call(
        functools.partial(_ffn_up_kernel, shapes=tuple(shapes), emit=emit),
        grid=(n_row_blocks, pl.cdiv(D_FF, tf)),
        in_specs=in_specs,
        out_specs=out_specs,
        out_shape=out_shape,
        scratch_shapes=scratch,
        compiler_params=_params(2),
        name="ffn_up",
    )(*args)
    per_set = [(outs[2 * s], outs[2 * s + 1][nblk - 1::nblk]) for s, (_, _, nblk) in enumerate(shapes)]
    return (per_set, outs[2 * len(shapes):]) if emit else per_set


def _rglru_kernel(xa_ref, ga_ref, cst_ref, h0_ref, cw_ref, cb_ref, wr_ref, br_ref, wi_ref, bi_ref, lam_ref,
                  oa_ref, cso_ref, ho_ref, prev_sc, h_sc, a_sc, b_sc, *, sh, tt, heads, reset_first):
    tb = pl.program_id(2)
    rows = tt * sh

    @pl.when(tb == 0)
    def _():
        prev_sc[...] = cst_ref[...]
        h_sc[...] = h0_ref[...]

    xa = xa_ref[...]
    prev = prev_sc[...]
    xc = cb_ref[...] + cw_ref[CONV_A - 1:CONV_A, :] * xa
    for k in range(CONV_A - 1):
        xc = xc + cw_ref[k:k + 1, :] * _delay(xa, prev, CONV_A - 1 - k, sh)
    new_prev = xa[rows - (CONV_A - 1) * sh:]
    prev_sc[...] = new_prev
    cso_ref[0] = new_prev

    row = lax.broadcasted_iota(jnp.int32, (rows, HEAD), 0)
    for hd in range(heads):
        cs = slice(hd * HEAD, (hd + 1) * HEAD)
        xh = xc[:, cs]
        xb = xh.astype(BF16)
        r = jax.nn.sigmoid(_dot(xb, wr_ref[hd]) + br_ref[:, cs])
        ig = jax.nn.sigmoid(_dot(xb, wi_ref[hd]) + bi_ref[:, cs])
        a = jnp.exp(-C_RG * r * _softplus(-lam_ref[:, cs]))
        mult = jnp.sqrt(1.0 - a * a)
        if reset_first:
            mult = jnp.where((row < sh) & (tb == 0), 1.0, mult)
        a_sc[:, cs] = a
        b_sc[:, cs] = mult * ig * xh

    if sh == 1:
        width = a_sc.shape[1]
        a3 = a_sc[...].reshape(rows // SUBLANE, SUBLANE, width)
        b3 = b_sc[...].reshape(rows // SUBLANE, SUBLANE, width)
        sub = lax.broadcasted_iota(jnp.int32, a3.shape, 1)
        d = 1
        while d < SUBLANE:
            keep = sub >= d
            b3 = b3 + a3 * jnp.where(keep, pltpu.roll(b3, d, 1), 0.0)
            a3 = a3 * jnp.where(keep, pltpu.roll(a3, d, 1), 1.0)
            d *= 2
        a_sc[...] = a3.reshape(rows, width)
        b_sc[...] = b3.reshape(rows, width)

        def tile_step(k, h):
            sl = pl.ds(pl.multiple_of(k * SUBLANE, SUBLANE), SUBLANE)
            ht = b_sc[sl, :] + a_sc[sl, :] * jnp.broadcast_to(h, (SUBLANE, width))
            b_sc[sl, :] = ht
            return ht[SUBLANE - 1:]

        h = lax.fori_loop(0, rows // SUBLANE, tile_step, h_sc[...], unroll=min(4, rows // SUBLANE))
    else:
        def step(t, h):
            sl = pl.ds(pl.multiple_of(t * sh, sh), sh)
            h = a_sc[sl, :] * h + b_sc[sl, :]
            b_sc[sl, :] = h
            return h

        h = lax.fori_loop(0, tt, step, h_sc[...], unroll=True)
    h_sc[...] = h
    ho_ref[0] = h
    oa_ref[...] = (b_sc[...] * jax.nn.gelu(ga_ref[...])).astype(oa_ref.dtype)


def _rglru(z, a_conv, a_h, prm, l, seg):
    G, sh, T = seg.G, seg.sh, seg.T
    tt = min(T, 512) if sh == 1 else T
    rows, nblk = tt * sh, T // tt
    heads = 8 if sh == 1 else 2
    cwid = heads * HEAD
    ncb = D_A // cwid
    (ac, acl), (ah, ahl) = a_conv, a_h
    gi = (lambda g: 0) if seg.share_state else (lambda g: g)
    row_blk = lambda c, g, tb: g * nblk + tb
    vec = lambda n: pl.BlockSpec((None, n, cwid), lambda c, g, tb: (l, 0, c))
    mat = pl.BlockSpec((None, heads, HEAD, HEAD), lambda c, g, tb: (l, c, 0, 0))
    return pl.pallas_call(
        functools.partial(_rglru_kernel, sh=sh, tt=tt, heads=heads, reset_first=seg.reset_first),
        grid=(ncb, G, nblk),
        in_specs=[pl.BlockSpec((rows, cwid), lambda c, g, tb: (row_blk(c, g, tb), c)),
                  pl.BlockSpec((rows, cwid), lambda c, g, tb: (row_blk(c, g, tb), ncb + c)),
                  pl.BlockSpec((None, None, (CONV_A - 1) * sh, cwid), lambda c, g, tb: (acl, gi(g), 0, c)),
                  pl.BlockSpec((None, None, sh, cwid), lambda c, g, tb: (ahl, gi(g), 0, c)),
                  vec(CONV_A), vec(1), mat, vec(1), mat, vec(1), vec(1)],
        out_specs=[pl.BlockSpec((rows, cwid), lambda c, g, tb: (row_blk(c, g, tb), c)),
                   pl.BlockSpec((1, (CONV_A - 1) * sh, cwid), lambda c, g, tb: (g, 0, c)),
                   pl.BlockSpec((1, sh, cwid), lambda c, g, tb: (g, 0, c))],
        out_shape=[jax.ShapeDtypeStruct((G * T * sh, D_A), BF16),
                   jax.ShapeDtypeStruct((G, (CONV_A - 1) * sh, D_A), F32),
                   jax.ShapeDtypeStruct((G, sh, D_A), F32)],
        scratch_shapes=[pltpu.VMEM(((CONV_A - 1) * sh, cwid), F32),
                        pltpu.VMEM((sh, cwid), F32),
                        pltpu.VMEM((rows, cwid), F32),
                        pltpu.VMEM((rows, cwid), F32)],
        compiler_params=_params(3),
        name="rglru",
    )(z, z, ac, ah, prm["conv_a_w"], prm["conv_a_b"], prm["w_ra"], prm["b_ra"], prm["w_ia"], prm["b_ia"],
      prm["lam_a"])


def _cumsum_rows(x):
    n = x.shape[0]
    row = lax.broadcasted_iota(jnp.int32, x.shape, 0)
    d = 1
    while d < n:
        x = x + jnp.where(row >= d, pltpu.roll(x, d, 0), 0.0)
        d *= 2
    return x


def _gla_pairwise(q, bc, k, v):
    n = q.shape[0]
    row = lax.broadcasted_iota(jnp.int32, (n, 1), 0)
    o = jnp.zeros((n, HEAD), F32)
    for s in range(n):
        diff = jnp.where(row >= s, bc - bc[s:s + 1], 0.0)
        w = q * k[s:s + 1] * jnp.exp(diff)
        a = jnp.where(row >= s, jnp.sum(w, axis=-1, keepdims=True), 0.0)
        o = o + a * v[s:s + 1]
    return o


def _gla_chunk_fast(q, bc, k, v, st):
    c = q.shape[0]
    h = c // 2
    b_mid = bc[h - 1:h]
    bl = jnp.concatenate([bc[:h], bc[h:] - b_mid], axis=0)
    qt = q * jnp.exp(bl)
    qtb = qt.astype(BF16)
    kt = (k * jnp.exp(-bl)).astype(BF16)
    vb = v.astype(BF16)
    row0 = lax.broadcasted_iota(jnp.int32, (h, h), 0)
    col0 = lax.broadcasted_iota(jnp.int32, (h, h), 1)
    a0 = jnp.where(row0 >= col0, _dot_nt(qtb[:h], kt[:h]), 0.0)
    o0 = _dot(a0.astype(BF16), vb[:h])
    kh = (k[:h] * jnp.exp(b_mid - bc[:h])).astype(BF16)
    row1 = lax.broadcasted_iota(jnp.int32, (h, c), 0) + h
    col1 = lax.broadcasted_iota(jnp.int32, (h, c), 1)
    a1 = jnp.where(row1 >= col1, _dot_nt(qtb[h:], jnp.concatenate([kh, kt[h:]], axis=0)), 0.0)
    o1 = _dot(a1.astype(BF16), vb)
    qs = jnp.concatenate([qt[:h], qt[h:] * jnp.exp(b_mid)], axis=0)
    o = jnp.concatenate([o0, o1], axis=0) + _dot_nt(qs.astype(BF16), st.astype(BF16))
    b_end = bc[c - 1:c]
    kd = k * jnp.exp(b_end - bc)
    st_new = st * jnp.exp(b_end) + _dot_tn(vb, kd.astype(BF16))
    return o, st_new


def _gla_chunk(q, bc, k, v, st, base):
    c = q.shape[0]
    o_state = _dot_nt((q * jnp.exp(bc)).astype(BF16), st.astype(BF16))
    parts = [_gla_pairwise(q[i:i + base], bc[i:i + base], k[i:i + base], v[i:i + base])
             for i in range(0, c, base)]
    s = base
    while s < c:
        for p in range(0, c, 2 * s):
            bm = bc[p + s - 1:p + s]
            qh = q[p + s:p + 2 * s] * jnp.exp(bc[p + s:p + 2 * s] - bm)
            kh = k[p:p + s] * jnp.exp(bm - bc[p:p + s])
            att = _dot_nt(qh.astype(BF16), kh.astype(BF16))
            upd = _dot(att.astype(BF16), v[p:p + s].astype(BF16))
            for i in range(s // base):
                idx = (p + s) // base + i
                parts[idx] = parts[idx] + upd[i * base:(i + 1) * base]
        s *= 2
    o = o_state + (parts[0] if len(parts) == 1 else jnp.concatenate(parts, axis=0))
    b_end = bc[c - 1:c]
    kd = k * jnp.exp(b_end - bc)
    st_new = st * jnp.exp(b_end) + _dot_tn(v.astype(BF16), kd.astype(BF16))
    return o, st_new


def _gla_block_kv(q, bc, k, v, s):
    n = q.shape[0]
    o = _dot((q * jnp.exp(bc)).astype(BF16), s.astype(BF16)) + _gla_pairwise(q, bc, k, v)
    b_end = bc[n - 1:n]
    kd = k * jnp.exp(b_end - bc)
    decay_col = jnp.broadcast_to(jnp.exp(b_end), (SUBLANE, HEAD)).T[:, 0:1]
    s_new = s * decay_col + _dot_tn(kd.astype(BF16), v.astype(BF16))
    return o, s_new


def _hgrn2_gates(fz, lb):
    la = jnp.log(jnp.maximum(lb, LB_FLOOR))
    lbv = jnp.log1p(-lb) + jnp.minimum(fz, 0.0) - jnp.log1p(jnp.exp(-jnp.abs(fz)))
    logf = jnp.maximum(la, lbv) + jnp.log1p(jnp.exp(-jnp.abs(la - lbv)))
    k = (1.0 - lb) * jax.nn.sigmoid(-fz)
    return logf, k


def _hgrn2_out(o, gn, gb):
    o = o * lax.rsqrt(jnp.mean(o * o, axis=-1, keepdims=True) + EPS)
    return o * gn * jax.nn.silu(gb)


def _hgrn2_long_kernel(q_ref, fz_ref, v_ref, gb_ref, s0_ref, lb_ref, gn_ref, ob_ref, so_ref, st_sc,
                       *, tt, chunk, base, nblk, hb):
    tb = pl.program_id(2)

    @pl.when(tb == 0)
    def _():
        for hd in range(hb):
            st_sc[hd] = s0_ref[hd].T

    def body(ci, carry):
        sl = pl.ds(pl.multiple_of(ci * chunk, chunk), chunk)
        heads = []
        for hd in range(hb):
            cs = slice(hd * HEAD, (hd + 1) * HEAD)
            logf, k = _hgrn2_gates(fz_ref[sl, cs], lb_ref[:, cs])
            heads.append((hd, cs, _cumsum_rows(logf), k))

        def run(chunk_fn):
            for hd, cs, bc, k in heads:
                o, st_new = chunk_fn(q_ref[sl, cs], bc, k, v_ref[sl, cs], st_sc[hd])
                st_sc[hd] = st_new
                ob_ref[sl, cs] = _hgrn2_out(o, gn_ref[:, cs], gb_ref[sl, cs]).astype(ob_ref.dtype)

        exact = functools.partial(_gla_chunk, base=base)
        if chunk < 2 * GLA_MIN_HALF:
            run(exact)
        else:
            half = chunk // 2
            decay = None
            for _, _, bc, _ in heads:
                d = jnp.maximum(-bc[half - 1:half], bc[half - 1:half] - bc[chunk - 1:chunk])
                decay = d if decay is None else jnp.maximum(decay, d)
            safe = jnp.max(decay) <= GLA_SAFE_DECAY
            pl.when(safe)(lambda: run(_gla_chunk_fast))
            pl.when(jnp.logical_not(safe))(lambda: run(exact))
        return carry

    lax.fori_loop(0, tt // chunk, body, 0)

    @pl.when(tb == nblk - 1)
    def _():
        for hd in range(hb):
            so_ref[0, hd] = st_sc[hd].T


def _hgrn2_wide_kernel(*refs, sh, T, nsb, unroll, first, n_slabs):
    if first:
        q_ref, fz_ref, v_ref, gb_ref, s0_ref, lb_ref, gn_ref, ob_ref, so_ref, o_sc = refs
        if n_slabs > 1:
            so_ref[1:] = jnp.zeros((n_slabs - 1, nsb, 1, HEAD, HEAD), F32)
        so = so_ref.at[0]
    else:
        q_ref, fz_ref, v_ref, gb_ref, s0_ref, lb_ref, gn_ref, _, ob_ref, so, o_sc = refs
    sb = pl.program_id(1)
    lb = lb_ref[...]

    def body(it, carry):
        for u in range(unroll):
            jj = it * unroll + u
            sl = pl.ds(sb * nsb + jj, T, stride=sh)
            logf, k = _hgrn2_gates(fz_ref[sl, :], lb)
            o, s_new = _gla_block_kv(q_ref[sl, :], _cumsum_rows(logf), k, v_ref[sl, :], s0_ref[jj, 0])
            so[jj, 0] = s_new
            o_sc[sl, :] = o
        return carry

    lax.fori_loop(0, nsb // unroll, body, 0)

    @pl.when(sb == sh // nsb - 1)
    def _():
        ob_ref[...] = _hgrn2_out(o_sc[...], gn_ref[...], gb_ref[...]).astype(ob_ref.dtype)


def _hgrn2(z, s0, prm, l, seg, n_slabs=1, slab=0, s_buf=None):
    G, sh, T = seg.G, seg.sh, seg.T
    nseq = G * sh
    (sa, sl_) = s0
    if sh == 1:
        hb = 8
        wid = hb * HEAD
        q0, f0, v0, g0 = (2 * D_A // wid, (2 * D_A + D_B) // wid, (2 * D_A + 2 * D_B) // wid,
                          (2 * D_A + 3 * D_B) // wid)
        tt = min(T, 256)
        chunk = min(tt, 64)
        nblk = T // tt
        gi = (lambda g: 0) if seg.share_state else (lambda g: g)
        col = lambda c0: pl.BlockSpec((tt, wid), lambda h, g, tb: (g * nblk + tb, c0 + h))
        vec = pl.BlockSpec((None, 1, wid), lambda h, g, tb: (l, 0, h))
        return pl.pallas_call(
            functools.partial(_hgrn2_long_kernel, tt=tt, chunk=chunk, base=min(chunk, 16), nblk=nblk, hb=hb),
            grid=(H_B // hb, G, nblk),
            in_specs=[col(q0), col(f0), col(v0), col(g0),
                      pl.BlockSpec((None, None, hb, HEAD, HEAD), lambda h, g, tb: (sl_, gi(g), h, 0, 0)),
                      vec, vec],
            out_specs=[pl.BlockSpec((tt, wid), lambda h, g, tb: (g * nblk + tb, h)),
                       pl.BlockSpec((1, hb, HEAD, HEAD), lambda h, g, tb: (g, h, 0, 0))],
            out_shape=[jax.ShapeDtypeStruct((G * T * sh, D_B), BF16),
                       jax.ShapeDtypeStruct((nseq, H_B, HEAD, HEAD), F32)],
            scratch_shapes=[pltpu.VMEM((hb, HEAD, HEAD), F32)],
            compiler_params=_params(3),
            name="hgrn2_long",
        )(z, z, z, z, sa, prm["lb_all"], prm["gn_b"])
    assert G == 1 and T == SUBLANE
    q0, f0, v0, g0 = 2 * D_A // HEAD, (2 * D_A + D_B) // HEAD, (2 * D_A + 2 * D_B) // HEAD, (2 * D_A + 3 * D_B) // HEAD
    rows = T * sh
    nsb = min(sh, 32)
    unroll = 8 if nsb % 8 == 0 else 1
    col = lambda c0: pl.BlockSpec((rows, HEAD), lambda h, sb: (0, c0 + h))
    vec = pl.BlockSpec((None, 1, HEAD), lambda h, sb: (l, 0, h))
    in_specs = [col(q0), col(f0), col(v0), col(g0),
                pl.BlockSpec((None, nsb, 1, HEAD, HEAD), lambda h, sb: (sl_, sb, h, 0, 0)), vec, vec]
    args = [z, z, z, z, sa, prm["lb_all"], prm["gn_b"]]
    first = s_buf is None
    if first:
        so_spec = pl.BlockSpec((n_slabs, nsb, 1, HEAD, HEAD), lambda h, sb: (0, sb, h, 0, 0))
        aliases = {}
    else:
        in_specs.append(pl.BlockSpec(memory_space=pl.ANY))
        args.append(s_buf)
        so_spec = pl.BlockSpec((None, nsb, 1, HEAD, HEAD), lambda h, sb: (slab, sb, h, 0, 0))
        aliases = {len(args) - 1: 1}
    return pl.pallas_call(
        functools.partial(_hgrn2_wide_kernel, sh=sh, T=T, nsb=nsb, unroll=unroll, first=first, n_slabs=n_slabs),
        grid=(H_B, sh // nsb),
        in_specs=in_specs,
        out_specs=[pl.BlockSpec((rows, HEAD), lambda h, sb: (0, h)), so_spec],
        out_shape=[jax.ShapeDtypeStruct((G * T * sh, D_B), BF16),
                   jax.ShapeDtypeStruct((n_slabs, nseq, H_B, HEAD, HEAD), F32)],
        scratch_shapes=[pltpu.VMEM((rows, HEAD), F32)],
        input_output_aliases=aliases,
        compiler_params=_params(2),
        name="hgrn2_wide",
    )(*args)


def _cmul(ar, ai, br, bi):
    return ar * br - ai * bi, ar * bi + ai * br


def _s5_packed_scan(xr_sc, xi_sc, pwr_sc, pwi_sc, sr0, si0, ar, ai, seg_len):
    lanes = xr_sc.shape[1]
    row = lax.broadcasted_iota(jnp.int32, (SUBLANE, lanes), 0)
    art, ait = jnp.broadcast_to(ar, (SUBLANE, lanes)), jnp.broadcast_to(ai, (SUBLANE, lanes))

    def local(t, carry):
        sr, si = carry
        sl = pl.ds(pl.multiple_of(t * SUBLANE, SUBLANE), SUBLANE)
        pr, pi = _cmul(art, ait, sr, si)
        nr, ni = pr + xr_sc[sl, :], pi + xi_sc[sl, :]
        xr_sc[sl, :] = nr
        xi_sc[sl, :] = ni
        return nr, ni

    init = (jnp.where(row == 0, jnp.broadcast_to(sr0, (SUBLANE, lanes)), 0.0),
            jnp.where(row == 0, jnp.broadcast_to(si0, (SUBLANE, lanes)), 0.0))
    fr, fi = lax.fori_loop(0, seg_len, local, init, unroll=True)

    last = seg_len * SUBLANE - 1
    alr, ali = pwr_sc[last:last + 1, :], pwi_sc[last:last + 1, :]
    cr, ci = fr[0:1], fi[0:1]
    car_r, car_i = jnp.zeros((SUBLANE, lanes), F32), jnp.zeros((SUBLANE, lanes), F32)
    for i in range(1, SUBLANE):
        car_r = jnp.where(row == i, jnp.broadcast_to(cr, (SUBLANE, lanes)), car_r)
        car_i = jnp.where(row == i, jnp.broadcast_to(ci, (SUBLANE, lanes)), car_i)
        pr, pi = _cmul(alr, ali, cr, ci)
        cr, ci = fr[i:i + 1] + pr, fi[i:i + 1] + pi

    def fix(t, carry):
        sl = pl.ds(pl.multiple_of(t * SUBLANE, SUBLANE), SUBLANE)
        dr, di = _cmul(pwr_sc[sl, :], pwi_sc[sl, :], car_r, car_i)
        xr_sc[sl, :] = xr_sc[sl, :] + dr
        xi_sc[sl, :] = xi_sc[sl, :] + di
        return carry

    lax.fori_loop(0, seg_len, fix, 0, unroll=True)
    return cr, ci


def _s5_kernel(*refs, sh, tt):
    if sh == 1:
        (u_ref, sr0_ref, si0_ref, ar_ref, ai_ref, bre_ref, bim_ref, cre_ref, cim_ref, d_ref,
         y_ref, sro_ref, sio_ref, sr_sc, si_sc, xr_sc, xi_sc, up_sc, yp_sc, yt_sc, pwr_sc, pwi_sc) = refs
    else:
        (u_ref, sr0_ref, si0_ref, ar_ref, ai_ref, bre_ref, bim_ref, cre_ref, cim_ref, d_ref,
         y_ref, sro_ref, sio_ref, sr_sc, si_sc, xr_sc, xi_sc) = refs
    tb = pl.program_id(2)
    rows = tt * sh

    @pl.when(tb == 0)
    def _():
        sr_sc[...] = sr0_ref[...]
        si_sc[...] = si0_ref[...]

    if sh == 1:
        seg_len = rows // SUBLANE

        @pl.when((pl.program_id(1) == 0) & (tb == 0))
        def _():
            def grow(t, pw):
                sl = pl.ds(pl.multiple_of(t * SUBLANE, SUBLANE), SUBLANE)
                pwr_sc[sl, :] = jnp.broadcast_to(pw[0], (SUBLANE, S5_LANES))
                pwi_sc[sl, :] = jnp.broadcast_to(pw[1], (SUBLANE, S5_LANES))
                return _cmul(pw[0], pw[1], ar_ref[...], ai_ref[...])

            lax.fori_loop(0, seg_len, grow, (ar_ref[...], ai_ref[...]))

        def pack(t, carry):
            up_sc[pl.ds(pl.multiple_of(t * SUBLANE, SUBLANE), SUBLANE), :] = u_ref[pl.ds(t, SUBLANE, stride=seg_len), :]
            return carry

        lax.fori_loop(0, seg_len, pack, 0, unroll=True)
        u_skip = up_sc[...]
    else:
        u_skip = u_ref[...]
    ub = u_skip.astype(BF16)
    xr_sc[...] = _dot(ub, bre_ref[...])
    xi_sc[...] = _dot(ub, bim_ref[...])

    if sh == 1:
        sr, si = _s5_packed_scan(xr_sc, xi_sc, pwr_sc, pwi_sc, sr_sc[...], si_sc[...], ar_ref[...], ai_ref[...],
                                 seg_len)
    else:
        ar = jnp.broadcast_to(ar_ref[...], (sh, S5_LANES))
        ai = jnp.broadcast_to(ai_ref[...], (sh, S5_LANES))

        def step(t, carry):
            sl = pl.ds(pl.multiple_of(t * sh, sh), sh)
            pr, pi = _cmul(ar, ai, carry[0], carry[1])
            nr, ni = pr + xr_sc[sl, :], pi + xi_sc[sl, :]
            xr_sc[sl, :] = nr
            xi_sc[sl, :] = ni
            return nr, ni

        sr, si = lax.fori_loop(0, tt, step, (sr_sc[...], si_sc[...]), unroll=True)
    sr_sc[...] = sr
    si_sc[...] = si
    sro_ref[0] = sr
    sio_ref[0] = si
    y = (_dot(xr_sc[...].astype(BF16), cre_ref[...]) - _dot(xi_sc[...].astype(BF16), cim_ref[...])
         + d_ref[...] * u_skip)
    out = jax.nn.gelu(y)
    if sh == 1:
        yp_sc[...] = out

        def unpack(t, carry):
            yt_sc[pl.ds(t, SUBLANE, stride=seg_len), :] = yp_sc[pl.ds(pl.multiple_of(t * SUBLANE, SUBLANE), SUBLANE), :]
            return carry

        lax.fori_loop(0, seg_len, unpack, 0, unroll=True)
        out = yt_sc[...]
    y_ref[...] = out.astype(BF16)


def _s5(u, s_re, s_im, prm, j, seg):
    G, sh, T = seg.G, seg.sh, seg.T
    tt = min(T, 512) if sh == 1 else T
    rows, nblk = tt * sh, T // tt
    ncb = D_C // LANE
    (sr, srl), (si, sil) = s_re, s_im
    gi = (lambda g: 0) if seg.share_state else (lambda g: g)
    st_in = lambda lay: pl.BlockSpec((None, None, sh, S5_LANES), lambda c, g, tb: (lay, gi(g), 0, c))
    st_out = pl.BlockSpec((1, sh, S5_LANES), lambda c, g, tb: (g, 0, c))
    vec = pl.BlockSpec((None, 1, S5_LANES), lambda c, g, tb: (j, 0, c))
    bmat = pl.BlockSpec((None, None, LANE, S5_LANES), lambda c, g, tb: (j, c, 0, 0))
    cmat = pl.BlockSpec((None, None, S5_LANES, LANE), lambda c, g, tb: (j, c, 0, 0))
    scratch = [pltpu.VMEM((sh, S5_LANES), F32), pltpu.VMEM((sh, S5_LANES), F32),
               pltpu.VMEM((rows, S5_LANES), F32), pltpu.VMEM((rows, S5_LANES), F32)]
    if sh == 1:
        scratch += [pltpu.VMEM((rows, LANE), F32)] * 3
        scratch += [pltpu.VMEM((rows, S5_LANES), F32)] * 2
    return pl.pallas_call(
        functools.partial(_s5_kernel, sh=sh, tt=tt),
        grid=(ncb, G, nblk),
        in_specs=[pl.BlockSpec((rows, LANE), lambda c, g, tb: (g * nblk + tb, c)),
                  st_in(srl), st_in(sil), vec, vec, bmat, bmat, cmat, cmat,
                  pl.BlockSpec((None, 1, LANE), lambda c, g, tb: (j, 0, c))],
        out_specs=[pl.BlockSpec((rows, LANE), lambda c, g, tb: (g * nblk + tb, c)), st_out, st_out],
        out_shape=[jax.ShapeDtypeStruct((G * T * sh, D_C), BF16),
                   jax.ShapeDtypeStruct((G, sh, STATE_C), F32),
                   jax.ShapeDtypeStruct((G, sh, STATE_C), F32)],
        scratch_shapes=scratch,
        compiler_params=_params(3),
        name="s5",
    )(u, sr, si, prm["ar"], prm["ai"], prm["bre"], prm["bim"], prm["cre"], prm["cim"], prm["d"])


def _s5_params(lam_re, lam_im, log_dt, bmat_re, bmat_im, cmat_re, cmat_im, d_skip):
    n = lam_re.shape[0]
    lr, li = lam_re.astype(F32), lam_im.astype(F32)
    dt = jnp.exp(log_dt.astype(F32))[..., None]
    mag = jnp.exp(lr * dt)
    ar = mag * jnp.cos(li * dt)
    ai = mag * jnp.sin(li * dt)
    den = lr * lr + li * li
    zr = ((ar - 1.0) * lr + ai * li) / den
    zi = (ai * lr - (ar - 1.0) * li) / den
    br_, bi_ = bmat_re.astype(F32), bmat_im.astype(F32)
    bb_re = zr[..., None] * br_ - zi[..., None] * bi_
    bb_im = zr[..., None] * bi_ + zi[..., None] * br_
    gpb = LANE // GROUP_C
    nb = G_C // gpb
    eye = jnp.eye(gpb, dtype=F32)

    def pack_b(bb):
        bb = bb.reshape(n, nb, gpb, P_C, GROUP_C)
        return jnp.einsum("ag,nbapc->nbacgp", eye, bb).reshape(n, nb, LANE, S5_LANES).astype(BF16)

    def pack_c(cm):
        cm = cm.astype(F32).reshape(n, nb, gpb, GROUP_C, P_C)
        return jnp.einsum("ag,nbacp->nbapgc", eye, cm).reshape(n, nb, S5_LANES, LANE).astype(BF16)

    return {"ar": ar.reshape(n, 1, STATE_C), "ai": ai.reshape(n, 1, STATE_C),
            "bre": pack_b(bb_re), "bim": pack_b(bb_im), "cre": pack_c(cmat_re), "cim": pack_c(cmat_im),
            "d": d_skip.astype(F32).reshape(n, 1, D_C)}


def _trunk(xs, segs, sts, w, n_even, wb=None):
    n = len(xs)
    emit = wb is None
    if emit:
        wb = {k: {} for k in ("w_in_e", "w_out_e", "w_in_c", "w_glu_v", "w_glu_g", "w_up", "w_gate", "w_down")}
    news = [{k: [] for k in ("a_conv", "a_h", "b_S", "c_re", "c_im", "f_conv")} for _ in range(n)]
    s_bufs = [None] * n
    lay = lambda s, name, idx: (sts[s][name], min(idx, sts[s][name].shape[0] - 1))

    def wsel(name, layer, kblocks=(0,)):
        if emit:
            return [(w[name], layer, kb) for kb in kblocks]
        return [(part, 0, 0) for part in wb[name][layer]]

    def mm(wname, layer, xs_sets, n_out, tn_one, tn_many, kblocks=(0,), **kw):
        outs = _matmul(xs_sets, wsel(wname, layer, kblocks), n_out, tn=tn_many if emit else tn_one, emit=emit, **kw)
        if emit:
            outs, wb[wname][layer] = outs
        return outs

    for l in range(DEPTH):
        if l % 2 == 0:
            i = l // 2
            zs = mm("w_in_e", i, [[x] for x in xs], 6 * D_A, 1024, 512, tm=1024, norm_g=(w["g_mix"], l),
                    name="in_proj_even")
            mixed = []
            for s, (z, seg) in enumerate(zip(zs, segs)):
                out_a, nb, hl = _rglru(z, lay(s, "a_conv", i), lay(s, "a_h", i), w, i, seg)
                if seg.sh == 1:
                    out_b, s_new = _hgrn2(z, lay(s, "b_S", i), w, i, seg)
                    news[s]["b_S"].append(s_new)
                else:
                    out_b, s_bufs[s] = _hgrn2(z, lay(s, "b_S", i), w, i, seg, n_slabs=n_even, slab=i,
                                              s_buf=s_bufs[s])
                mixed.append([out_a, out_b])
                news[s]["a_conv"].append(nb)
                news[s]["a_h"].append(hl)
            xs = mm("w_out_e", i, mixed, D_MODEL, D_MODEL, 512, kblocks=(0, 1), tm=512, res_sets=xs,
                    name="out_proj_even")
        else:
            j = l // 2
            us = mm("w_in_c", j, [[x] for x in xs], D_C, D_C, 512, tm=1024, norm_g=(w["g_mix"], l),
                    name="in_proj_odd")
            mixed = []
            for s, (u, seg) in enumerate(zip(us, segs)):
                y, sr, si = _s5(u, lay(s, "c_re", j), lay(s, "c_im", j), w["s5"], j, seg)
                mixed.append([y, y])
                news[s]["c_re"].append(sr)
                news[s]["c_im"].append(si)
            outs = _matmul(mixed, wsel("w_glu_v", j) + wsel("w_glu_g", j), D_MODEL, tm=512,
                           tn=512 if emit else D_MODEL, res_sets=xs, glu=True, emit=emit, name="glu_odd")
            if emit:
                outs, (wv, wg) = outs
                wb["w_glu_v"][j], wb["w_glu_g"][j] = [wv], [wg]
            xs = outs
        ffn = _ffn_up(xs, w["g_ffn"], [lay(s, "f_conv", l) for s in range(n)], wsel("w_up", l)[0][:2],
                      wsel("w_gate", l)[0][:2], w["conv_f_w"], w["conv_f_b"], l, segs, tf=512,
                      emit=emit)
        if emit:
            ffn, (wu, wg) = ffn
            wb["w_up"][l], wb["w_gate"][l] = [wu], [wg]
        xs = mm("w_down", l, [[act] for act, _ in ffn], D_MODEL, 512, 256, tm=1024, res_sets=xs, name="ffn_down")
        for s, (_, tail) in enumerate(ffn):
            news[s]["f_conv"].append(tail)
    for s in range(n):
        if s_bufs[s] is not None:
            news[s]["b_S"] = s_bufs[s]
    return [_rmsnorm(x, w["g_final"]) for x in xs], news, wb


def kernel(x_prompt, x_sample, state_a_conv, state_a_h, state_b_S, state_c_re, state_c_im, state_ffn_conv,
           meta_tokens, g_mix, w_in_e, conv_a_w, conv_a_b, w_ra, b_ra, w_ia, b_ia, lam_a, lb_logits, gn_b,
           w_out_e, w_in_c, lam_re, lam_im, log_dt, bmat_re, bmat_im, cmat_re, cmat_im, d_skip,
           w_glu_v, w_glu_g, g_ffn, w_up, w_gate, conv_f_w, conv_f_b, w_down, g_final):
    n_even, n_odd = w_in_e.shape[0], w_in_c.shape[0]
    batch, seq = x_prompt.shape[0], x_prompt.shape[1]
    dec_batch, dec_seq = x_sample.shape[0], x_sample.shape[1]

    sm = jax.nn.softmax(lb_logits.astype(F32), axis=0)
    lb_all = jnp.clip(jnp.clip(jnp.cumsum(sm, axis=0) - sm[0:1], 0.0, 1.0), 0.0, 1.0)
    w = {
        "g_mix": g_mix.reshape(DEPTH, 1, D_MODEL), "g_ffn": g_ffn.reshape(DEPTH, 1, D_MODEL), "g_final": g_final,
        "w_in_e": w_in_e, "w_out_e": w_out_e,
        "conv_a_w": conv_a_w, "conv_a_b": conv_a_b.reshape(n_even, 1, D_A),
        "w_ra": w_ra.astype(BF16), "b_ra": b_ra.reshape(n_even, 1, D_A),
        "w_ia": w_ia.astype(BF16), "b_ia": b_ia.reshape(n_even, 1, D_A), "lam_a": lam_a.reshape(n_even, 1, D_A),
        "lb_all": lb_all.reshape(n_even, 1, D_B), "gn_b": gn_b.reshape(n_even, 1, D_B),
        "w_in_c": w_in_c, "w_glu_v": w_glu_v, "w_glu_g": w_glu_g,
        "s5": _s5_params(lam_re, lam_im, log_dt, bmat_re, bmat_im, cmat_re, cmat_im, d_skip),
        "w_up": w_up, "w_gate": w_gate, "w_down": w_down,
        "conv_f_w": conv_f_w, "conv_f_b": conv_f_b.reshape(DEPTH, 1, D_FF),
    }

    meta_seg = Seg(G=1, sh=1, T=N_META, reset_first=True, share_state=False)
    zero = {
        "a_conv": jnp.zeros((1, 1, CONV_A - 1, D_A), F32),
        "a_h": jnp.zeros((1, 1, 1, D_A), F32),
        "b_S": jnp.zeros((1, 1, H_B, HEAD, HEAD), F32),
        "c_re": jnp.zeros((1, 1, 1, STATE_C), F32),
        "c_im": jnp.zeros((1, 1, 1, STATE_C), F32),
        "f_conv": jnp.zeros((1, 1, CONV_F - 1, D_FF), F32),
    }

    s_seg = Seg(G=1, sh=dec_batch, T=dec_seq, reset_first=False, share_state=False)
    s_init = {
        "a_conv": jnp.swapaxes(state_a_conv, 1, 2).reshape(n_even, 1, (CONV_A - 1) * dec_batch, D_A),
        "a_h": state_a_h.reshape(n_even, 1, dec_batch, D_A),
        "b_S": state_b_S,
        "c_re": state_c_re.reshape(n_odd, 1, dec_batch, STATE_C),
        "c_im": state_c_im.reshape(n_odd, 1, dec_batch, STATE_C),
        "f_conv": jnp.swapaxes(state_ffn_conv, 1, 2).reshape(DEPTH, 1, (CONV_F - 1) * dec_batch, D_FF),
    }
    (ys, _), (s_st, meta_st), wb = _trunk(
        [jnp.swapaxes(x_sample, 0, 1).reshape(dec_seq * dec_batch, D_MODEL), meta_tokens.astype(F32)],
        [s_seg, meta_seg], [s_init, zero], w, n_even)

    p_seg = Seg(G=batch, sh=1, T=seq, reset_first=False, share_state=True)
    p_init = {k: jnp.stack(v) for k, v in meta_st.items()}
    (yp,), (p_st,), _ = _trunk([x_prompt.reshape(batch * seq, D_MODEL)], [p_seg], [p_init], w, n_even, wb)

    y_prompt = yp.reshape(batch, seq, D_MODEL)
    y_sample = jnp.swapaxes(ys.reshape(dec_seq, dec_batch, D_MODEL), 0, 1)
    p_out = (jnp.stack(p_st["a_conv"]),
             jnp.stack(p_st["a_h"]).reshape(n_even, batch, D_A),
             jnp.stack(p_st["b_S"]),
             jnp.stack(p_st["c_re"]).reshape(n_odd, batch, G_C, P_C),
             jnp.stack(p_st["c_im"]).reshape(n_odd, batch, G_C, P_C),
             jnp.stack(p_st["f_conv"]))
    s_out = (jnp.swapaxes(jnp.stack(s_st["a_conv"]).reshape(n_even, CONV_A - 1, dec_batch, D_A), 1, 2),
             jnp.stack(s_st["a_h"]).reshape(n_even, dec_batch, D_A),
             s_st["b_S"],
             jnp.stack(s_st["c_re"]).reshape(n_odd, dec_batch, G_C, P_C),
             jnp.stack(s_st["c_im"]).reshape(n_odd, dec_batch, G_C, P_C),
             jnp.swapaxes(jnp.stack(s_st["f_conv"]).reshape(DEPTH, CONV_F - 1, dec_batch, D_FF), 1, 2))
    return (y_prompt, y_sample) + p_out + s_out
```

```python
import functools
from typing import NamedTuple

import jax
import jax.numpy as jnp
from jax import lax
from jax.experimental import pallas as pl
from jax.experimental.pallas import tpu as pltpu

F32 = jnp.float32
BF16 = jnp.bfloat16

D_MODEL = 2048
DEPTH = 4
N_META = 16
D_A = 1024
CONV_A = 4
C_RG = 8.0
D_B = 1024
H_B = 8
HEAD = 128
LB_FLOOR = 1e-30
D_C = 1024
GROUP_C = 16
G_C = 64
P_C = 64
STATE_C = G_C * P_C
D_FF = 5504
CONV_F = 3
EPS = 1e-6
GLA_SAFE_DECAY = 60.0
GLA_MIN_HALF = 32

LANE = 128
SUBLANE = 8
S5_LANES = LANE * P_C // GROUP_C
VMEM_LIMIT = 52 * 1024 * 1024


class Seg(NamedTuple):
    G: int
    sh: int
    T: int
    reset_first: bool
    share_state: bool


def _params(n_axes):
    return pltpu.CompilerParams(dimension_semantics=("arbitrary",) * n_axes, vmem_limit_bytes=VMEM_LIMIT)


def _dot(a, b):
    return jnp.dot(a, b, preferred_element_type=F32)


def _dot_nt(a, b):
    return lax.dot_general(a, b, (((1,), (1,)), ((), ())), preferred_element_type=F32)


def _dot_tn(a, b):
    return lax.dot_general(a, b, (((0,), (0,)), ((), ())), preferred_element_type=F32)


def _softplus(x):
    return jnp.maximum(x, 0.0) + jnp.log1p(jnp.exp(-jnp.abs(x)))


def _rms(x, g):
    ms = jnp.mean(x * x, axis=-1, keepdims=True)
    return x * lax.rsqrt(ms + EPS) * g


def _delay(x, prev, k, sh):
    rows = x.shape[0]
    n = k * sh
    p = prev.shape[0]
    if sh % SUBLANE == 0:
        return jnp.concatenate([prev[p - n:], x[:rows - n]], axis=0)
    assert sh == 1
    y = pltpu.roll(x, n, 0)
    row = lax.broadcasted_iota(jnp.int32, x.shape, 0)
    for i in range(n):
        y = jnp.where(row == i, prev[p - n + i:p - n + i + 1], y)
    return y


class _Value:
    def __init__(self, value):
        self.value = value

    def __getitem__(self, idx):
        return self.value


def _norm_kernel(x_ref, g_ref, o_ref):
    o_ref[...] = _rms(x_ref[...], g_ref[...]).astype(o_ref.dtype)


def _rmsnorm(x, g):
    rows = x.shape[0]
    tm = min(rows, 512)
    return pl.pallas_call(
        _norm_kernel,
        grid=(rows // tm,),
        in_specs=[pl.BlockSpec((tm, D_MODEL), lambda i: (i, 0)),
                  pl.BlockSpec((1, D_MODEL), lambda i: (0, 0))],
        out_specs=pl.BlockSpec((tm, D_MODEL), lambda i: (i, 0)),
        out_shape=jax.ShapeDtypeStruct((rows, D_MODEL), F32),
        compiler_params=_params(1),
        name="rmsnorm",
    )(x, g.reshape(1, D_MODEL))


def _mm_kernel(*refs, n_sets, n_in, has_res, glu, norm, emit):
    pos = n_sets * n_in
    ws = refs[pos:pos + n_in]
    pos += n_in
    if norm:
        g_ref = refs[pos]
        pos += 1
    res_refs = refs[pos:pos + n_sets] if has_res else (None,) * n_sets
    pos += n_sets if has_res else 0
    out_refs = refs[pos:pos + n_sets]
    pos += n_sets
    if emit:
        tiles = [w_ref[...].astype(BF16) for w_ref in ws]
        for wb_ref, tile in zip(refs[pos:pos + n_in], tiles):
            wb_ref[...] = tile
        ws = [_Value(t) for t in tiles]
        pos += n_in
    h_scs = refs[pos:]
    for s in range(n_sets):
        xs = list(refs[s * n_in:(s + 1) * n_in])
        if norm:
            @pl.when(pl.program_id(1) == 0)
            def _(x_ref=xs[0], h_sc=h_scs[s]):
                h_sc[...] = _rms(x_ref[...], g_ref[...]).astype(BF16)

            xs[0] = h_scs[s]
        if glu:
            acc = _dot(xs[0][...], ws[0][...]) * jax.nn.sigmoid(_dot(xs[1][...], ws[1][...]))
        else:
            acc = _dot(xs[0][...], ws[0][...])
            for x_ref, w_ref in zip(xs[1:], ws[1:]):
                acc = acc + _dot(x_ref[...], w_ref[...])
        if has_res:
            acc = res_refs[s][...] + acc
        out_refs[s][...] = acc.astype(out_refs[s].dtype)


def _matmul(xs_sets, ws, n_out, *, tm, tn, res_sets=None, glu=False, norm_g=None, emit=False, name="matmul"):
    n_sets, n_in = len(xs_sets), len(ws)
    assert not emit or n_sets > 1
    rows = [xs[0].shape[0] for xs in xs_sets]
    tms = [min(r, tm) for r in rows] if n_sets == 1 else rows
    in_specs, args, scratch = [], [], []
    for xs, t in zip(xs_sets, tms):
        in_specs += [pl.BlockSpec((t, x.shape[1]), lambda i, j: (i, 0)) for x in xs]
        args += list(xs)
    in_specs += [pl.BlockSpec((None, x.shape[1], tn), functools.partial(lambda i, j, l, kb: (l, kb, j), l=l, kb=kb))
                 for x, (_, l, kb) in zip(xs_sets[0], ws)]
    args += [w for w, _, _ in ws]
    if norm_g is not None:
        g, gl = norm_g
        in_specs.append(pl.BlockSpec((None, 1, D_MODEL), lambda i, j: (gl, 0, 0)))
        args.append(g)
        scratch = [pltpu.VMEM((t, D_MODEL), BF16) for t in tms]
    if res_sets is not None:
        in_specs += [pl.BlockSpec((t, tn), lambda i, j: (i, j)) for t in tms]
        args += list(res_sets)
    out_specs = [pl.BlockSpec((t, tn), lambda i, j: (i, j)) for t in tms]
    out_shape = [jax.ShapeDtypeStruct((r, n_out), F32) for r in rows]
    if emit:
        out_specs += [pl.BlockSpec((None, x.shape[1], tn), lambda i, j: (0, 0, j)) for x in xs_sets[0]]
        out_shape += [jax.ShapeDtypeStruct((1, x.shape[1], n_out), BF16) for x in xs_sets[0]]
    outs = pl.pallas_call(
        functools.partial(_mm_kernel, n_sets=n_sets, n_in=n_in, has_res=res_sets is not None, glu=glu,
                          norm=norm_g is not None, emit=emit),
        grid=(rows[0] // tms[0], n_out // tn),
        in_specs=in_specs,
        out_specs=out_specs,
        out_shape=out_shape,
        scratch_shapes=scratch,
        compiler_params=_params(2),
        name=name,
    )(*args)
    return (outs[:n_sets], outs[n_sets:]) if emit else outs


def _ffn_up_kernel(*refs, shapes, emit):
    n_in = sum(3 if nblk > 1 else 2 for _, _, nblk in shapes)
    g_ref, wu_ref, wg_ref, cw_ref, cb_ref = refs[n_in:n_in + 5]
    n_out = 2 * len(shapes) + (2 if emit else 0)
    ins, outs, scs = list(refs[:n_in]), list(refs[n_in + 5:n_in + 5 + n_out]), list(refs[n_in + 5 + n_out:])
    if emit:
        wu_ref, wg_ref = _Value(wu_ref[...].astype(BF16)), _Value(wg_ref[...].astype(BF16))
        outs[-2][...] = wu_ref[...]
        outs[-1][...] = wg_ref[...]
    for sh, rows, nblk in shapes:
        x_ref = ins.pop(0)
        xp_ref = ins.pop(0) if nblk > 1 else None
        fb_ref = ins.pop(0)
        act_ref, tail_ref = outs.pop(0), outs.pop(0)
        h_sc = scs.pop(0)
        hp_sc = scs.pop(0) if nblk > 1 else None

        @pl.when(pl.program_id(1) == 0)
        def _(x_ref=x_ref, xp_ref=xp_ref, h_sc=h_sc, hp_sc=hp_sc):
            h_sc[...] = _rms(x_ref[...], g_ref[...]).astype(BF16)
            if xp_ref is not None:
                hp_sc[...] = _rms(xp_ref[...], g_ref[...]).astype(BF16)

        h = h_sc[...]
        up = _dot(h, wu_ref[...])
        gate = _dot(h, wg_ref[...])
        prev = fb_ref[...]
        if nblk > 1:
            up_prev = _dot(hp_sc[...], wu_ref[...])[SUBLANE - 2:]
            prev = jnp.where(pl.program_id(0) % nblk == 0, prev, up_prev)
        upc = (cb_ref[...] + cw_ref[0:1, :] * _delay(up, prev, 2, sh)
               + cw_ref[1:2, :] * _delay(up, prev, 1, sh) + cw_ref[2:3, :] * up)
        act_ref[...] = (jax.nn.gelu(upc) * gate).astype(act_ref.dtype)
        tail_ref[0] = up[rows - 2 * sh:]


def _ffn_up(xs, g, fbufs, w_up, w_gate, cw, cb, l, segs, *, tf, emit=False):
    (w_up, lu), (w_gate, lg) = w_up, w_gate
    assert not emit or len(xs) > 1
    shapes, in_specs, args, scratch, out_specs, out_shape = [], [], [], [], [], []
    for x, (fb, fl), seg in zip(xs, fbufs, segs):
        G, sh, T = seg.G, seg.sh, seg.T
        tt = min(T, 1024 // sh)
        rows, nblk = tt * sh, T // tt
        assert len(xs) == 1 or G * nblk == 1
        shapes.append((sh, rows, nblk))
        gi = (lambda i: 0) if seg.share_state else functools.partial(lambda i, nblk: i // nblk, nblk=nblk)
        in_specs.append(pl.BlockSpec((rows, D_MODEL), lambda i, j: (i, 0),
                                     **({"pipeline_mode": pl.Buffered(1)} if emit else {})))
        args.append(x)
        scratch.append(pltpu.VMEM((rows, D_MODEL), BF16))
        if nblk > 1:
            in_specs.append(pl.BlockSpec((SUBLANE, D_MODEL), functools.partial(
                lambda i, j, per: (jnp.maximum(i * per - 1, 0), 0), per=rows // SUBLANE)))
            args.append(x)
            scratch.append(pltpu.VMEM((SUBLANE, D_MODEL), BF16))
        in_specs.append(pl.BlockSpec((None, None, 2 * sh, tf),
                                     functools.partial(lambda i, j, fl, gi: (fl, gi(i), 0, j), fl=fl, gi=gi)))
        args.append(fb)
        out_specs += [pl.BlockSpec((rows, tf), lambda i, j: (i, j)),
                      pl.BlockSpec((1, 2 * sh, tf), lambda i, j: (i, 0, j))]
        out_shape += [jax.ShapeDtypeStruct((G * T * sh, D_FF), BF16),
                      jax.ShapeDtypeStruct((G * nblk, 2 * sh, D_FF), F32)]
    in_specs += [pl.BlockSpec((None, 1, D_MODEL), lambda i, j: (l, 0, 0)),
                 pl.BlockSpec((None, D_MODEL, tf), lambda i, j: (lu, 0, j)),
                 pl.BlockSpec((None, D_MODEL, tf), lambda i, j: (lg, 0, j)),
                 pl.BlockSpec((None, CONV_F, tf), lambda i, j: (l, 0, j)),
                 pl.BlockSpec((None, 1, tf), lambda i, j: (l, 0, j))]
    args += [g, w_up, w_gate, cw, cb]
    if emit:
        out_specs += [pl.BlockSpec((None, D_MODEL, tf), lambda i, j: (0, 0, j))] * 2
        out_shape += [jax.ShapeDtypeStruct((1, D_MODEL, D_FF), BF16)] * 2
    n_row_blocks = segs[0].G * shapes[0][2]
    outs = pl.pallas_call(
        functools.partial(_ffn_up_kernel, shapes=tuple(shapes), emit=emit),
        grid=(n_row_blocks, pl.cdiv(D_FF, tf)),
        in_specs=in_specs,
        out_specs=out_specs,
        out_shape=out_shape,
        scratch_shapes=scratch,
        compiler_params=_params(2),
        name="ffn_up",
    )(*args)
    per_set = [(outs[2 * s], outs[2 * s + 1][nblk - 1::nblk]) for s, (_, _, nblk) in enumerate(shapes)]
    return (per_set, outs[2 * len(shapes):]) if emit else per_set


def _rglru_kernel(xa_ref, ga_ref, cst_ref, h0_ref, cw_ref, cb_ref, wr_ref, br_ref, wi_ref, bi_ref, lam_ref,
                  oa_ref, cso_ref, ho_ref, prev_sc, h_sc, a_sc, b_sc, *, sh, tt, heads, reset_first):
    tb = pl.program_id(2)
    rows = tt * sh

    @pl.when(tb == 0)
    def _():
        prev_sc[...] = cst_ref[...]
        h_sc[...] = h0_ref[...]

    xa = xa_ref[...]
    prev = prev_sc[...]
    xc = cb_ref[...] + cw_ref[CONV_A - 1:CONV_A, :] * xa
    for k in range(CONV_A - 1):
        xc = xc + cw_ref[k:k + 1, :] * _delay(xa, prev, CONV_A - 1 - k, sh)
    new_prev = xa[rows - (CONV_A - 1) * sh:]
    prev_sc[...] = new_prev
    cso_ref[0] = new_prev

    row = lax.broadcasted_iota(jnp.int32, (rows, HEAD), 0)
    for hd in range(heads):
        cs = slice(hd * HEAD, (hd + 1) * HEAD)
        xh = xc[:, cs]
        xb = xh.astype(BF16)
        r = jax.nn.sigmoid(_dot(xb, wr_ref[hd]) + br_ref[:, cs])
        ig = jax.nn.sigmoid(_dot(xb, wi_ref[hd]) + bi_ref[:, cs])
        a = jnp.exp(-C_RG * r * _softplus(-lam_ref[:, cs]))
        mult = jnp.sqrt(1.0 - a * a)
        if reset_first:
            mult = jnp.where((row < sh) & (tb == 0), 1.0, mult)
        a_sc[:, cs] = a
        b_sc[:, cs] = mult * ig * xh

    if sh == 1:
        width = a_sc.shape[1]
        a3 = a_sc[...].reshape(rows // SUBLANE, SUBLANE, width)
        b3 = b_sc[...].reshape(rows // SUBLANE, SUBLANE, width)
        sub = lax.broadcasted_iota(jnp.int32, a3.shape, 1)
        d = 1
        while d < SUBLANE:
            keep = sub >= d
            b3 = b3 + a3 * jnp.where(keep, pltpu.roll(b3, d, 1), 0.0)
            a3 = a3 * jnp.where(keep, pltpu.roll(a3, d, 1), 1.0)
            d *= 2
        a_sc[...] = a3.reshape(rows, width)
        b_sc[...] = b3.reshape(rows, width)

        def tile_step(k, h):
            sl = pl.ds(pl.multiple_of(k * SUBLANE, SUBLANE), SUBLANE)
            ht = b_sc[sl, :] + a_sc[sl, :] * jnp.broadcast_to(h, (SUBLANE, width))
            b_sc[sl, :] = ht
            return ht[SUBLANE - 1:]

        h = lax.fori_loop(0, rows // SUBLANE, tile_step, h_sc[...], unroll=min(4, rows // SUBLANE))
    else:
        def step(t, h):
            sl = pl.ds(pl.multiple_of(t * sh, sh), sh)
            h = a_sc[sl, :] * h + b_sc[sl, :]
            b_sc[sl, :] = h
            return h

        h = lax.fori_loop(0, tt, step, h_sc[...], unroll=True)
    h_sc[...] = h
    ho_ref[0] = h
    oa_ref[...] = (b_sc[...] * jax.nn.gelu(ga_ref[...])).astype(oa_ref.dtype)


def _rglru(z, a_conv, a_h, prm, l, seg):
    G, sh, T = seg.G, seg.sh, seg.T
    tt = min(T, 512) if sh == 1 else T
    rows, nblk = tt * sh, T // tt
    heads = 8 if sh == 1 else 2
    cwid = heads * HEAD
    ncb = D_A // cwid
    (ac, acl), (ah, ahl) = a_conv, a_h
    gi = (lambda g: 0) if seg.share_state else (lambda g: g)
    row_blk = lambda c, g, tb: g * nblk + tb
    vec = lambda n: pl.BlockSpec((None, n, cwid), lambda c, g, tb: (l, 0, c))
    mat = pl.BlockSpec((None, heads, HEAD, HEAD), lambda c, g, tb: (l, c, 0, 0))
    return pl.pallas_call(
        functools.partial(_rglru_kernel, sh=sh, tt=tt, heads=heads, reset_first=seg.reset_first),
        grid=(ncb, G, nblk),
        in_specs=[pl.BlockSpec((rows, cwid), lambda c, g, tb: (row_blk(c, g, tb), c)),
                  pl.BlockSpec((rows, cwid), lambda c, g, tb: (row_blk(c, g, tb), ncb + c)),
                  pl.BlockSpec((None, None, (CONV_A - 1) * sh, cwid), lambda c, g, tb: (acl, gi(g), 0, c)),
                  pl.BlockSpec((None, None, sh, cwid), lambda c, g, tb: (ahl, gi(g), 0, c)),
                  vec(CONV_A), vec(1), mat, vec(1), mat, vec(1), vec(1)],
        out_specs=[pl.BlockSpec((rows, cwid), lambda c, g, tb: (row_blk(c, g, tb), c)),
                   pl.BlockSpec((1, (CONV_A - 1) * sh, cwid), lambda c, g, tb: (g, 0, c)),
                   pl.BlockSpec((1, sh, cwid), lambda c, g, tb: (g, 0, c))],
        out_shape=[jax.ShapeDtypeStruct((G * T * sh, D_A), BF16),
                   jax.ShapeDtypeStruct((G, (CONV_A - 1) * sh, D_A), F32),
                   jax.ShapeDtypeStruct((G, sh, D_A), F32)],
        scratch_shapes=[pltpu.VMEM(((CONV_A - 1) * sh, cwid), F32),
                        pltpu.VMEM((sh, cwid), F32),
                        pltpu.VMEM((rows, cwid), F32),
                        pltpu.VMEM((rows, cwid), F32)],
        compiler_params=_params(3),
        name="rglru",
    )(z, z, ac, ah, prm["conv_a_w"], prm["conv_a_b"], prm["w_ra"], prm["b_ra"], prm["w_ia"], prm["b_ia"],
      prm["lam_a"])


def _cumsum_rows(x):
    n = x.shape[0]
    row = lax.broadcasted_iota(jnp.int32, x.shape, 0)
    d = 1
    while d < n:
        x = x + jnp.where(row >= d, pltpu.roll(x, d, 0), 0.0)
        d *= 2
    return x


def _gla_pairwise(q, bc, k, v):
    n = q.shape[0]
    row = lax.broadcasted_iota(jnp.int32, (n, 1), 0)
    o = jnp.zeros((n, HEAD), F32)
    for s in range(n):
        diff = jnp.where(row >= s, bc - bc[s:s + 1], 0.0)
        w = q * k[s:s + 1] * jnp.exp(diff)
        a = jnp.where(row >= s, jnp.sum(w, axis=-1, keepdims=True), 0.0)
        o = o + a * v[s:s + 1]
    return o


def _gla_chunk_fast(q, bc, k, v, st):
    c = q.shape[0]
    h = c // 2
    b_mid = bc[h - 1:h]
    bl = jnp.concatenate([bc[:h], bc[h:] - b_mid], axis=0)
    qt = q * jnp.exp(bl)
    qtb = qt.astype(BF16)
    kt = (k * jnp.exp(-bl)).astype(BF16)
    vb = v.astype(BF16)
    row0 = lax.broadcasted_iota(jnp.int32, (h, h), 0)
    col0 = lax.broadcasted_iota(jnp.int32, (h, h), 1)
    a0 = jnp.where(row0 >= col0, _dot_nt(qtb[:h], kt[:h]), 0.0)
    o0 = _dot(a0.astype(BF16), vb[:h])
    kh = (k[:h] * jnp.exp(b_mid - bc[:h])).astype(BF16)
    row1 = lax.broadcasted_iota(jnp.int32, (h, c), 0) + h
    col1 = lax.broadcasted_iota(jnp.int32, (h, c), 1)
    a1 = jnp.where(row1 >= col1, _dot_nt(qtb[h:], jnp.concatenate([kh, kt[h:]], axis=0)), 0.0)
    o1 = _dot(a1.astype(BF16), vb)
    qs = jnp.concatenate([qt[:h], qt[h:] * jnp.exp(b_mid)], axis=0)
    o = jnp.concatenate([o0, o1], axis=0) + _dot_nt(qs.astype(BF16), st.astype(BF16))
    b_end = bc[c - 1:c]
    kd = k * jnp.exp(b_end - bc)
    st_new = st * jnp.exp(b_end) + _dot_tn(vb, kd.astype(BF16))
    return o, st_new


def _gla_chunk(q, bc, k, v, st, base):
    c = q.shape[0]
    o_state = _dot_nt((q * jnp.exp(bc)).astype(BF16), st.astype(BF16))
    parts = [_gla_pairwise(q[i:i + base], bc[i:i + base], k[i:i + base], v[i:i + base])
             for i in range(0, c, base)]
    s = base
    while s < c:
        for p in range(0, c, 2 * s):
            bm = bc[p + s - 1:p + s]
            qh = q[p + s:p + 2 * s] * jnp.exp(bc[p + s:p + 2 * s] - bm)
            kh = k[p:p + s] * jnp.exp(bm - bc[p:p + s])
            att = _dot_nt(qh.astype(BF16), kh.astype(BF16))
            upd = _dot(att.astype(BF16), v[p:p + s].astype(BF16))
            for i in range(s // base):
                idx = (p + s) // base + i
                parts[idx] = parts[idx] + upd[i * base:(i + 1) * base]
        s *= 2
    o = o_state + (parts[0] if len(parts) == 1 else jnp.concatenate(parts, axis=0))
    b_end = bc[c - 1:c]
    kd = k * jnp.exp(b_end - bc)
    st_new = st * jnp.exp(b_end) + _dot_tn(v.astype(BF16), kd.astype(BF16))
    return o, st_new


def _gla_block_kv(q, bc, k, v, s):
    n = q.shape[0]
    o = _dot((q * jnp.exp(bc)).astype(BF16), s.astype(BF16)) + _gla_pairwise(q, bc, k, v)
    b_end = bc[n - 1:n]
    kd = k * jnp.exp(b_end - bc)
    decay_col = jnp.broadcast_to(jnp.exp(b_end), (SUBLANE, HEAD)).T[:, 0:1]
    s_new = s * decay_col + _dot_tn(kd.astype(BF16), v.astype(BF16))
    return o, s_new


def _hgrn2_gates(fz, lb):
    la = jnp.log(jnp.maximum(lb, LB_FLOOR))
    lbv = jnp.log1p(-lb) + jnp.minimum(fz, 0.0) - jnp.log1p(jnp.exp(-jnp.abs(fz)))
    logf = jnp.maximum(la, lbv) + jnp.log1p(jnp.exp(-jnp.abs(la - lbv)))
    k = (1.0 - lb) * jax.nn.sigmoid(-fz)
    return logf, k


def _hgrn2_out(o, gn, gb):
    o = o * lax.rsqrt(jnp.mean(o * o, axis=-1, keepdims=True) + EPS)
    return o * gn * jax.nn.silu(gb)


def _hgrn2_long_kernel(q_ref, fz_ref, v_ref, gb_ref, s0_ref, lb_ref, gn_ref, ob_ref, so_ref, st_sc,
                       *, tt, chunk, base, nblk, hb):
    tb = pl.program_id(2)

    @pl.when(tb == 0)
    def _():
        for hd in range(hb):
            st_sc[hd] = s0_ref[hd].T

    def body(ci, carry):
        sl = pl.ds(pl.multiple_of(ci * chunk, chunk), chunk)
        heads = []
        for hd in range(hb):
            cs = slice(hd * HEAD, (hd + 1) * HEAD)
            logf, k = _hgrn2_gates(fz_ref[sl, cs], lb_ref[:, cs])
            heads.append((hd, cs, _cumsum_rows(logf), k))

        def run(chunk_fn):
            for hd, cs, bc, k in heads:
                o, st_new = chunk_fn(q_ref[sl, cs], bc, k, v_ref[sl, cs], st_sc[hd])
                st_sc[hd] = st_new
                ob_ref[sl, cs] = _hgrn2_out(o, gn_ref[:, cs], gb_ref[sl, cs]).astype(ob_ref.dtype)

        exact = functools.partial(_gla_chunk, base=base)
        if chunk < 2 * GLA_MIN_HALF:
            run(exact)
        else:
            half = chunk // 2
            decay = None
            for _, _, bc, _ in heads:
                d = jnp.maximum(-bc[half - 1:half], bc[half - 1:half] - bc[chunk - 1:chunk])
                decay = d if decay is None else jnp.maximum(decay, d)
            safe = jnp.max(decay) <= GLA_SAFE_DECAY
            pl.when(safe)(lambda: run(_gla_chunk_fast))
            pl.when(jnp.logical_not(safe))(lambda: run(exact))
        return carry

    lax.fori_loop(0, tt // chunk, body, 0)

    @pl.when(tb == nblk - 1)
    def _():
        for hd in range(hb):
            so_ref[0, hd] = st_sc[hd].T


def _hgrn2_wide_kernel(*refs, sh, T, nsb, unroll, first, n_slabs):
    if first:
        q_ref, fz_ref, v_ref, gb_ref, s0_ref, lb_ref, gn_ref, ob_ref, so_ref, o_sc = refs
        if n_slabs > 1:
            so_ref[1:] = jnp.zeros((n_slabs - 1, nsb, 1, HEAD, HEAD), F32)
        so = so_ref.at[0]
    else:
        q_ref, fz_ref, v_ref, gb_ref, s0_ref, lb_ref, gn_ref, _, ob_ref, so, o_sc = refs
    sb = pl.program_id(1)
    lb = lb_ref[...]

    def body(it, carry):
        for u in range(unroll):
            jj = it * unroll + u
            sl = pl.ds(sb * nsb + jj, T, stride=sh)
            logf, k = _hgrn2_gates(fz_ref[sl, :], lb)
            o, s_new = _gla_block_kv(q_ref[sl, :], _cumsum_rows(logf), k, v_ref[sl, :], s0_ref[jj, 0])
            so[jj, 0] = s_new
            o_sc[sl, :] = o
        return carry

    lax.fori_loop(0, nsb // unroll, body, 0)

    @pl.when(sb == sh // nsb - 1)
    def _():
        ob_ref[...] = _hgrn2_out(o_sc[...], gn_ref[...], gb_ref[...]).astype(ob_ref.dtype)


def _hgrn2(z, s0, prm, l, seg, n_slabs=1, slab=0, s_buf=None):
    G, sh, T = seg.G, seg.sh, seg.T
    nseq = G * sh
    (sa, sl_) = s0
    if sh == 1:
        hb = 8
        wid = hb * HEAD
        q0, f0, v0, g0 = (2 * D_A // wid, (2 * D_A + D_B) // wid, (2 * D_A + 2 * D_B) // wid,
                          (2 * D_A + 3 * D_B) // wid)
        tt = min(T, 512)
        chunk = min(tt, 64)
        nblk = T // tt
        gi = (lambda g: 0) if seg.share_state else (lambda g: g)
        col = lambda c0: pl.BlockSpec((tt, wid), lambda h, g, tb: (g * nblk + tb, c0 + h))
        vec = pl.BlockSpec((None, 1, wid), lambda h, g, tb: (l, 0, h))
        return pl.pallas_call(
            functools.partial(_hgrn2_long_kernel, tt=tt, chunk=chunk, base=min(chunk, 16), nblk=nblk, hb=hb),
            grid=(H_B // hb, G, nblk),
            in_specs=[col(q0), col(f0), col(v0), col(g0),
                      pl.BlockSpec((None, None, hb, HEAD, HEAD), lambda h, g, tb: (sl_, gi(g), h, 0, 0)),
                      vec, vec],
            out_specs=[pl.BlockSpec((tt, wid), lambda h, g, tb: (g * nblk + tb, h)),
                       pl.BlockSpec((1, hb, HEAD, HEAD), lambda h, g, tb: (g, h, 0, 0))],
            out_shape=[jax.ShapeDtypeStruct((G * T * sh, D_B), BF16),
                       jax.ShapeDtypeStruct((nseq, H_B, HEAD, HEAD), F32)],
            scratch_shapes=[pltpu.VMEM((hb, HEAD, HEAD), F32)],
            compiler_params=_params(3),
            name="hgrn2_long",
        )(z, z, z, z, sa, prm["lb_all"], prm["gn_b"])
    assert G == 1 and T == SUBLANE
    q0, f0, v0, g0 = 2 * D_A // HEAD, (2 * D_A + D_B) // HEAD, (2 * D_A + 2 * D_B) // HEAD, (2 * D_A + 3 * D_B) // HEAD
    rows = T * sh
    nsb = min(sh, 32)
    unroll = 8 if nsb % 8 == 0 else 1
    col = lambda c0: pl.BlockSpec((rows, HEAD), lambda h, sb: (0, c0 + h))
    vec = pl.BlockSpec((None, 1, HEAD), lambda h, sb: (l, 0, h))
    in_specs = [col(q0), col(f0), col(v0), col(g0),
                pl.BlockSpec((None, nsb, 1, HEAD, HEAD), lambda h, sb: (sl_, sb, h, 0, 0)), vec, vec]
    args = [z, z, z, z, sa, prm["lb_all"], prm["gn_b"]]
    first = s_buf is None
    if first:
        so_spec = pl.BlockSpec((n_slabs, nsb, 1, HEAD, HEAD), lambda h, sb: (0, sb, h, 0, 0))
        aliases = {}
    else:
        in_specs.append(pl.BlockSpec(memory_space=pl.ANY))
        args.append(s_buf)
        so_spec = pl.BlockSpec((None, nsb, 1, HEAD, HEAD), lambda h, sb: (slab, sb, h, 0, 0))
        aliases = {len(args) - 1: 1}
    return pl.pallas_call(
        functools.partial(_hgrn2_wide_kernel, sh=sh, T=T, nsb=nsb, unroll=unroll, first=first, n_slabs=n_slabs),
        grid=(H_B, sh // nsb),
        in_specs=in_specs,
        out_specs=[pl.BlockSpec((rows, HEAD), lambda h, sb: (0, h)), so_spec],
        out_shape=[jax.ShapeDtypeStruct((G * T * sh, D_B), BF16),
                   jax.ShapeDtypeStruct((n_slabs, nseq, H_B, HEAD, HEAD), F32)],
        scratch_shapes=[pltpu.VMEM((rows, HEAD), F32)],
        input_output_aliases=aliases,
        compiler_params=_params(2),
        name="hgrn2_wide",
    )(*args)


def _cmul(ar, ai, br, bi):
    return ar * br - ai * bi, ar * bi + ai * br


def _s5_packed_scan(xr_sc, xi_sc, pwr_sc, pwi_sc, sr0, si0, ar, ai, seg_len):
    lanes = xr_sc.shape[1]
    row = lax.broadcasted_iota(jnp.int32, (SUBLANE, lanes), 0)
    art, ait = jnp.broadcast_to(ar, (SUBLANE, lanes)), jnp.broadcast_to(ai, (SUBLANE, lanes))

    def local(t, carry):
        sr, si = carry
        sl = pl.ds(pl.multiple_of(t * SUBLANE, SUBLANE), SUBLANE)
        pr, pi = _cmul(art, ait, sr, si)
        nr, ni = pr + xr_sc[sl, :], pi + xi_sc[sl, :]
        xr_sc[sl, :] = nr
        xi_sc[sl, :] = ni
        return nr, ni

    init = (jnp.where(row == 0, jnp.broadcast_to(sr0, (SUBLANE, lanes)), 0.0),
            jnp.where(row == 0, jnp.broadcast_to(si0, (SUBLANE, lanes)), 0.0))
    fr, fi = lax.fori_loop(0, seg_len, local, init, unroll=True)

    last = seg_len * SUBLANE - 1
    alr, ali = pwr_sc[last:last + 1, :], pwi_sc[last:last + 1, :]
    cr, ci = fr[0:1], fi[0:1]
    car_r, car_i = jnp.zeros((SUBLANE, lanes), F32), jnp.zeros((SUBLANE, lanes), F32)
    for i in range(1, SUBLANE):
        car_r = jnp.where(row == i, jnp.broadcast_to(cr, (SUBLANE, lanes)), car_r)
        car_i = jnp.where(row == i, jnp.broadcast_to(ci, (SUBLANE, lanes)), car_i)
        pr, pi = _cmul(alr, ali, cr, ci)
        cr, ci = fr[i:i + 1] + pr, fi[i:i + 1] + pi

    def fix(t, carry):
        sl = pl.ds(pl.multiple_of(t * SUBLANE, SUBLANE), SUBLANE)
        dr, di = _cmul(pwr_sc[sl, :], pwi_sc[sl, :], car_r, car_i)
        xr_sc[sl, :] = xr_sc[sl, :] + dr
        xi_sc[sl, :] = xi_sc[sl, :] + di
        return carry

    lax.fori_loop(0, seg_len, fix, 0, unroll=True)
    return cr, ci


def _s5_kernel(*refs, sh, tt):
    if sh == 1:
        (u_ref, sr0_ref, si0_ref, ar_ref, ai_ref, bre_ref, bim_ref, cre_ref, cim_ref, d_ref,
         y_ref, sro_ref, sio_ref, sr_sc, si_sc, xr_sc, xi_sc, up_sc, yp_sc, yt_sc, pwr_sc, pwi_sc) = refs
    else:
        (u_ref, sr0_ref, si0_ref, ar_ref, ai_ref, bre_ref, bim_ref, cre_ref, cim_ref, d_ref,
         y_ref, sro_ref, sio_ref, sr_sc, si_sc, xr_sc, xi_sc) = refs
    tb = pl.program_id(2)
    rows = tt * sh

    @pl.when(tb == 0)
    def _():
        sr_sc[...] = sr0_ref[...]
        si_sc[...] = si0_ref[...]

    if sh == 1:
        seg_len = rows // SUBLANE

        @pl.when((pl.program_id(1) == 0) & (tb == 0))
        def _():
            def grow(t, pw):
                sl = pl.ds(pl.multiple_of(t * SUBLANE, SUBLANE), SUBLANE)
                pwr_sc[sl, :] = jnp.broadcast_to(pw[0], (SUBLANE, S5_LANES))
                pwi_sc[sl, :] = jnp.broadcast_to(pw[1], (SUBLANE, S5_LANES))
                return _cmul(pw[0], pw[1], ar_ref[...], ai_ref[...])

            lax.fori_loop(0, seg_len, grow, (ar_ref[...], ai_ref[...]))

        def pack(t, carry):
            up_sc[pl.ds(pl.multiple_of(t * SUBLANE, SUBLANE), SUBLANE), :] = u_ref[pl.ds(t, SUBLANE, stride=seg_len), :]
            return carry

        lax.fori_loop(0, seg_len, pack, 0, unroll=True)
        u_skip = up_sc[...]
    else:
        u_skip = u_ref[...]
    ub = u_skip.astype(BF16)
    xr_sc[...] = _dot(ub, bre_ref[...])
    xi_sc[...] = _dot(ub, bim_ref[...])

    if sh == 1:
        sr, si = _s5_packed_scan(xr_sc, xi_sc, pwr_sc, pwi_sc, sr_sc[...], si_sc[...], ar_ref[...], ai_ref[...],
                                 seg_len)
    else:
        ar = jnp.broadcast_to(ar_ref[...], (sh, S5_LANES))
        ai = jnp.broadcast_to(ai_ref[...], (sh, S5_LANES))

        def step(t, carry):
            sl = pl.ds(pl.multiple_of(t * sh, sh), sh)
            pr, pi = _cmul(ar, ai, carry[0], carry[1])
            nr, ni = pr + xr_sc[sl, :], pi + xi_sc[sl, :]
            xr_sc[sl, :] = nr
            xi_sc[sl, :] = ni
            return nr, ni

        sr, si = lax.fori_loop(0, tt, step, (sr_sc[...], si_sc[...]), unroll=True)
    sr_sc[...] = sr
    si_sc[...] = si
    sro_ref[0] = sr
    sio_ref[0] = si
    y = (_dot(xr_sc[...].astype(BF16), cre_ref[...]) - _dot(xi_sc[...].astype(BF16), cim_ref[...])
         + d_ref[...] * u_skip)
    out = jax.nn.gelu(y)
    if sh == 1:
        yp_sc[...] = out

        def unpack(t, carry):
            yt_sc[pl.ds(t, SUBLANE, stride=seg_len), :] = yp_sc[pl.ds(pl.multiple_of(t * SUBLANE, SUBLANE), SUBLANE), :]
            return carry

        lax.fori_loop(0, seg_len, unpack, 0, unroll=True)
        out = yt_sc[...]
    y_ref[...] = out.astype(BF16)


def _s5(u, s_re, s_im, prm, j, seg):
    G, sh, T = seg.G, seg.sh, seg.T
    tt = min(T, 1024) if sh == 1 else T
    rows, nblk = tt * sh, T // tt
    ncb = D_C // LANE
    (sr, srl), (si, sil) = s_re, s_im
    gi = (lambda g: 0) if seg.share_state else (lambda g: g)
    st_in = lambda lay: pl.BlockSpec((None, None, sh, S5_LANES), lambda c, g, tb: (lay, gi(g), 0, c))
    st_out = pl.BlockSpec((1, sh, S5_LANES), lambda c, g, tb: (g, 0, c))
    vec = pl.BlockSpec((None, 1, S5_LANES), lambda c, g, tb: (j, 0, c))
    bmat = pl.BlockSpec((None, None, LANE, S5_LANES), lambda c, g, tb: (j, c, 0, 0))
    cmat = pl.BlockSpec((None, None, S5_LANES, LANE), lambda c, g, tb: (j, c, 0, 0))
    scratch = [pltpu.VMEM((sh, S5_LANES), F32), pltpu.VMEM((sh, S5_LANES), F32),
               pltpu.VMEM((rows, S5_LANES), F32), pltpu.VMEM((rows, S5_LANES), F32)]
    if sh == 1:
        scratch += [pltpu.VMEM((rows, LANE), F32)] * 3
        scratch += [pltpu.VMEM((rows, S5_LANES), F32)] * 2
    return pl.pallas_call(
        functools.partial(_s5_kernel, sh=sh, tt=tt),
        grid=(ncb, G, nblk),
        in_specs=[pl.BlockSpec((rows, LANE), lambda c, g, tb: (g * nblk + tb, c)),
                  st_in(srl), st_in(sil), vec, vec, bmat, bmat, cmat, cmat,
                  pl.BlockSpec((None, 1, LANE), lambda c, g, tb: (j, 0, c))],
        out_specs=[pl.BlockSpec((rows, LANE), lambda c, g, tb: (g * nblk + tb, c)), st_out, st_out],
        out_shape=[jax.ShapeDtypeStruct((G * T * sh, D_C), BF16),
                   jax.ShapeDtypeStruct((G, sh, STATE_C), F32),
                   jax.ShapeDtypeStruct((G, sh, STATE_C), F32)],
        scratch_shapes=scratch,
        compiler_params=_params(3),
        name="s5",
    )(u, sr, si, prm["ar"], prm["ai"], prm["bre"], prm["bim"], prm["cre"], prm["cim"], prm["d"])


def _s5_params(lam_re, lam_im, log_dt, bmat_re, bmat_im, cmat_re, cmat_im, d_skip):
    n = lam_re.shape[0]
    lr, li = lam_re.astype(F32), lam_im.astype(F32)
    dt = jnp.exp(log_dt.astype(F32))[..., None]
    mag = jnp.exp(lr * dt)
    ar = mag * jnp.cos(li * dt)
    ai = mag * jnp.sin(li * dt)
    den = lr * lr + li * li
    zr = ((ar - 1.0) * lr + ai * li) / den
    zi = (ai * lr - (ar - 1.0) * li) / den
    br_, bi_ = bmat_re.astype(F32), bmat_im.astype(F32)
    bb_re = zr[..., None] * br_ - zi[..., None] * bi_
    bb_im = zr[..., None] * bi_ + zi[..., None] * br_
    gpb = LANE // GROUP_C
    nb = G_C // gpb
    eye = jnp.eye(gpb, dtype=F32)

    def pack_b(bb):
        bb = bb.reshape(n, nb, gpb, P_C, GROUP_C)
        return jnp.einsum("ag,nbapc->nbacgp", eye, bb).reshape(n, nb, LANE, S5_LANES).astype(BF16)

    def pack_c(cm):
        cm = cm.astype(F32).reshape(n, nb, gpb, GROUP_C, P_C)
        return jnp.einsum("ag,nbacp->nbapgc", eye, cm).reshape(n, nb, S5_LANES, LANE).astype(BF16)

    return {"ar": ar.reshape(n, 1, STATE_C), "ai": ai.reshape(n, 1, STATE_C),
            "bre": pack_b(bb_re), "bim": pack_b(bb_im), "cre": pack_c(cmat_re), "cim": pack_c(cmat_im),
            "d": d_skip.astype(F32).reshape(n, 1, D_C)}


def _trunk(xs, segs, sts, w, n_even, wb=None):
    n = len(xs)
    emit = wb is None
    if emit:
        wb = {k: {} for k in ("w_in_e", "w_out_e", "w_in_c", "w_glu_v", "w_glu_g", "w_up", "w_gate", "w_down")}
    news = [{k: [] for k in ("a_conv", "a_h", "b_S", "c_re", "c_im", "f_conv")} for _ in range(n)]
    s_bufs = [None] * n
    lay = lambda s, name, idx: (sts[s][name], min(idx, sts[s][name].shape[0] - 1))

    def wsel(name, layer, kblocks=(0,)):
        if emit:
            return [(w[name], layer, kb) for kb in kblocks]
        return [(part, 0, 0) for part in wb[name][layer]]

    def mm(wname, layer, xs_sets, n_out, tn_one, tn_many, kblocks=(0,), **kw):
        outs = _matmul(xs_sets, wsel(wname, layer, kblocks), n_out, tn=tn_many if emit else tn_one, emit=emit, **kw)
        if emit:
            outs, wb[wname][layer] = outs
        return outs

    for l in range(DEPTH):
        if l % 2 == 0:
            i = l // 2
            zs = mm("w_in_e", i, [[x] for x in xs], 6 * D_A, 1024, 512, tm=1024, norm_g=(w["g_mix"], l),
                    name="in_proj_even")
            mixed = []
            for s, (z, seg) in enumerate(zip(zs, segs)):
                out_a, nb, hl = _rglru(z, lay(s, "a_conv", i), lay(s, "a_h", i), w, i, seg)
                if seg.sh == 1:
                    out_b, s_new = _hgrn2(z, lay(s, "b_S", i), w, i, seg)
                    news[s]["b_S"].append(s_new)
                else:
                    out_b, s_bufs[s] = _hgrn2(z, lay(s, "b_S", i), w, i, seg, n_slabs=n_even, slab=i,
                                              s_buf=s_bufs[s])
                mixed.append([out_a, out_b])
                news[s]["a_conv"].append(nb)
                news[s]["a_h"].append(hl)
            xs = mm("w_out_e", i, mixed, D_MODEL, D_MODEL, 512, kblocks=(0, 1), tm=512, res_sets=xs,
                    name="out_proj_even")
        else:
            j = l // 2
            us = mm("w_in_c", j, [[x] for x in xs], D_C, D_C, 512, tm=1024, norm_g=(w["g_mix"], l),
                    name="in_proj_odd")
            mixed = []
            for s, (u, seg) in enumerate(zip(us, segs)):
                y, sr, si = _s5(u, lay(s, "c_re", j), lay(s, "c_im", j), w["s5"], j, seg)
                mixed.append([y, y])
                news[s]["c_re"].append(sr)
                news[s]["c_im"].append(si)
            outs = _matmul(mixed, wsel("w_glu_v", j) + wsel("w_glu_g", j), D_MODEL, tm=512,
                           tn=512 if emit else D_MODEL, res_sets=xs, glu=True, emit=emit, name="glu_odd")
            if emit:
                outs, (wv, wg) = outs
                wb["w_glu_v"][j], wb["w_glu_g"][j] = [wv], [wg]
            xs = outs
        ffn = _ffn_up(xs, w["g_ffn"], [lay(s, "f_conv", l) for s in range(n)], wsel("w_up", l)[0][:2],
                      wsel("w_gate", l)[0][:2], w["conv_f_w"], w["conv_f_b"], l, segs, tf=512,
                      emit=emit)
        if emit:
            ffn, (wu, wg) = ffn
            wb["w_up"][l], wb["w_gate"][l] = [wu], [wg]
        xs = mm("w_down", l, [[act] for act, _ in ffn], D_MODEL, 512, 256, tm=1024, res_sets=xs, name="ffn_down")
        for s, (_, tail) in enumerate(ffn):
            news[s]["f_conv"].append(tail)
    for s in range(n):
        if s_bufs[s] is not None:
            news[s]["b_S"] = s_bufs[s]
    return [_rmsnorm(x, w["g_final"]) for x in xs], news, wb


def kernel(x_prompt, x_sample, state_a_conv, state_a_h, state_b_S, state_c_re, state_c_im, state_ffn_conv,
           meta_tokens, g_mix, w_in_e, conv_a_w, conv_a_b, w_ra, b_ra, w_ia, b_ia, lam_a, lb_logits, gn_b,
           w_out_e, w_in_c, lam_re, lam_im, log_dt, bmat_re, bmat_im, cmat_re, cmat_im, d_skip,
           w_glu_v, w_glu_g, g_ffn, w_up, w_gate, conv_f_w, conv_f_b, w_down, g_final):
    n_even, n_odd = w_in_e.shape[0], w_in_c.shape[0]
    batch, seq = x_prompt.shape[0], x_prompt.shape[1]
    dec_batch, dec_seq = x_sample.shape[0], x_sample.shape[1]

    sm = jax.nn.softmax(lb_logits.astype(F32), axis=0)
    lb_all = jnp.clip(jnp.clip(jnp.cumsum(sm, axis=0) - sm[0:1], 0.0, 1.0), 0.0, 1.0)
    w = {
        "g_mix": g_mix.reshape(DEPTH, 1, D_MODEL), "g_ffn": g_ffn.reshape(DEPTH, 1, D_MODEL), "g_final": g_final,
        "w_in_e": w_in_e, "w_out_e": w_out_e,
        "conv_a_w": conv_a_w, "conv_a_b": conv_a_b.reshape(n_even, 1, D_A),
        "w_ra": w_ra.astype(BF16), "b_ra": b_ra.reshape(n_even, 1, D_A),
        "w_ia": w_ia.astype(BF16), "b_ia": b_ia.reshape(n_even, 1, D_A), "lam_a": lam_a.reshape(n_even, 1, D_A),
        "lb_all": lb_all.reshape(n_even, 1, D_B), "gn_b": gn_b.reshape(n_even, 1, D_B),
        "w_in_c": w_in_c, "w_glu_v": w_glu_v, "w_glu_g": w_glu_g,
        "s5": _s5_params(lam_re, lam_im, log_dt, bmat_re, bmat_im, cmat_re, cmat_im, d_skip),
        "w_up": w_up, "w_gate": w_gate, "w_down": w_down,
        "conv_f_w": conv_f_w, "conv_f_b": conv_f_b.reshape(DEPTH, 1, D_FF),
    }

    meta_seg = Seg(G=1, sh=1, T=N_META, reset_first=True, share_state=False)
    zero = {
        "a_conv": jnp.zeros((1, 1, CONV_A - 1, D_A), F32),
        "a_h": jnp.zeros((1, 1, 1, D_A), F32),
        "b_S": jnp.zeros((1, 1, H_B, HEAD, HEAD), F32),
        "c_re": jnp.zeros((1, 1, 1, STATE_C), F32),
        "c_im": jnp.zeros((1, 1, 1, STATE_C), F32),
        "f_conv": jnp.zeros((1, 1, CONV_F - 1, D_FF), F32),
    }

    s_seg = Seg(G=1, sh=dec_batch, T=dec_seq, reset_first=False, share_state=False)
    s_init = {
        "a_conv": jnp.swapaxes(state_a_conv, 1, 2).reshape(n_even, 1, (CONV_A - 1) * dec_batch, D_A),
        "a_h": state_a_h.reshape(n_even, 1, dec_batch, D_A),
        "b_S": state_b_S,
        "c_re": state_c_re.reshape(n_odd, 1, dec_batch, STATE_C),
        "c_im": state_c_im.reshape(n_odd, 1, dec_batch, STATE_C),
        "f_conv": jnp.swapaxes(state_ffn_conv, 1, 2).reshape(DEPTH, 1, (CONV_F - 1) * dec_batch, D_FF),
    }
    (ys, _), (s_st, meta_st), wb = _trunk(
        [jnp.swapaxes(x_sample, 0, 1).reshape(dec_seq * dec_batch, D_MODEL), meta_tokens.astype(F32)],
        [s_seg, meta_seg], [s_init, zero], w, n_even)

    p_seg = Seg(G=batch, sh=1, T=seq, reset_first=False, share_state=True)
    p_init = {k: jnp.stack(v) for k, v in meta_st.items()}
    (yp,), (p_st,), _ = _trunk([x_prompt.reshape(batch * seq, D_MODEL)], [p_seg], [p_init], w, n_even, wb)

    y_prompt = yp.reshape(batch, seq, D_MODEL)
    y_sample = jnp.swapaxes(ys.reshape(dec_seq, dec_batch, D_MODEL), 0, 1)
    p_out = (jnp.stack(p_st["a_conv"]),
             jnp.stack(p_st["a_h"]).reshape(n_even, batch, D_A),
             jnp.stack(p_st["b_S"]),
             jnp.stack(p_st["c_re"]).reshape(n_odd, batch, G_C, P_C),
             jnp.stack(p_st["c_im"]).reshape(n_odd, batch, G_C, P_C),
             jnp.stack(p_st["f_conv"]))
    s_out = (jnp.swapaxes(jnp.stack(s_st["a_conv"]).reshape(n_even, CONV_A - 1, dec_batch, D_A), 1, 2),
             jnp.stack(s_st["a_h"]).reshape(n_even, dec_batch, D_A),
             s_st["b_S"],
             jnp.stack(s_st["c_re"]).reshape(n_odd, dec_batch, G_C, P_C),
             jnp.stack(s_st["c_im"]).reshape(n_odd, dec_batch, G_C, P_C),
             jnp.swapaxes(jnp.stack(s_st["f_conv"]).reshape(DEPTH, CONV_F - 1, dec_batch, D_FF), 1, 2))
    return (y_prompt, y_sample) + p_out + s_out
```
